```python
import functools
import jax, jax.numpy as jnp
from jax import lax
import numpy as np

D_MODEL = 1024
BATCH = 8
SEQ = 2048
DEPTH = 1
DEC_BATCH = 128
DEC_SEQ = 1
PAST_LEN = 8192
PAGE_SIZE = 128

HEAD_DIM = 64
HEADS_A = 8
DILATED_CONFIGS = ((128, 1), (512, 4), (2048, 16))
WINDOW_A = 2048
HEADS_B = 8
KV_HEADS_B = 2
GROUP_B = HEADS_B // KV_HEADS_B
WINDOW_B = 128
BLOCK = 128
ROPE_THETA = 10000.0
NORM_EPS = 1e-6
NEG_INF = -1e30
SCALE = HEAD_DIM ** -0.5

WIDTH_A = HEADS_A * HEAD_DIM
WIDTH_B = HEADS_B * HEAD_DIM
KV_WIDTH_B = KV_HEADS_B * HEAD_DIM
IN_WIDTHS = (WIDTH_A, WIDTH_A, WIDTH_A, WIDTH_B, KV_WIDTH_B, KV_WIDTH_B, D_MODEL, D_MODEL)
D_IN = sum(IN_WIDTHS)
SPLIT_POINTS = tuple(int(v) for v in np.cumsum(IN_WIDTHS)[:-1])

N_KEYS = 128
N_EXPERTS = N_KEYS * N_KEYS
PEER_HEADS = 8
PEER_TOPK = 16
PEER_HALF = 128
PEER_QUERY_DIM = 2 * PEER_HALF
PEER_CHUNK = 128

kernel_name = 'hybrid_dilated_swa_peer_step'


def rms_norm(x, g):
    xf = x.astype(jnp.float32)
    y = xf * lax.rsqrt(jnp.mean(xf * xf, axis=-1, keepdims=True) + NORM_EPS)
    return (y * g.astype(jnp.float32)).astype(x.dtype)


def rotary(x, pos):
    inv = ROPE_THETA ** (-jnp.arange(0, HEAD_DIM, 2, dtype=jnp.float32) / HEAD_DIM)
    ang = pos.astype(jnp.float32)[:, None] * inv[None, :]
    cos, sin = jnp.cos(ang)[:, None, :], jnp.sin(ang)[:, None, :]
    x1, x2 = jnp.split(x.astype(jnp.float32), 2, axis=-1)
    return jnp.concatenate([x1 * cos - x2 * sin, x2 * cos + x1 * sin], axis=-1).astype(x.dtype)


def masked_softmax(s, mask, sink):
    s = jnp.where(mask, s, NEG_INF)
    m = s.max(axis=-1)
    if sink is not None:
        m = jnp.maximum(m, sink)
    e = jnp.exp(s - m[..., None])
    denom = e.sum(axis=-1)
    if sink is not None:
        denom = denom + jnp.exp(sink - m)
    return e / denom[..., None], m + jnp.log(denom)


def band_attention(q, k, v, window, sink):
    n, L, kvh, g, dh = q.shape
    nb = -(-L // BLOCK)
    pad = nb * BLOCK - L
    qb = jnp.pad(q, ((0, 0), (0, pad), (0, 0), (0, 0), (0, 0))).reshape(n, nb, BLOCK, kvh, g, dh)
    kp = jnp.pad(k, ((0, 0), (BLOCK, pad), (0, 0), (0, 0))).reshape(n, nb + 1, BLOCK, kvh, dh)
    vp = jnp.pad(v, ((0, 0), (BLOCK, pad), (0, 0), (0, 0))).reshape(n, nb + 1, BLOCK, kvh, dh)
    kw = jnp.concatenate([kp[:, :-1], kp[:, 1:]], axis=2)
    vw = jnp.concatenate([vp[:, :-1], vp[:, 1:]], axis=2)
    s = jnp.einsum('nbqkgd,nbckd->nbkgqc', qb, kw, preferred_element_type=jnp.float32) * SCALE
    a = jnp.arange(BLOCK)[:, None]
    c = jnp.arange(2 * BLOCK)[None, :]
    off = BLOCK + a - c
    kpos = jnp.arange(nb)[:, None, None] * BLOCK + c[None] - BLOCK
    mask = ((off >= 0) & (off <= window))[None] & (kpos >= 0)
    p, lse = masked_softmax(s, mask[None, :, None, None], sink)
    o = jnp.einsum('nbkgqc,nbckd->nbqkgd', p.astype(vw.dtype), vw, preferred_element_type=jnp.float32)
    o = o.reshape(n, nb * BLOCK, kvh, g, dh)[:, :L].astype(q.dtype)
    lse = lse.transpose(0, 1, 4, 2, 3).reshape(n, nb * BLOCK, kvh, g)[:, :L]
    return o, lse


def gathered_attention(q, kc, vc, idx, sink):
    valid = idx >= 0
    safe = jnp.maximum(idx, 0)
    kg = jnp.take(kc, safe, axis=1)
    vg = jnp.take(vc, safe, axis=1)
    s = jnp.einsum('nskgd,nsjkd->nskgj', q, kg, preferred_element_type=jnp.float32) * SCALE
    p, lse = masked_softmax(s, valid[None, :, None, None, :], sink)
    o = jnp.einsum('nskgj,nsjkd->nskgd', p.astype(vg.dtype), vg, preferred_element_type=jnp.float32)
    return o.astype(q.dtype), lse


def to_strided(x, d):
    n, L = x.shape[:2]
    rest = x.shape[2:]
    return x.reshape(n, L // d, d, *rest).swapaxes(1, 2).reshape(n * d, L // d, *rest)


def from_strided(x, d):
    nd, Ld = x.shape[:2]
    rest = x.shape[2:]
    n = nd // d
    return x.reshape(n, d, Ld, *rest).swapaxes(1, 2).reshape(n, Ld * d, *rest)


def combine_by_denominator(outs, lses):
    w = jax.nn.softmax(jnp.stack(lses).astype(jnp.float32), axis=0)
    o = jnp.einsum('cnshg,cnshgd->nshgd', w, jnp.stack(outs).astype(jnp.float32))
    return o.astype(outs[0].dtype)


def prompt_mixers(qa, ka, va, qb, kb, vb, sink):
    s = qa.shape[1]
    outs, lses = [], []
    for window, dil in DILATED_CONFIGS:
        o, lse = band_attention(to_strided(qa, dil), to_strided(ka, dil), to_strided(va, dil),
                                window // dil, None)
        outs.append(from_strided(o, dil))
        lses.append(from_strided(lse, dil))
    oa = combine_by_denominator(outs, lses)
    ob, _ = band_attention(qb, kb, vb, WINDOW_B, sink[:, :, None])
    keep_a = min(WINDOW_A, s)
    keep_b = min(WINDOW_B, s)
    state = (ka[:, s - keep_a:], va[:, s - keep_a:], kb[:, s - keep_b:], vb[:, s - keep_b:])
    return oa, ob, state


def sample_mixers(qa, ka, va, qb, kb, vb, sink, ck_a, cv_a, ck_b, cv_b):
    s = qa.shape[1]
    kca = jnp.concatenate([ck_a, ka], axis=1)
    vca = jnp.concatenate([cv_a, va], axis=1)
    qi_a = ck_a.shape[1] + jnp.arange(s)
    outs, lses = [], []
    for window, dil in DILATED_CONFIGS:
        idx = qi_a[:, None] - dil * jnp.arange(window // dil + 1)[None, :]
        o, lse = gathered_attention(qa, kca, vca, idx, None)
        outs.append(o)
        lses.append(lse)
    oa = combine_by_denominator(outs, lses)
    kcb = jnp.concatenate([ck_b, kb], axis=1)
    vcb = jnp.concatenate([cv_b, vb], axis=1)
    qi_b = ck_b.shape[1] + jnp.arange(s)
    idx_b = qi_b[:, None] - jnp.arange(WINDOW_B + 1)[None, :]
    ob, _ = gathered_attention(qb, kcb, vcb, idx_b, sink)
    ma, mb = kca.shape[1], kcb.shape[1]
    keep_a = min(WINDOW_A, ma)
    keep_b = min(WINDOW_B, mb)
    state = (kca[:, ma - keep_a:], vca[:, ma - keep_a:], kcb[:, mb - keep_b:], vcb[:, mb - keep_b:])
    return oa, ob, state


def peer(xn, w_q, sub_keys, u_tab, v_tab):
    t = xn.shape[0]
    q = (xn @ w_q).reshape(t, PEER_HEADS, 2, PEER_HALF)
    s = jnp.einsum('thpc,pnc->thpn', q, sub_keys, preferred_element_type=jnp.float32)
    top_s, top_i = lax.top_k(s, PEER_TOPK)
    cand_s = (top_s[:, :, 0, :, None] + top_s[:, :, 1, None, :]).reshape(t, PEER_HEADS, PEER_TOPK ** 2)
    cand_e = (top_i[:, :, 0, :, None] * N_KEYS + top_i[:, :, 1, None, :]).reshape(t, PEER_HEADS, PEER_TOPK ** 2)
    best_s, best_j = lax.top_k(cand_s, PEER_TOPK)
    experts = jnp.take_along_axis(cand_e, best_j, axis=-1).reshape(t, PEER_HEADS * PEER_TOPK)
    gates = jax.nn.softmax(best_s, axis=-1).reshape(t, PEER_HEADS * PEER_TOPK)
    pad = (-t) % PEER_CHUNK
    nc = (t + pad) // PEER_CHUNK
    xp = jnp.pad(xn, ((0, pad), (0, 0))).reshape(nc, PEER_CHUNK, D_MODEL)
    ep = jnp.pad(experts, ((0, pad), (0, 0))).reshape(nc, PEER_CHUNK, PEER_HEADS * PEER_TOPK)
    gp = jnp.pad(gates, ((0, pad), (0, 0))).reshape(nc, PEER_CHUNK, PEER_HEADS * PEER_TOPK)

    def expert_block(args):
        xb, eb, gb = args
        u = jnp.take(u_tab, eb, axis=0)
        act = jax.nn.gelu(jnp.einsum('cd,ced->ce', xb, u, preferred_element_type=jnp.float32),
                          approximate=False)
        v = jnp.take(v_tab, eb, axis=0)
        out = jnp.einsum('ce,ced->cd', (gb * act).astype(v.dtype), v, preferred_element_type=jnp.float32)
        return out.astype(xb.dtype)

    out = lax.map(expert_block, (xp, ep, gp))
    return out.reshape(nc * PEER_CHUNK, D_MODEL)[:t]


def trunk_layer(x, pos, mix_fn, g_mix, w_in, w_a, w_b, w_o, g_ffn, w_pq, sub_keys, u_tab, v_tab):
    n, s, _ = x.shape
    xn = rms_norm(x, g_mix)
    z = xn @ w_in
    qa, ka, va, qb, kb, vb, ga, gb = jnp.split(z, SPLIT_POINTS, axis=-1)
    qa = rotary(qa.reshape(n, s, HEADS_A, HEAD_DIM), pos).reshape(n, s, HEADS_A, 1, HEAD_DIM)
    ka = rotary(ka.reshape(n, s, HEADS_A, HEAD_DIM), pos)
    va = va.reshape(n, s, HEADS_A, HEAD_DIM)
    qb = rotary(qb.reshape(n, s, HEADS_B, HEAD_DIM), pos).reshape(n, s, KV_HEADS_B, GROUP_B, HEAD_DIM)
    kb = rotary(kb.reshape(n, s, KV_HEADS_B, HEAD_DIM), pos)
    vb = vb.reshape(n, s, KV_HEADS_B, HEAD_DIM)
    oa, ob, state = mix_fn(qa, ka, va, qb, kb, vb)
    ya = oa.reshape(n, s, WIDTH_A) @ w_a
    yb = ob.reshape(n, s, WIDTH_B) @ w_b
    merged = jax.nn.sigmoid(ga) * ya + jax.nn.sigmoid(gb) * yb
    h = x + merged @ w_o
    hn = rms_norm(h, g_ffn)
    y = h + peer(hn.reshape(n * s, D_MODEL), w_pq, sub_keys, u_tab, v_tab).reshape(n, s, D_MODEL)
    return y, state


def setup_inputs(seed: int = 0) -> dict:
    key = jax.random.key(seed)
    ks = jax.random.split(key, 20)

    def nrm(k, shape, scale):
        return jax.random.normal(k, shape, jnp.float32) * scale

    win_a = min(WINDOW_A, PAST_LEN)
    win_b = min(WINDOW_B, PAST_LEN)
    return {
        'x_prompt': nrm(ks[0], (BATCH, SEQ, D_MODEL), 1.0),
        'x_sample': nrm(ks[1], (DEC_BATCH, DEC_SEQ, D_MODEL), 1.0),
        'cache_a_k': nrm(ks[2], (DEPTH, DEC_BATCH, win_a, HEADS_A, HEAD_DIM), 1.0),
        'cache_a_v': nrm(ks[3], (DEPTH, DEC_BATCH, win_a, HEADS_A, HEAD_DIM), 1.0),
        'cache_b_k': nrm(ks[4], (DEPTH, DEC_BATCH, win_b, KV_HEADS_B, HEAD_DIM), 1.0),
        'cache_b_v': nrm(ks[5], (DEPTH, DEC_BATCH, win_b, KV_HEADS_B, HEAD_DIM), 1.0),
        'norm_mix': 1.0 + nrm(ks[6], (DEPTH, D_MODEL), 0.1),
        'w_in': nrm(ks[7], (DEPTH, D_MODEL, D_IN), D_MODEL ** -0.5),
        'w_branch_a': nrm(ks[8], (DEPTH, WIDTH_A, D_MODEL), WIDTH_A ** -0.5),
        'w_branch_b': nrm(ks[9], (DEPTH, WIDTH_B, D_MODEL), WIDTH_B ** -0.5),
        'w_out': nrm(ks[10], (DEPTH, D_MODEL, D_MODEL), D_MODEL ** -0.5),
        'sink_b': nrm(ks[11], (DEPTH, HEADS_B), 0.5),
        'norm_ffn': 1.0 + nrm(ks[12], (DEPTH, D_MODEL), 0.1),
        'w_peer_q': nrm(ks[13], (DEPTH, D_MODEL, PEER_HEADS * PEER_QUERY_DIM), D_MODEL ** -0.5),
        'peer_sub_keys': nrm(ks[14], (DEPTH, 2, N_KEYS, PEER_HALF), PEER_HALF ** -0.5),
        'peer_u': nrm(ks[15], (DEPTH, N_EXPERTS, D_MODEL), D_MODEL ** -0.5),
        'peer_v': nrm(ks[16], (DEPTH, N_EXPERTS, D_MODEL), 0.5),
        'norm_final': 1.0 + nrm(ks[17], (D_MODEL,), 0.1),
    }


def reference(x_prompt, x_sample, cache_a_k, cache_a_v, cache_b_k, cache_b_v, norm_mix, w_in,
              w_branch_a, w_branch_b, w_out, sink_b, norm_ffn, w_peer_q, peer_sub_keys, peer_u,
              peer_v, norm_final):
    pos_p = jnp.arange(x_prompt.shape[1])
    pos_s = PAST_LEN + jnp.arange(x_sample.shape[1])
    hp, hs = x_prompt, x_sample
    new_p = ([], [], [], [])
    new_s = ([], [], [], [])
    for l in range(DEPTH):
        weights = (norm_mix[l], w_in[l], w_branch_a[l], w_branch_b[l], w_out[l], norm_ffn[l],
                   w_peer_q[l], peer_sub_keys[l], peer_u[l], peer_v[l])
        sink = sink_b[l].astype(jnp.float32).reshape(KV_HEADS_B, GROUP_B)
        hp, st_p = trunk_layer(hp, pos_p, functools.partial(prompt_mixers, sink=sink), *weights)
        hs, st_s = trunk_layer(hs, pos_s,
                               functools.partial(sample_mixers, sink=sink, ck_a=cache_a_k[l],
                                                 cv_a=cache_a_v[l], ck_b=cache_b_k[l],
                                                 cv_b=cache_b_v[l]),
                               *weights)
        for lst, arr in zip(new_p, st_p):
            lst.append(arr)
        for lst, arr in zip(new_s, st_s):
            lst.append(arr)
    y_prompt = rms_norm(hp, norm_final)
    y_sample = rms_norm(hs, norm_final)
    a_k_p, a_v_p, b_k_p, b_v_p = [jnp.stack(lst) for lst in new_p]
    a_k_s, a_v_s, b_k_s, b_v_s = [jnp.stack(lst) for lst in new_s]
    return (y_prompt, y_sample, a_k_p, a_v_p, b_k_p, b_v_p, a_k_s, a_v_s, b_k_s, b_v_s)
```

```python
import functools
import jax, jax.numpy as jnp
from jax import lax
import numpy as np
from jax.experimental import pallas as pl
from jax.experimental.pallas import tpu as pltpu

D_MODEL = 1024
BATCH = 8
SEQ = 2048
DEPTH = 1
DEC_BATCH = 128
DEC_SEQ = 1
PAST_LEN = 8192

HEAD_DIM = 64
HEADS_A = 8
DILATED_CONFIGS = ((128, 1), (512, 4), (2048, 16))
WINDOW_A = 2048
HEADS_B = 8
KV_HEADS_B = 2
GROUP_B = HEADS_B // KV_HEADS_B
WINDOW_B = 128
BLOCK = 128
ROPE_THETA = 10000.0
NORM_EPS = 1e-6
NEG_INF = -1e30
SCALE = HEAD_DIM ** -0.5

WIDTH_A = HEADS_A * HEAD_DIM
WIDTH_B = HEADS_B * HEAD_DIM
KV_WIDTH_B = KV_HEADS_B * HEAD_DIM
IN_WIDTHS = (WIDTH_A, WIDTH_A, WIDTH_A, WIDTH_B, KV_WIDTH_B, KV_WIDTH_B, D_MODEL, D_MODEL)
D_IN = sum(IN_WIDTHS)
SPLIT_POINTS = tuple(int(v) for v in np.cumsum(IN_WIDTHS)[:-1])

N_KEYS = 128
N_EXPERTS = N_KEYS * N_KEYS
PEER_HEADS = 8
PEER_TOPK = 16
PEER_HALF = 128
PEER_QUERY_DIM = 2 * PEER_HALF
PEER_CHUNK = 128


def rms_norm(x, g):
    xf = x.astype(jnp.float32)
    y = xf * lax.rsqrt(jnp.mean(xf * xf, axis=-1, keepdims=True) + NORM_EPS)
    return (y * g.astype(jnp.float32)).astype(x.dtype)


def rotary(x, pos):
    inv = ROPE_THETA ** (-jnp.arange(0, HEAD_DIM, 2, dtype=jnp.float32) / HEAD_DIM)
    ang = pos.astype(jnp.float32)[:, None] * inv[None, :]
    cos, sin = jnp.cos(ang)[:, None, :], jnp.sin(ang)[:, None, :]
    x1, x2 = jnp.split(x.astype(jnp.float32), 2, axis=-1)
    return jnp.concatenate([x1 * cos - x2 * sin, x2 * cos + x1 * sin], axis=-1).astype(x.dtype)


def masked_softmax(s, mask, sink):
    s = jnp.where(mask, s, NEG_INF)
    m = s.max(axis=-1)
    if sink is not None:
        m = jnp.maximum(m, sink)
    e = jnp.exp(s - m[..., None])
    denom = e.sum(axis=-1)
    if sink is not None:
        denom = denom + jnp.exp(sink - m)
    return e / denom[..., None], m + jnp.log(denom)


def band_attention(q, k, v, window, sink):
    n, L, kvh, g, dh = q.shape
    nb = -(-L // BLOCK)
    pad = nb * BLOCK - L
    qb = jnp.pad(q, ((0, 0), (0, pad), (0, 0), (0, 0), (0, 0))).reshape(n, nb, BLOCK, kvh, g, dh)
    kp = jnp.pad(k, ((0, 0), (BLOCK, pad), (0, 0), (0, 0))).reshape(n, nb + 1, BLOCK, kvh, dh)
    vp = jnp.pad(v, ((0, 0), (BLOCK, pad), (0, 0), (0, 0))).reshape(n, nb + 1, BLOCK, kvh, dh)
    kw = jnp.concatenate([kp[:, :-1], kp[:, 1:]], axis=2)
    vw = jnp.concatenate([vp[:, :-1], vp[:, 1:]], axis=2)
    s = jnp.einsum('nbqkgd,nbckd->nbkgqc', qb, kw, preferred_element_type=jnp.float32) * SCALE
    a = jnp.arange(BLOCK)[:, None]
    c = jnp.arange(2 * BLOCK)[None, :]
    off = BLOCK + a - c
    kpos = jnp.arange(nb)[:, None, None] * BLOCK + c[None] - BLOCK
    mask = ((off >= 0) & (off <= window))[None] & (kpos >= 0)
    p, lse = masked_softmax(s, mask[None, :, None, None], sink)
    o = jnp.einsum('nbkgqc,nbckd->nbqkgd', p.astype(vw.dtype), vw, preferred_element_type=jnp.float32)
    o = o.reshape(n, nb * BLOCK, kvh, g, dh)[:, :L].astype(q.dtype)
    lse = lse.transpose(0, 1, 4, 2, 3).reshape(n, nb * BLOCK, kvh, g)[:, :L]
    return o, lse


def gathered_attention(q, kc, vc, idx, sink):
    valid = idx >= 0
    safe = jnp.maximum(idx, 0)
    kg = jnp.take(kc, safe, axis=1)
    vg = jnp.take(vc, safe, axis=1)
    s = jnp.einsum('nskgd,nsjkd->nskgj', q, kg, preferred_element_type=jnp.float32) * SCALE
    p, lse = masked_softmax(s, valid[None, :, None, None, :], sink)
    o = jnp.einsum('nskgj,nsjkd->nskgd', p.astype(vg.dtype), vg, preferred_element_type=jnp.float32)
    return o.astype(q.dtype), lse


def to_strided(x, d):
    n, L = x.shape[:2]
    rest = x.shape[2:]
    return x.reshape(n, L // d, d, *rest).swapaxes(1, 2).reshape(n * d, L // d, *rest)


def from_strided(x, d):
    nd, Ld = x.shape[:2]
    rest = x.shape[2:]
    n = nd // d
    return x.reshape(n, d, Ld, *rest).swapaxes(1, 2).reshape(n, Ld * d, *rest)


def combine_by_denominator(outs, lses):
    w = jax.nn.softmax(jnp.stack(lses).astype(jnp.float32), axis=0)
    o = jnp.einsum('cnshg,cnshgd->nshgd', w, jnp.stack(outs).astype(jnp.float32))
    return o.astype(outs[0].dtype)


def prompt_mixers(qa, ka, va, qb, kb, vb, sink):
    s = qa.shape[1]
    outs, lses = [], []
    for window, dil in DILATED_CONFIGS:
        o, lse = band_attention(to_strided(qa, dil), to_strided(ka, dil), to_strided(va, dil),
                                window // dil, None)
        outs.append(from_strided(o, dil))
        lses.append(from_strided(lse, dil))
    oa = combine_by_denominator(outs, lses)
    ob, _ = band_attention(qb, kb, vb, WINDOW_B, sink[:, :, None])
    keep_a = min(WINDOW_A, s)
    keep_b = min(WINDOW_B, s)
    state = (ka[:, s - keep_a:], va[:, s - keep_a:], kb[:, s - keep_b:], vb[:, s - keep_b:])
    return oa, ob, state


def sample_mixers(qa, ka, va, qb, kb, vb, sink, ck_a, cv_a, ck_b, cv_b):
    s = qa.shape[1]
    kca = jnp.concatenate([ck_a, ka], axis=1)
    vca = jnp.concatenate([cv_a, va], axis=1)
    qi_a = ck_a.shape[1] + jnp.arange(s)
    outs, lses = [], []
    for window, dil in DILATED_CONFIGS:
        idx = qi_a[:, None] - dil * jnp.arange(window // dil + 1)[None, :]
        o, lse = gathered_attention(qa, kca, vca, idx, None)
        outs.append(o)
        lses.append(lse)
    oa = combine_by_denominator(outs, lses)
    kcb = jnp.concatenate([ck_b, kb], axis=1)
    vcb = jnp.concatenate([cv_b, vb], axis=1)
    qi_b = ck_b.shape[1] + jnp.arange(s)
    idx_b = qi_b[:, None] - jnp.arange(WINDOW_B + 1)[None, :]
    ob, _ = gathered_attention(qb, kcb, vcb, idx_b, sink)
    ma, mb = kca.shape[1], kcb.shape[1]
    keep_a = min(WINDOW_A, ma)
    keep_b = min(WINDOW_B, mb)
    state = (kca[:, ma - keep_a:], vca[:, ma - keep_a:], kcb[:, mb - keep_b:], vcb[:, mb - keep_b:])
    return oa, ob, state


PEER_TOKEN_BLOCK = 512
PEER_EXPERT_BLOCK = 1024
PEER_ROWS = 16
PEER_VMEM_LIMIT = 48 * 1024 * 1024
INV_SQRT2 = 0.7071067811865476


def _peer_expert_body(hnT_ref, u_ref, vT_ref, s1_ref, e1_ref, s2_ref, e2_ref, tau_ref, o_ref,
                      h_scr, a_scr, acc_scr):
    j = pl.program_id(1)
    keys_per_step = PEER_EXPERT_BLOCK // N_KEYS

    @pl.when(j == 0)
    def _():
        acc_scr[...] = jnp.zeros_like(acc_scr)

    h_scr[...] = jnp.dot(u_ref[...], hnT_ref[...], preferred_element_type=jnp.float32)

    for k in range(keys_per_step):
        i1 = j * keys_per_step + k

        def chunk(c, carry, i1=i1, k=k):
            r = pl.multiple_of(c * PEER_ROWS, PEER_ROWS)
            gate = jnp.zeros((PEER_ROWS, PEER_TOKEN_BLOCK), jnp.float32)
            for h in range(PEER_HEADS):
                s1row = s1_ref[h, pl.ds(i1, 1), :]
                e1row = e1_ref[h, pl.ds(i1, 1), :]
                tau = tau_ref[pl.ds(h, 1), :]
                score = s2_ref[h, pl.ds(r, PEER_ROWS), :] + s1row
                val = e2_ref[h, pl.ds(r, PEER_ROWS), :] * e1row
                gate = gate + jnp.where(score >= tau, val, 0.0)
            rows = pl.ds(pl.multiple_of(k * N_KEYS + r, PEER_ROWS), PEER_ROWS)
            x = h_scr[rows, :]
            act = 0.5 * x * (1.0 + lax.erf(x * INV_SQRT2))
            a_scr[rows, :] = (act * gate).astype(jnp.bfloat16)
            return carry

        lax.fori_loop(0, N_KEYS // PEER_ROWS, chunk, 0)

    acc_scr[...] += jnp.dot(vT_ref[...], a_scr[...], preferred_element_type=jnp.float32)

    @pl.when(j == pl.num_programs(1) - 1)
    def _():
        o_ref[...] = acc_scr[...].T


def _peer_experts(hn, s, tau, inv_z, m1, m2, u_bf, vT_bf):
    t = hn.shape[0]
    tb, eb = PEER_TOKEN_BLOCK, PEER_EXPERT_BLOCK
    t_pad = -(-t // tb) * tb
    pad = t_pad - t
    s1 = s[:, :, 0, :]
    s2 = s[:, :, 1, :]
    e1 = jnp.exp(s1 - m1[..., None]) * inv_z[..., None]
    e2 = jnp.exp(s2 - m2[..., None])

    def tr(a):
        return jnp.pad(a, ((0, pad), (0, 0), (0, 0))).transpose(1, 2, 0)

    hnT = jnp.pad(hn, ((0, pad), (0, 0))).astype(jnp.bfloat16).T
    tauT = jnp.pad(tau, ((0, pad), (0, 0))).T
    tok3 = pl.BlockSpec((PEER_HEADS, N_KEYS, tb), lambda i, j: (0, 0, i))
    out = pl.pallas_call(
        _peer_expert_body,
        grid=(t_pad // tb, N_EXPERTS // eb),
        in_specs=[
            pl.BlockSpec((D_MODEL, tb), lambda i, j: (0, i)),
            pl.BlockSpec((eb, D_MODEL), lambda i, j: (j, 0)),
            pl.BlockSpec((D_MODEL, eb), lambda i, j: (0, j)),
            tok3, tok3, tok3, tok3,
            pl.BlockSpec((PEER_HEADS, tb), lambda i, j: (0, i)),
        ],
        out_specs=pl.BlockSpec((tb, D_MODEL), lambda i, j: (i, 0)),
        out_shape=jax.ShapeDtypeStruct((t_pad, D_MODEL), jnp.float32),
        scratch_shapes=[
            pltpu.VMEM((eb, tb), jnp.float32),
            pltpu.VMEM((eb, tb), jnp.bfloat16),
            pltpu.VMEM((D_MODEL, tb), jnp.float32),
        ],
        compiler_params=pltpu.CompilerParams(
            dimension_semantics=("parallel", "arbitrary"),
            vmem_limit_bytes=PEER_VMEM_LIMIT),
        name="peer_experts",
    )(hnT, u_bf, vT_bf, tr(s1), tr(e1), tr(s2), tr(e2), tauT)
    return out[:t]


def peer(xn, w_q, sub_keys, u_bf, vT_bf):
    t = xn.shape[0]
    q = (xn @ w_q).reshape(t, PEER_HEADS, 2, PEER_HALF)
    s = jnp.einsum('thpc,pnc->thpn', q, sub_keys, preferred_element_type=jnp.float32)
    top_s, _ = lax.top_k(s, PEER_TOPK)
    cand_s = (top_s[:, :, 0, :, None] + top_s[:, :, 1, None, :]).reshape(t, PEER_HEADS, PEER_TOPK ** 2)
    best_s, _ = lax.top_k(cand_s, PEER_TOPK)
    tau = best_s[..., PEER_TOPK - 1]
    inv_z = 1.0 / jnp.sum(jnp.exp(best_s - best_s[..., :1]), axis=-1)
    return _peer_experts(xn, s, tau, inv_z, top_s[:, :, 0, 0], top_s[:, :, 1, 0], u_bf, vT_bf)


def trunk_layer(x, pos, mix_fn, g_mix, w_in, w_a, w_b, w_o, g_ffn):
    n, s, _ = x.shape
    xn = rms_norm(x, g_mix)
    z = xn @ w_in
    qa, ka, va, qb, kb, vb, ga, gb = jnp.split(z, SPLIT_POINTS, axis=-1)
    qa = rotary(qa.reshape(n, s, HEADS_A, HEAD_DIM), pos).reshape(n, s, HEADS_A, 1, HEAD_DIM)
    ka = rotary(ka.reshape(n, s, HEADS_A, HEAD_DIM), pos)
    va = va.reshape(n, s, HEADS_A, HEAD_DIM)
    qb = rotary(qb.reshape(n, s, HEADS_B, HEAD_DIM), pos).reshape(n, s, KV_HEADS_B, GROUP_B, HEAD_DIM)
    kb = rotary(kb.reshape(n, s, KV_HEADS_B, HEAD_DIM), pos)
    vb = vb.reshape(n, s, KV_HEADS_B, HEAD_DIM)
    oa, ob, state = mix_fn(qa, ka, va, qb, kb, vb)
    ya = oa.reshape(n, s, WIDTH_A) @ w_a
    yb = ob.reshape(n, s, WIDTH_B) @ w_b
    merged = jax.nn.sigmoid(ga) * ya + jax.nn.sigmoid(gb) * yb
    h = x + merged @ w_o
    hn = rms_norm(h, g_ffn)
    return h, hn, state


def _norm_body(x_ref, g_ref, o_ref):
    x = x_ref[...]
    y = x * lax.rsqrt(jnp.mean(x * x, axis=-1, keepdims=True) + NORM_EPS)
    o_ref[...] = y * g_ref[...]


def _final_norm(x, g):
    shape = x.shape
    x2 = x.reshape(-1, shape[-1])
    t = x2.shape[0]
    tb = min(t, 512)
    out = pl.pallas_call(
        _norm_body,
        grid=(t // tb,),
        in_specs=[pl.BlockSpec((tb, shape[-1]), lambda i: (i, 0)),
                  pl.BlockSpec((1, shape[-1]), lambda i: (0, 0))],
        out_specs=pl.BlockSpec((tb, shape[-1]), lambda i: (i, 0)),
        out_shape=jax.ShapeDtypeStruct(x2.shape, x2.dtype),
    )(x2, g.reshape(1, -1))
    return out.reshape(shape)


def kernel(x_prompt, x_sample, cache_a_k, cache_a_v, cache_b_k, cache_b_v, norm_mix, w_in,
           w_branch_a, w_branch_b, w_out, sink_b, norm_ffn, w_peer_q, peer_sub_keys, peer_u,
           peer_v, norm_final):
    pos_p = jnp.arange(x_prompt.shape[1])
    pos_s = PAST_LEN + jnp.arange(x_sample.shape[1])
    hp, hs = x_prompt, x_sample
    new_p = ([], [], [], [])
    new_s = ([], [], [], [])
    for l in range(DEPTH):
        weights = (norm_mix[l], w_in[l], w_branch_a[l], w_branch_b[l], w_out[l], norm_ffn[l])
        sink = sink_b[l].astype(jnp.float32).reshape(KV_HEADS_B, GROUP_B)
        hp, hnp, st_p = trunk_layer(hp, pos_p, functools.partial(prompt_mixers, sink=sink), *weights)
        hs, hns, st_s = trunk_layer(hs, pos_s,
                                    functools.partial(sample_mixers, sink=sink, ck_a=cache_a_k[l],
                                                      cv_a=cache_a_v[l], ck_b=cache_b_k[l],
                                                      cv_b=cache_b_v[l]),
                                    *weights)
        n_p = hp.shape[0] * hp.shape[1]
        hn_all = jnp.concatenate([hnp.reshape(n_p, D_MODEL), hns.reshape(-1, D_MODEL)], axis=0)
        ffn = peer(hn_all, w_peer_q[l], peer_sub_keys[l], peer_u[l].astype(jnp.bfloat16),
                   peer_v[l].astype(jnp.bfloat16).T)
        hp = hp + ffn[:n_p].reshape(hp.shape)
        hs = hs + ffn[n_p:].reshape(hs.shape)
        for lst, arr in zip(new_p, st_p):
            lst.append(arr)
        for lst, arr in zip(new_s, st_s):
            lst.append(arr)
    y_prompt = _final_norm(hp, norm_final)
    y_sample = _final_norm(hs, norm_final)
    a_k_p, a_v_p, b_k_p, b_v_p = [jnp.stack(lst) for lst in new_p]
    a_k_s, a_v_s, b_k_s, b_v_s = [jnp.stack(lst) for lst in new_s]
    return (y_prompt, y_sample, a_k_p, a_v_p, b_k_p, b_v_p, a_k_s, a_v_s, b_k_s, b_v_s)
```

```python
import functools
import jax, jax.numpy as jnp
from jax import lax
import numpy as np
from jax.experimental import pallas as pl
from jax.experimental.pallas import tpu as pltpu

D_MODEL = 1024
BATCH = 8
SEQ = 2048
DEPTH = 1
DEC_BATCH = 128
DEC_SEQ = 1
PAST_LEN = 8192

HEAD_DIM = 64
HEADS_A = 8
DILATED_CONFIGS = ((128, 1), (512, 4), (2048, 16))
WINDOW_A = 2048
HEADS_B = 8
KV_HEADS_B = 2
GROUP_B = HEADS_B // KV_HEADS_B
WINDOW_B = 128
BLOCK = 128
ROPE_THETA = 10000.0
NORM_EPS = 1e-6
NEG_INF = -1e30
SCALE = HEAD_DIM ** -0.5

WIDTH_A = HEADS_A * HEAD_DIM
WIDTH_B = HEADS_B * HEAD_DIM
KV_WIDTH_B = KV_HEADS_B * HEAD_DIM
IN_WIDTHS = (WIDTH_A, WIDTH_A, WIDTH_A, WIDTH_B, KV_WIDTH_B, KV_WIDTH_B, D_MODEL, D_MODEL)
D_IN = sum(IN_WIDTHS)
SPLIT_POINTS = tuple(int(v) for v in np.cumsum(IN_WIDTHS)[:-1])

N_KEYS = 128
N_EXPERTS = N_KEYS * N_KEYS
PEER_HEADS = 8
PEER_TOPK = 16
PEER_HALF = 128
PEER_QUERY_DIM = 2 * PEER_HALF
PEER_CHUNK = 128


def rms_norm(x, g):
    xf = x.astype(jnp.float32)
    y = xf * lax.rsqrt(jnp.mean(xf * xf, axis=-1, keepdims=True) + NORM_EPS)
    return (y * g.astype(jnp.float32)).astype(x.dtype)


def rotary(x, pos):
    inv = ROPE_THETA ** (-jnp.arange(0, HEAD_DIM, 2, dtype=jnp.float32) / HEAD_DIM)
    ang = pos.astype(jnp.float32)[:, None] * inv[None, :]
    cos, sin = jnp.cos(ang)[:, None, :], jnp.sin(ang)[:, None, :]
    x1, x2 = jnp.split(x.astype(jnp.float32), 2, axis=-1)
    return jnp.concatenate([x1 * cos - x2 * sin, x2 * cos + x1 * sin], axis=-1).astype(x.dtype)


def masked_softmax(s, mask, sink):
    s = jnp.where(mask, s, NEG_INF)
    m = s.max(axis=-1)
    if sink is not None:
        m = jnp.maximum(m, sink)
    e = jnp.exp(s - m[..., None])
    denom = e.sum(axis=-1)
    if sink is not None:
        denom = denom + jnp.exp(sink - m)
    return e / denom[..., None], m + jnp.log(denom)


def band_attention(q, k, v, window, sink):
    n, L, kvh, g, dh = q.shape
    nb = -(-L // BLOCK)
    pad = nb * BLOCK - L
    qb = jnp.pad(q, ((0, 0), (0, pad), (0, 0), (0, 0), (0, 0))).reshape(n, nb, BLOCK, kvh, g, dh)
    kp = jnp.pad(k, ((0, 0), (BLOCK, pad), (0, 0), (0, 0))).reshape(n, nb + 1, BLOCK, kvh, dh)
    vp = jnp.pad(v, ((0, 0), (BLOCK, pad), (0, 0), (0, 0))).reshape(n, nb + 1, BLOCK, kvh, dh)
    kw = jnp.concatenate([kp[:, :-1], kp[:, 1:]], axis=2)
    vw = jnp.concatenate([vp[:, :-1], vp[:, 1:]], axis=2)
    s = jnp.einsum('nbqkgd,nbckd->nbkgqc', qb, kw, preferred_element_type=jnp.float32) * SCALE
    a = jnp.arange(BLOCK)[:, None]
    c = jnp.arange(2 * BLOCK)[None, :]
    off = BLOCK + a - c
    kpos = jnp.arange(nb)[:, None, None] * BLOCK + c[None] - BLOCK
    mask = ((off >= 0) & (off <= window))[None] & (kpos >= 0)
    p, lse = masked_softmax(s, mask[None, :, None, None], sink)
    o = jnp.einsum('nbkgqc,nbckd->nbqkgd', p.astype(vw.dtype), vw, preferred_element_type=jnp.float32)
    o = o.reshape(n, nb * BLOCK, kvh, g, dh)[:, :L].astype(q.dtype)
    lse = lse.transpose(0, 1, 4, 2, 3).reshape(n, nb * BLOCK, kvh, g)[:, :L]
    return o, lse


def gathered_attention(q, kc, vc, idx, sink):
    valid = idx >= 0
    safe = jnp.maximum(idx, 0)
    kg = jnp.take(kc, safe, axis=1)
    vg = jnp.take(vc, safe, axis=1)
    s = jnp.einsum('nskgd,nsjkd->nskgj', q, kg, preferred_element_type=jnp.float32) * SCALE
    p, lse = masked_softmax(s, valid[None, :, None, None, :], sink)
    o = jnp.einsum('nskgj,nsjkd->nskgd', p.astype(vg.dtype), vg, preferred_element_type=jnp.float32)
    return o.astype(q.dtype), lse


def to_strided(x, d):
    n, L = x.shape[:2]
    rest = x.shape[2:]
    return x.reshape(n, L // d, d, *rest).swapaxes(1, 2).reshape(n * d, L // d, *rest)


def from_strided(x, d):
    nd, Ld = x.shape[:2]
    rest = x.shape[2:]
    n = nd // d
    return x.reshape(n, d, Ld, *rest).swapaxes(1, 2).reshape(n, Ld * d, *rest)


def combine_by_denominator(outs, lses):
    w = jax.nn.softmax(jnp.stack(lses).astype(jnp.float32), axis=0)
    o = jnp.einsum('cnshg,cnshgd->nshgd', w, jnp.stack(outs).astype(jnp.float32))
    return o.astype(outs[0].dtype)


def prompt_mixers(qa, ka, va, qb, kb, vb, sink):
    s = qa.shape[1]
    outs, lses = [], []
    for window, dil in DILATED_CONFIGS:
        o, lse = band_attention(to_strided(qa, dil), to_strided(ka, dil), to_strided(va, dil),
                                window // dil, None)
        outs.append(from_strided(o, dil))
        lses.append(from_strided(lse, dil))
    oa = combine_by_denominator(outs, lses)
    ob, _ = band_attention(qb, kb, vb, WINDOW_B, sink[:, :, None])
    keep_a = min(WINDOW_A, s)
    keep_b = min(WINDOW_B, s)
    state = (ka[:, s - keep_a:], va[:, s - keep_a:], kb[:, s - keep_b:], vb[:, s - keep_b:])
    return oa, ob, state


def sample_mixers(qa, ka, va, qb, kb, vb, sink, ck_a, cv_a, ck_b, cv_b):
    s = qa.shape[1]
    kca = jnp.concatenate([ck_a, ka], axis=1)
    vca = jnp.concatenate([cv_a, va], axis=1)
    qi_a = ck_a.shape[1] + jnp.arange(s)
    outs, lses = [], []
    for window, dil in DILATED_CONFIGS:
        idx = qi_a[:, None] - dil * jnp.arange(window // dil + 1)[None, :]
        o, lse = gathered_attention(qa, kca, vca, idx, None)
        outs.append(o)
        lses.append(lse)
    oa = combine_by_denominator(outs, lses)
    kcb = jnp.concatenate([ck_b, kb], axis=1)
    vcb = jnp.concatenate([cv_b, vb], axis=1)
    qi_b = ck_b.shape[1] + jnp.arange(s)
    idx_b = qi_b[:, None] - jnp.arange(WINDOW_B + 1)[None, :]
    ob, _ = gathered_attention(qb, kcb, vcb, idx_b, sink)
    ma, mb = kca.shape[1], kcb.shape[1]
    keep_a = min(WINDOW_A, ma)
    keep_b = min(WINDOW_B, mb)
    state = (kca[:, ma - keep_a:], vca[:, ma - keep_a:], kcb[:, mb - keep_b:], vcb[:, mb - keep_b:])
    return oa, ob, state


LANES = 128
PEER_TOKEN_BLOCK = 512
PEER_EXPERT_BLOCK = 1024
PEER_ROWS = 16
TOP_ROWS = 24
PEER_VMEM_LIMIT = 48 * 1024 * 1024
INV_SQRT2 = 0.7071067811865476


def _peer_route_body(hnT_ref, wqT_ref, keys_ref, a_ref, e_ref, q_scr, s_scr, top_scr, thr_scr, invz_scr):
    tb = PEER_TOKEN_BLOCK
    lane_tiles = tb // LANES
    q_scr[...] = jnp.dot(wqT_ref[...], hnT_ref[...], preferred_element_type=jnp.float32).astype(jnp.bfloat16)
    for hp in range(2 * PEER_HEADS):
        s_scr[hp] = jnp.dot(keys_ref[hp % 2], q_scr[hp * PEER_HALF:(hp + 1) * PEER_HALF, :],
                            preferred_element_type=jnp.float32)

    def take_max(x, iota, n):
        m = jnp.max(x, axis=0, keepdims=True)
        first = jnp.min(jnp.where(x == m, iota, float(n)), axis=0, keepdims=True)
        return m, jnp.where(iota == first, -jnp.inf, x)

    def half_top(u, carry):
        hp = u // lane_tiles
        lanes = pl.ds(pl.multiple_of((u % lane_tiles) * LANES, LANES), LANES)
        iota = lax.broadcasted_iota(jnp.int32, (N_KEYS, LANES), 0).astype(jnp.float32)
        x = s_scr[hp, :, lanes]
        top_scr[hp, PEER_TOPK:, lanes] = jnp.full((TOP_ROWS - PEER_TOPK, LANES), -jnp.inf, jnp.float32)
        for r in range(PEER_TOPK + 1):
            m, x = take_max(x, iota, N_KEYS)
            top_scr[hp, pl.ds(r, 1), lanes] = m
        return carry

    lax.fori_loop(0, 2 * PEER_HEADS * lane_tiles, half_top, 0)

    def pair_top(u, carry):
        h = u // lane_tiles
        lanes = pl.ds(pl.multiple_of((u % lane_tiles) * LANES, LANES), LANES)
        t1 = top_scr[2 * h, :, lanes]
        t2 = top_scr[2 * h + 1, :, lanes]
        x = jnp.concatenate([t1[0:1, :] + t2] + [t1[k:k + 1, :] + t2[0:8, :] for k in range(1, 8)]
                            + [t1[8:, :] + t2[0:1, :]], axis=0)
        n = x.shape[0]
        iota = lax.broadcasted_iota(jnp.int32, (n, LANES), 0).astype(jnp.float32)
        best, x = take_max(x, iota, n)
        z = jnp.ones_like(best)
        v = best
        for r in range(1, PEER_TOPK):
            v, x = take_max(x, iota, n)
            z = z + jnp.exp(v - best)
        nxt, x = take_max(x, iota, n)
        thr_scr[h, :, lanes] = 0.5 * (v + nxt)
        invz_scr[h, :, lanes] = 1.0 / z
        return carry

    lax.fori_loop(0, PEER_HEADS * lane_tiles, pair_top, 0)

    def emit(u, carry):
        h = u // (N_KEYS // PEER_ROWS)
        rows = pl.ds(pl.multiple_of((u % (N_KEYS // PEER_ROWS)) * PEER_ROWS, PEER_ROWS), PEER_ROWS)
        s1 = s_scr[2 * h, rows, :]
        s2 = s_scr[2 * h + 1, rows, :]
        a_ref[2 * h, rows, :] = thr_scr[h] - s1
        a_ref[2 * h + 1, rows, :] = s2
        e_ref[2 * h, rows, :] = jnp.exp(s1 - top_scr[2 * h, pl.ds(0, 1), :]) * invz_scr[h]
        e_ref[2 * h + 1, rows, :] = jnp.exp(s2 - top_scr[2 * h + 1, pl.ds(0, 1), :])
        return carry

    lax.fori_loop(0, PEER_HEADS * (N_KEYS // PEER_ROWS), emit, 0)


def _peer_route(hnT, wqT, keys_bf):
    t_pad = hnT.shape[1]
    tb = PEER_TOKEN_BLOCK
    hp = 2 * PEER_HEADS
    tok3 = pl.BlockSpec((hp, N_KEYS, tb), lambda i: (0, 0, i))
    return pl.pallas_call(
        _peer_route_body,
        grid=(t_pad // tb,),
        in_specs=[
            pl.BlockSpec((D_MODEL, tb), lambda i: (0, i)),
            pl.BlockSpec((hp * PEER_HALF, D_MODEL), lambda i: (0, 0)),
            pl.BlockSpec((2, N_KEYS, PEER_HALF), lambda i: (0, 0, 0)),
        ],
        out_specs=[tok3, tok3],
        out_shape=[jax.ShapeDtypeStruct((hp, N_KEYS, t_pad), jnp.float32)] * 2,
        scratch_shapes=[
            pltpu.VMEM((hp * PEER_HALF, tb), jnp.bfloat16),
            pltpu.VMEM((hp, N_KEYS, tb), jnp.float32),
            pltpu.VMEM((hp, TOP_ROWS, tb), jnp.float32),
            pltpu.VMEM((PEER_HEADS, 1, tb), jnp.float32),
            pltpu.VMEM((PEER_HEADS, 1, tb), jnp.float32),
        ],
        compiler_params=pltpu.CompilerParams(
            dimension_semantics=("parallel",),
            vmem_limit_bytes=PEER_VMEM_LIMIT),
        name="peer_route",
    )(hnT, wqT, keys_bf)


def _peer_expert_body(hnT_ref, u_ref, vT_ref, a_ref, e_ref, res_ref, gfin_ref, o_ref,
                      h_scr, a_scr, acc_scr):
    j = pl.program_id(1)
    keys_per_step = PEER_EXPERT_BLOCK // N_KEYS

    @pl.when(j == 0)
    def _():
        acc_scr[...] = jnp.zeros_like(acc_scr)

    h_scr[...] = jnp.dot(u_ref[...], hnT_ref[...], preferred_element_type=jnp.float32)

    for k in range(keys_per_step):
        i1 = j * keys_per_step + k

        def chunk(c, carry, i1=i1, k=k):
            r = pl.multiple_of(c * PEER_ROWS, PEER_ROWS)
            gate = jnp.zeros((PEER_ROWS, PEER_TOKEN_BLOCK), jnp.float32)
            for h in range(PEER_HEADS):
                need = a_ref[2 * h, pl.ds(i1, 1), :]
                e1row = e_ref[2 * h, pl.ds(i1, 1), :]
                val = e_ref[2 * h + 1, pl.ds(r, PEER_ROWS), :] * e1row
                gate = gate + jnp.where(a_ref[2 * h + 1, pl.ds(r, PEER_ROWS), :] >= need, val, 0.0)
            rows = pl.ds(pl.multiple_of(k * N_KEYS + r, PEER_ROWS), PEER_ROWS)
            x = h_scr[rows, :]
            act = 0.5 * x * (1.0 + lax.erf(x * INV_SQRT2))
            a_scr[rows, :] = (act * gate).astype(jnp.bfloat16)
            return carry

        lax.fori_loop(0, N_KEYS // PEER_ROWS, chunk, 0)

    acc_scr[...] += jnp.dot(vT_ref[...], a_scr[...], preferred_element_type=jnp.float32)

    @pl.when(j == pl.num_programs(1) - 1)
    def _():
        y = res_ref[...] + acc_scr[...].T
        y = y * lax.rsqrt(jnp.mean(y * y, axis=-1, keepdims=True) + NORM_EPS)
        o_ref[...] = y * gfin_ref[...]


def _peer_experts(hnT, a, e, res, g_final, u_bf, vT_bf):
    t_pad = hnT.shape[1]
    tb, eb = PEER_TOKEN_BLOCK, PEER_EXPERT_BLOCK
    tok3 = pl.BlockSpec((2 * PEER_HEADS, N_KEYS, tb), lambda i, j: (0, 0, i))
    return pl.pallas_call(
        _peer_expert_body,
        grid=(t_pad // tb, N_EXPERTS // eb),
        in_specs=[
            pl.BlockSpec((D_MODEL, tb), lambda i, j: (0, i)),
            pl.BlockSpec((eb, D_MODEL), lambda i, j: (j, 0)),
            pl.BlockSpec((D_MODEL, eb), lambda i, j: (0, j)),
            tok3, tok3,
            pl.BlockSpec((tb, D_MODEL), lambda i, j: (i, 0)),
            pl.BlockSpec((1, D_MODEL), lambda i, j: (0, 0)),
        ],
        out_specs=pl.BlockSpec((tb, D_MODEL), lambda i, j: (i, 0)),
        out_shape=jax.ShapeDtypeStruct((t_pad, D_MODEL), jnp.float32),
        scratch_shapes=[
            pltpu.VMEM((eb, tb), jnp.float32),
            pltpu.VMEM((eb, tb), jnp.bfloat16),
            pltpu.VMEM((D_MODEL, tb), jnp.float32),
        ],
        compiler_params=pltpu.CompilerParams(
            dimension_semantics=("parallel", "arbitrary"),
            vmem_limit_bytes=PEER_VMEM_LIMIT),
        name="peer_experts",
    )(hnT, u_bf, vT_bf, a, e, res, g_final.reshape(1, D_MODEL))


def peer_block(h, hn, w_q, sub_keys, u_tab, v_tab, g_final):
    t = h.shape[0]
    pad = -t % PEER_TOKEN_BLOCK
    hnT = jnp.pad(hn, ((0, pad), (0, 0))).astype(jnp.bfloat16).T
    a, e = _peer_route(hnT, w_q.astype(jnp.bfloat16).T, sub_keys.astype(jnp.bfloat16))
    out = _peer_experts(hnT, a, e, jnp.pad(h, ((0, pad), (0, 0))), g_final,
                        u_tab.astype(jnp.bfloat16), v_tab.astype(jnp.bfloat16).T)
    return out[:t]


def trunk_layer(x, pos, mix_fn, g_mix, w_in, w_a, w_b, w_o, g_ffn):
    n, s, _ = x.shape
    xn = rms_norm(x, g_mix)
    z = xn @ w_in
    qa, ka, va, qb, kb, vb, ga, gb = jnp.split(z, SPLIT_POINTS, axis=-1)
    qa = rotary(qa.reshape(n, s, HEADS_A, HEAD_DIM), pos).reshape(n, s, HEADS_A, 1, HEAD_DIM)
    ka = rotary(ka.reshape(n, s, HEADS_A, HEAD_DIM), pos)
    va = va.reshape(n, s, HEADS_A, HEAD_DIM)
    qb = rotary(qb.reshape(n, s, HEADS_B, HEAD_DIM), pos).reshape(n, s, KV_HEADS_B, GROUP_B, HEAD_DIM)
    kb = rotary(kb.reshape(n, s, KV_HEADS_B, HEAD_DIM), pos)
    vb = vb.reshape(n, s, KV_HEADS_B, HEAD_DIM)
    oa, ob, state = mix_fn(qa, ka, va, qb, kb, vb)
    ya = oa.reshape(n, s, WIDTH_A) @ w_a
    yb = ob.reshape(n, s, WIDTH_B) @ w_b
    merged = jax.nn.sigmoid(ga) * ya + jax.nn.sigmoid(gb) * yb
    h = x + merged @ w_o
    hn = rms_norm(h, g_ffn)
    return h, hn, state


def kernel(x_prompt, x_sample, cache_a_k, cache_a_v, cache_b_k, cache_b_v, norm_mix, w_in,
           w_branch_a, w_branch_b, w_out, sink_b, norm_ffn, w_peer_q, peer_sub_keys, peer_u,
           peer_v, norm_final):
    assert DEPTH == 1
    pos_p = jnp.arange(x_prompt.shape[1])
    pos_s = PAST_LEN + jnp.arange(x_sample.shape[1])
    weights = (norm_mix[0], w_in[0], w_branch_a[0], w_branch_b[0], w_out[0], norm_ffn[0])
    sink = sink_b[0].astype(jnp.float32).reshape(KV_HEADS_B, GROUP_B)
    hp, hnp, st_p = trunk_layer(x_prompt, pos_p, functools.partial(prompt_mixers, sink=sink), *weights)
    hs, hns, st_s = trunk_layer(x_sample, pos_s,
                                functools.partial(sample_mixers, sink=sink, ck_a=cache_a_k[0],
                                                  cv_a=cache_a_v[0], ck_b=cache_b_k[0],
                                                  cv_b=cache_b_v[0]),
                                *weights)
    n_p = hp.shape[0] * hp.shape[1]
    h_all = jnp.concatenate([hp.reshape(n_p, D_MODEL), hs.reshape(-1, D_MODEL)], axis=0)
    hn_all = jnp.concatenate([hnp.reshape(n_p, D_MODEL), hns.reshape(-1, D_MODEL)], axis=0)
    y_all = peer_block(h_all, hn_all, w_peer_q[0], peer_sub_keys[0], peer_u[0], peer_v[0], norm_final)
    y_prompt = y_all[:n_p].reshape(hp.shape)
    y_sample = y_all[n_p:].reshape(hs.shape)
    a_k_p, a_v_p, b_k_p, b_v_p = [arr[None] for arr in st_p]
    a_k_s, a_v_s, b_k_s, b_v_s = [arr[None] for arr in st_s]
    return (y_prompt, y_sample, a_k_p, a_v_p, b_k_p, b_v_p, a_k_s, a_v_s, b_k_s, b_v_s)
```

```python
import functools
import jax, jax.numpy as jnp
from jax import lax
import numpy as np
from jax.experimental import pallas as pl
from jax.experimental.pallas import tpu as pltpu

D_MODEL = 1024
BATCH = 8
SEQ = 2048
DEPTH = 1
DEC_BATCH = 128
DEC_SEQ = 1
PAST_LEN = 8192

HEAD_DIM = 64
HEADS_A = 8
DILATED_CONFIGS = ((128, 1), (512, 4), (2048, 16))
WINDOW_A = 2048
HEADS_B = 8
KV_HEADS_B = 2
GROUP_B = HEADS_B // KV_HEADS_B
WINDOW_B = 128
BLOCK = 128
ROPE_THETA = 10000.0
NORM_EPS = 1e-6
NEG_INF = -1e30
SCALE = HEAD_DIM ** -0.5

WIDTH_A = HEADS_A * HEAD_DIM
WIDTH_B = HEADS_B * HEAD_DIM
KV_WIDTH_B = KV_HEADS_B * HEAD_DIM
IN_WIDTHS = (WIDTH_A, WIDTH_A, WIDTH_A, WIDTH_B, KV_WIDTH_B, KV_WIDTH_B, D_MODEL, D_MODEL)
D_IN = sum(IN_WIDTHS)
SPLIT_POINTS = tuple(int(v) for v in np.cumsum(IN_WIDTHS)[:-1])

N_KEYS = 128
N_EXPERTS = N_KEYS * N_KEYS
PEER_HEADS = 8
PEER_TOPK = 16
PEER_HALF = 128
PEER_QUERY_DIM = 2 * PEER_HALF
PEER_CHUNK = 128


def rms_norm(x, g):
    xf = x.astype(jnp.float32)
    y = xf * lax.rsqrt(jnp.mean(xf * xf, axis=-1, keepdims=True) + NORM_EPS)
    return (y * g.astype(jnp.float32)).astype(x.dtype)


def rotary(x, pos):
    inv = ROPE_THETA ** (-jnp.arange(0, HEAD_DIM, 2, dtype=jnp.float32) / HEAD_DIM)
    ang = pos.astype(jnp.float32)[:, None] * inv[None, :]
    cos, sin = jnp.cos(ang)[:, None, :], jnp.sin(ang)[:, None, :]
    x1, x2 = jnp.split(x.astype(jnp.float32), 2, axis=-1)
    return jnp.concatenate([x1 * cos - x2 * sin, x2 * cos + x1 * sin], axis=-1).astype(x.dtype)


def masked_softmax(s, mask, sink):
    s = jnp.where(mask, s, NEG_INF)
    m = s.max(axis=-1)
    if sink is not None:
        m = jnp.maximum(m, sink)
    e = jnp.exp(s - m[..., None])
    denom = e.sum(axis=-1)
    if sink is not None:
        denom = denom + jnp.exp(sink - m)
    return e / denom[..., None], m + jnp.log(denom)


def band_attention(q, k, v, window, sink):
    n, L, kvh, g, dh = q.shape
    nb = -(-L // BLOCK)
    pad = nb * BLOCK - L
    qb = jnp.pad(q, ((0, 0), (0, pad), (0, 0), (0, 0), (0, 0))).reshape(n, nb, BLOCK, kvh, g, dh)
    kp = jnp.pad(k, ((0, 0), (BLOCK, pad), (0, 0), (0, 0))).reshape(n, nb + 1, BLOCK, kvh, dh)
    vp = jnp.pad(v, ((0, 0), (BLOCK, pad), (0, 0), (0, 0))).reshape(n, nb + 1, BLOCK, kvh, dh)
    kw = jnp.concatenate([kp[:, :-1], kp[:, 1:]], axis=2)
    vw = jnp.concatenate([vp[:, :-1], vp[:, 1:]], axis=2)
    s = jnp.einsum('nbqkgd,nbckd->nbkgqc', qb, kw, preferred_element_type=jnp.float32) * SCALE
    a = jnp.arange(BLOCK)[:, None]
    c = jnp.arange(2 * BLOCK)[None, :]
    off = BLOCK + a - c
    kpos = jnp.arange(nb)[:, None, None] * BLOCK + c[None] - BLOCK
    mask = ((off >= 0) & (off <= window))[None] & (kpos >= 0)
    p, lse = masked_softmax(s, mask[None, :, None, None], sink)
    o = jnp.einsum('nbkgqc,nbckd->nbqkgd', p.astype(vw.dtype), vw, preferred_element_type=jnp.float32)
    o = o.reshape(n, nb * BLOCK, kvh, g, dh)[:, :L].astype(q.dtype)
    lse = lse.transpose(0, 1, 4, 2, 3).reshape(n, nb * BLOCK, kvh, g)[:, :L]
    return o, lse


def gathered_attention(q, kc, vc, idx, sink):
    valid = idx >= 0
    safe = jnp.maximum(idx, 0)
    kg = jnp.take(kc, safe, axis=1)
    vg = jnp.take(vc, safe, axis=1)
    s = jnp.einsum('nskgd,nsjkd->nskgj', q, kg, preferred_element_type=jnp.float32) * SCALE
    p, lse = masked_softmax(s, valid[None, :, None, None, :], sink)
    o = jnp.einsum('nskgj,nsjkd->nskgd', p.astype(vg.dtype), vg, preferred_element_type=jnp.float32)
    return o.astype(q.dtype), lse


def to_strided(x, d):
    n, L = x.shape[:2]
    rest = x.shape[2:]
    return x.reshape(n, L // d, d, *rest).swapaxes(1, 2).reshape(n * d, L // d, *rest)


def from_strided(x, d):
    nd, Ld = x.shape[:2]
    rest = x.shape[2:]
    n = nd // d
    return x.reshape(n, d, Ld, *rest).swapaxes(1, 2).reshape(n, Ld * d, *rest)


def combine_by_denominator(outs, lses):
    w = jax.nn.softmax(jnp.stack(lses).astype(jnp.float32), axis=0)
    o = jnp.einsum('cnshg,cnshgd->nshgd', w, jnp.stack(outs).astype(jnp.float32))
    return o.astype(outs[0].dtype)


def prompt_mixers(qa, ka, va, qb, kb, vb, sink):
    s = qa.shape[1]
    outs, lses = [], []
    for window, dil in DILATED_CONFIGS:
        o, lse = band_attention(to_strided(qa, dil), to_strided(ka, dil), to_strided(va, dil),
                                window // dil, None)
        outs.append(from_strided(o, dil))
        lses.append(from_strided(lse, dil))
    oa = combine_by_denominator(outs, lses)
    ob, _ = band_attention(qb, kb, vb, WINDOW_B, sink[:, :, None])
    keep_a = min(WINDOW_A, s)
    keep_b = min(WINDOW_B, s)
    state = (ka[:, s - keep_a:], va[:, s - keep_a:], kb[:, s - keep_b:], vb[:, s - keep_b:])
    return oa, ob, state


SAMPLE_VMEM_LIMIT = 48 * 1024 * 1024
KEYS_PER_BRANCH = 128


def _bf16_round(x):
    return x.astype(jnp.bfloat16).astype(jnp.float32)


def _decode_softmax(s, s_new, sink):
    m = jnp.maximum(jnp.max(s, axis=-1, keepdims=True), s_new)
    if sink is not None:
        m = jnp.maximum(m, sink)
    e = jnp.exp(s - m)
    e_new = jnp.exp(s_new - m)
    denom = jnp.sum(e, axis=-1, keepdims=True) + e_new
    if sink is not None:
        denom = denom + jnp.exp(sink - m)
    return e / denom, e_new / denom, m + jnp.log(denom)


def _sample_mixer_body(qa_ref, ka_ref, va_ref, qb_ref, kb_ref, vb_ref, sink_ref,
                       cak_ref, cav_ref, cbk_ref, cbv_ref,
                       oa_ref, ob_ref, nak_ref, nav_ref, nbk_ref, nbv_ref, s_scr, o_scr):
    f32, bf16 = jnp.float32, jnp.bfloat16
    nt = (((1,), (1,)), ((), ()))
    win_a = cak_ref.shape[2]
    lane_tiles = win_a // LANES

    def as_column(row):
        return jnp.broadcast_to(row, (LANES, row.shape[1])).T

    def shifted(old, col):
        rolled = pltpu.roll(old, old.shape[1] - 1, axis=1)
        pos = lax.broadcasted_iota(jnp.int32, old.shape, 1)
        return jnp.where(pos == old.shape[1] - 1, pltpu.repeat(col, old.shape[1] // LANES, axis=1), rolled)

    qa, ka, va = qa_ref[0], ka_ref[0], va_ref[0]
    own = (lax.broadcasted_iota(jnp.int32, (HEADS_A, WIDTH_A), 1) // HEAD_DIM
           == lax.broadcasted_iota(jnp.int32, (HEADS_A, WIDTH_A), 0))
    s_new = jnp.sum(jnp.where(own, qa * ka, 0.0), axis=-1, keepdims=True) * SCALE
    q_col = as_column(qa)
    k_col = as_column(ka)
    v_col = as_column(va)
    for h in range(HEADS_A):
        rows = slice(h * HEAD_DIM, (h + 1) * HEAD_DIM)
        qh = q_col[rows, :]

        def logits(t, carry, rows=rows, qh=qh, h=h):
            lanes = pl.ds(pl.multiple_of(t * LANES, LANES), LANES)
            s_scr[pl.ds(h, 1), lanes] = jnp.sum(cak_ref[0, rows, lanes] * qh, axis=0, keepdims=True)
            return carry

        lax.fori_loop(0, lane_tiles, logits, 0)
        nak_ref[0, rows, :] = shifted(cak_ref[0, rows, :], k_col[rows, :])

    s = s_scr[...] * SCALE
    pos = lax.broadcasted_iota(jnp.int32, s.shape, 1)
    ps, p_news, lses = [], [], []
    for window, dil in DILATED_CONFIGS:
        reach = (pos >= win_a - window) & (pos % dil == 0)
        p, p_new, lse = _decode_softmax(jnp.where(reach, s, NEG_INF), s_new, None)
        ps.append(p)
        p_news.append(p_new)
        lses.append(lse)
    top = jnp.maximum(jnp.maximum(lses[0], lses[1]), lses[2])
    ws = [jnp.exp(l - top) for l in lses]
    inv = 1.0 / (ws[0] + ws[1] + ws[2])
    s_scr[...] = (ws[0] * ps[0] + ws[1] * ps[1] + ws[2] * ps[2]) * inv
    p_new = (ws[0] * p_news[0] + ws[1] * p_news[1] + ws[2] * p_news[2]) * inv

    for h in range(HEADS_A):
        rows = slice(h * HEAD_DIM, (h + 1) * HEAD_DIM)

        def weighted(t, acc, rows=rows, h=h):
            lanes = pl.ds(pl.multiple_of(t * LANES, LANES), LANES)
            return acc + cav_ref[0, rows, lanes] * s_scr[pl.ds(h, 1), lanes]

        acc = lax.fori_loop(0, lane_tiles, weighted, jnp.zeros((HEAD_DIM, LANES), f32))
        o_scr[rows, :] = jnp.broadcast_to(jnp.sum(acc, axis=1, keepdims=True), (HEAD_DIM, LANES))
        nav_ref[0, rows, :] = shifted(cav_ref[0, rows, :], v_col[rows, :])
    p_new_lanes = jnp.sum(jnp.where(own, p_new, 0.0), axis=0, keepdims=True)
    oa_ref[0] = o_scr[...].T[0:1, :] + p_new_lanes * va

    qb, kb, vb = qb_ref[0], kb_ref[0], vb_ref[0]
    lane = lax.broadcasted_iota(jnp.int32, (1, KV_WIDTH_B), 1)
    heads_per_chunk = KV_WIDTH_B // HEAD_DIM
    q_rows = []
    for h in range(HEADS_B):
        c = h // heads_per_chunk
        piece = qb[:, c * KV_WIDTH_B:(c + 1) * KV_WIDTH_B]
        if h % heads_per_chunk != h // GROUP_B:
            piece = pltpu.roll(piece, HEAD_DIM, axis=1)
        q_rows.append(jnp.where(lane // HEAD_DIM == h // GROUP_B, piece, 0.0))
    q_rows = jnp.concatenate(q_rows, axis=0)
    s = jnp.dot(q_rows.astype(bf16), cbk_ref[0].astype(bf16), preferred_element_type=f32) * SCALE
    s_new = jnp.sum(_bf16_round(q_rows) * _bf16_round(kb), axis=-1, keepdims=True) * SCALE
    p, p_new, _ = _decode_softmax(s, s_new, sink_ref[...])
    o = (lax.dot_general(p.astype(bf16), cbv_ref[0].astype(bf16), nt, preferred_element_type=f32)
         + _bf16_round(p_new) * _bf16_round(vb))
    for c in range(WIDTH_B // KV_WIDTH_B):
        halves = []
        for slot in range(heads_per_chunk):
            h = c * heads_per_chunk + slot
            r = o[h:h + 1, :]
            if h // GROUP_B != slot:
                r = pltpu.roll(r, HEAD_DIM, axis=1)
            halves.append(r)
        ob_ref[0, :, c * KV_WIDTH_B:(c + 1) * KV_WIDTH_B] = jnp.where(lane < HEAD_DIM, halves[0], halves[1])

    nbk_ref[0] = shifted(cbk_ref[0], as_column(kb))
    nbv_ref[0] = shifted(cbv_ref[0], as_column(vb))


def sample_mixers(qa, ka, va, qb, kb, vb, sink, ck_a, cv_a, ck_b, cv_b):
    n, s = qa.shape[:2]
    assert s == 1 and ck_a.shape[1] == WINDOW_A and ck_b.shape[1] == WINDOW_B
    assert KV_WIDTH_B == 2 * HEAD_DIM == LANES
    row = lambda a, w: a.reshape(n, 1, w)
    win = lambda a, w: a.transpose(0, 2, 3, 1).reshape(n, w, a.shape[1])
    unwin = lambda a, like: a.reshape(n, like.shape[2], like.shape[3], like.shape[1]).transpose(0, 3, 1, 2)
    row_spec = lambda w: pl.BlockSpec((1, 1, w), lambda b: (b, 0, 0))
    win_spec = lambda r, w: pl.BlockSpec((1, w, r), lambda b: (b, 0, 0))
    f32 = jnp.float32
    oa, ob, nak, nav, nbk, nbv = pl.pallas_call(
        _sample_mixer_body,
        grid=(n,),
        in_specs=[row_spec(WIDTH_A), row_spec(WIDTH_A), row_spec(WIDTH_A),
                  row_spec(WIDTH_B), row_spec(KV_WIDTH_B), row_spec(KV_WIDTH_B),
                  pl.BlockSpec((HEADS_B, 1), lambda b: (0, 0)),
                  win_spec(WINDOW_A, WIDTH_A), win_spec(WINDOW_A, WIDTH_A),
                  win_spec(WINDOW_B, KV_WIDTH_B), win_spec(WINDOW_B, KV_WIDTH_B)],
        out_specs=[row_spec(WIDTH_A), row_spec(WIDTH_B),
                   win_spec(WINDOW_A, WIDTH_A), win_spec(WINDOW_A, WIDTH_A),
                   win_spec(WINDOW_B, KV_WIDTH_B), win_spec(WINDOW_B, KV_WIDTH_B)],
        out_shape=[jax.ShapeDtypeStruct((n, 1, WIDTH_A), f32), jax.ShapeDtypeStruct((n, 1, WIDTH_B), f32),
                   jax.ShapeDtypeStruct((n, WIDTH_A, WINDOW_A), f32), jax.ShapeDtypeStruct((n, WIDTH_A, WINDOW_A), f32),
                   jax.ShapeDtypeStruct((n, KV_WIDTH_B, WINDOW_B), f32),
                   jax.ShapeDtypeStruct((n, KV_WIDTH_B, WINDOW_B), f32)],
        scratch_shapes=[pltpu.VMEM((HEADS_A, WINDOW_A), f32), pltpu.VMEM((WIDTH_A, LANES), f32)],
        compiler_params=pltpu.CompilerParams(
            dimension_semantics=("parallel",),
            vmem_limit_bytes=SAMPLE_VMEM_LIMIT),
        name="sample_mixers",
    )(row(qa, WIDTH_A), row(ka, WIDTH_A), row(va, WIDTH_A), row(qb, WIDTH_B), row(kb, KV_WIDTH_B),
      row(vb, KV_WIDTH_B), sink.reshape(HEADS_B, 1),
      win(ck_a, WIDTH_A), win(cv_a, WIDTH_A), win(ck_b, KV_WIDTH_B), win(cv_b, KV_WIDTH_B))
    state = (unwin(nak, ck_a), unwin(nav, cv_a), unwin(nbk, ck_b), unwin(nbv, cv_b))
    return oa.reshape(qa.shape), ob.reshape(qb.shape), state


LANES = 128
PEER_TOKEN_BLOCK = 512
PEER_EXPERT_BLOCK = 1024
PEER_ROWS = 16
TOP_ROWS = 24
PEER_VMEM_LIMIT = 48 * 1024 * 1024
INV_SQRT2 = 0.7071067811865476


def _peer_route_body(hnT_ref, wqT_ref, keys_ref, a_ref, e_ref, q_scr, s_scr, top_scr, thr_scr, invz_scr):
    tb = PEER_TOKEN_BLOCK
    lane_tiles = tb // LANES
    q_scr[...] = jnp.dot(wqT_ref[...], hnT_ref[...], preferred_element_type=jnp.float32).astype(jnp.bfloat16)
    for hp in range(2 * PEER_HEADS):
        s_scr[hp] = jnp.dot(keys_ref[hp % 2], q_scr[hp * PEER_HALF:(hp + 1) * PEER_HALF, :],
                            preferred_element_type=jnp.float32)

    def take_max(x, iota, n):
        m = jnp.max(x, axis=0, keepdims=True)
        first = jnp.min(jnp.where(x == m, iota, float(n)), axis=0, keepdims=True)
        return m, jnp.where(iota == first, -jnp.inf, x)

    def half_top(u, carry):
        hp = u // lane_tiles
        lanes = pl.ds(pl.multiple_of((u % lane_tiles) * LANES, LANES), LANES)
        iota = lax.broadcasted_iota(jnp.int32, (N_KEYS, LANES), 0).astype(jnp.float32)
        x = s_scr[hp, :, lanes]
        top_scr[hp, PEER_TOPK:, lanes] = jnp.full((TOP_ROWS - PEER_TOPK, LANES), -jnp.inf, jnp.float32)
        for r in range(PEER_TOPK + 1):
            m, x = take_max(x, iota, N_KEYS)
            top_scr[hp, pl.ds(r, 1), lanes] = m
        return carry

    lax.fori_loop(0, 2 * PEER_HEADS * lane_tiles, half_top, 0)

    def pair_top(u, carry):
        h = u // lane_tiles
        lanes = pl.ds(pl.multiple_of((u % lane_tiles) * LANES, LANES), LANES)
        t1 = top_scr[2 * h, :, lanes]
        t2 = top_scr[2 * h + 1, :, lanes]
        x = jnp.concatenate([t1[0:1, :] + t2] + [t1[k:k + 1, :] + t2[0:8, :] for k in range(1, 8)]
                            + [t1[8:, :] + t2[0:1, :]], axis=0)
        n = x.shape[0]
        iota = lax.broadcasted_iota(jnp.int32, (n, LANES), 0).astype(jnp.float32)
        best, x = take_max(x, iota, n)
        z = jnp.ones_like(best)
        v = best
        for r in range(1, PEER_TOPK):
            v, x = take_max(x, iota, n)
            z = z + jnp.exp(v - best)
        nxt, x = take_max(x, iota, n)
        thr_scr[h, :, lanes] = 0.5 * (v + nxt)
        invz_scr[h, :, lanes] = 1.0 / z
        return carry

    lax.fori_loop(0, PEER_HEADS * lane_tiles, pair_top, 0)

    def emit(u, carry):
        h = u // (N_KEYS // PEER_ROWS)
        rows = pl.ds(pl.multiple_of((u % (N_KEYS // PEER_ROWS)) * PEER_ROWS, PEER_ROWS), PEER_ROWS)
        s1 = s_scr[2 * h, rows, :]
        s2 = s_scr[2 * h + 1, rows, :]
        a_ref[2 * h, rows, :] = thr_scr[h] - s1
        a_ref[2 * h + 1, rows, :] = s2
        e_ref[2 * h, rows, :] = jnp.exp(s1 - top_scr[2 * h, pl.ds(0, 1), :]) * invz_scr[h]
        e_ref[2 * h + 1, rows, :] = jnp.exp(s2 - top_scr[2 * h + 1, pl.ds(0, 1), :])
        return carry

    lax.fori_loop(0, PEER_HEADS * (N_KEYS // PEER_ROWS), emit, 0)


def _peer_route(hnT, wqT, keys_bf):
    t_pad = hnT.shape[1]
    tb = PEER_TOKEN_BLOCK
    hp = 2 * PEER_HEADS
    tok3 = pl.BlockSpec((hp, N_KEYS, tb), lambda i: (0, 0, i))
    return pl.pallas_call(
        _peer_route_body,
        grid=(t_pad // tb,),
        in_specs=[
            pl.BlockSpec((D_MODEL, tb), lambda i: (0, i)),
            pl.BlockSpec((hp * PEER_HALF, D_MODEL), lambda i: (0, 0)),
            pl.BlockSpec((2, N_KEYS, PEER_HALF), lambda i: (0, 0, 0)),
        ],
        out_specs=[tok3, tok3],
        out_shape=[jax.ShapeDtypeStruct((hp, N_KEYS, t_pad), jnp.float32)] * 2,
        scratch_shapes=[
            pltpu.VMEM((hp * PEER_HALF, tb), jnp.bfloat16),
            pltpu.VMEM((hp, N_KEYS, tb), jnp.float32),
            pltpu.VMEM((hp, TOP_ROWS, tb), jnp.float32),
            pltpu.VMEM((PEER_HEADS, 1, tb), jnp.float32),
            pltpu.VMEM((PEER_HEADS, 1, tb), jnp.float32),
        ],
        compiler_params=pltpu.CompilerParams(
            dimension_semantics=("parallel",),
            vmem_limit_bytes=PEER_VMEM_LIMIT),
        name="peer_route",
    )(hnT, wqT, keys_bf)


def _peer_expert_body(hnT_ref, u_ref, vT_ref, a_ref, e_ref, res_ref, gfin_ref, o_ref,
                      h_scr, a_scr, acc_scr):
    j = pl.program_id(1)
    keys_per_step = PEER_EXPERT_BLOCK // N_KEYS

    @pl.when(j == 0)
    def _():
        acc_scr[...] = jnp.zeros_like(acc_scr)

    h_scr[...] = jnp.dot(u_ref[...], hnT_ref[...], preferred_element_type=jnp.float32)

    for k in range(keys_per_step):
        i1 = j * keys_per_step + k

        def chunk(c, carry, i1=i1, k=k):
            r = pl.multiple_of(c * PEER_ROWS, PEER_ROWS)
            gate = jnp.zeros((PEER_ROWS, PEER_TOKEN_BLOCK), jnp.float32)
            for h in range(PEER_HEADS):
                need = a_ref[2 * h, pl.ds(i1, 1), :]
                e1row = e_ref[2 * h, pl.ds(i1, 1), :]
                val = e_ref[2 * h + 1, pl.ds(r, PEER_ROWS), :] * e1row
                gate = gate + jnp.where(a_ref[2 * h + 1, pl.ds(r, PEER_ROWS), :] >= need, val, 0.0)
            rows = pl.ds(pl.multiple_of(k * N_KEYS + r, PEER_ROWS), PEER_ROWS)
            x = h_scr[rows, :]
            act = 0.5 * x * (1.0 + lax.erf(x * INV_SQRT2))
            a_scr[rows, :] = (act * gate).astype(jnp.bfloat16)
            return carry

        lax.fori_loop(0, N_KEYS // PEER_ROWS, chunk, 0)

    acc_scr[...] += jnp.dot(vT_ref[...], a_scr[...], preferred_element_type=jnp.float32)

    @pl.when(j == pl.num_programs(1) - 1)
    def _():
        y = res_ref[...] + acc_scr[...].T
        y = y * lax.rsqrt(jnp.mean(y * y, axis=-1, keepdims=True) + NORM_EPS)
        o_ref[...] = y * gfin_ref[...]


def _peer_experts(hnT, a, e, res, g_final, u_bf, vT_bf):
    t_pad = hnT.shape[1]
    tb, eb = PEER_TOKEN_BLOCK, PEER_EXPERT_BLOCK
    tok3 = pl.BlockSpec((2 * PEER_HEADS, N_KEYS, tb), lambda i, j: (0, 0, i))
    return pl.pallas_call(
        _peer_expert_body,
        grid=(t_pad // tb, N_EXPERTS // eb),
        in_specs=[
            pl.BlockSpec((D_MODEL, tb), lambda i, j: (0, i)),
            pl.BlockSpec((eb, D_MODEL), lambda i, j: (j, 0)),
            pl.BlockSpec((D_MODEL, eb), lambda i, j: (0, j)),
            tok3, tok3,
            pl.BlockSpec((tb, D_MODEL), lambda i, j: (i, 0)),
            pl.BlockSpec((1, D_MODEL), lambda i, j: (0, 0)),
        ],
        out_specs=pl.BlockSpec((tb, D_MODEL), lambda i, j: (i, 0)),
        out_shape=jax.ShapeDtypeStruct((t_pad, D_MODEL), jnp.float32),
        scratch_shapes=[
            pltpu.VMEM((eb, tb), jnp.float32),
            pltpu.VMEM((eb, tb), jnp.bfloat16),
            pltpu.VMEM((D_MODEL, tb), jnp.float32),
        ],
        compiler_params=pltpu.CompilerParams(
            dimension_semantics=("parallel", "arbitrary"),
            vmem_limit_bytes=PEER_VMEM_LIMIT),
        name="peer_experts",
    )(hnT, u_bf, vT_bf, a, e, res, g_final.reshape(1, D_MODEL))


def peer_block(h, hn, w_q, sub_keys, u_tab, v_tab, g_final):
    t = h.shape[0]
    pad = -t % PEER_TOKEN_BLOCK
    hnT = jnp.pad(hn, ((0, pad), (0, 0))).astype(jnp.bfloat16).T
    a, e = _peer_route(hnT, w_q.astype(jnp.bfloat16).T, sub_keys.astype(jnp.bfloat16))
    out = _peer_experts(hnT, a, e, jnp.pad(h, ((0, pad), (0, 0))), g_final,
                        u_tab.astype(jnp.bfloat16), v_tab.astype(jnp.bfloat16).T)
    return out[:t]


def trunk_layer(x, pos, mix_fn, g_mix, w_in, w_a, w_b, w_o, g_ffn):
    n, s, _ = x.shape
    xn = rms_norm(x, g_mix)
    z = xn @ w_in
    qa, ka, va, qb, kb, vb, ga, gb = jnp.split(z, SPLIT_POINTS, axis=-1)
    qa = rotary(qa.reshape(n, s, HEADS_A, HEAD_DIM), pos).reshape(n, s, HEADS_A, 1, HEAD_DIM)
    ka = rotary(ka.reshape(n, s, HEADS_A, HEAD_DIM), pos)
    va = va.reshape(n, s, HEADS_A, HEAD_DIM)
    qb = rotary(qb.reshape(n, s, HEADS_B, HEAD_DIM), pos).reshape(n, s, KV_HEADS_B, GROUP_B, HEAD_DIM)
    kb = rotary(kb.reshape(n, s, KV_HEADS_B, HEAD_DIM), pos)
    vb = vb.reshape(n, s, KV_HEADS_B, HEAD_DIM)
    oa, ob, state = mix_fn(qa, ka, va, qb, kb, vb)
    ya = oa.reshape(n, s, WIDTH_A) @ w_a
    yb = ob.reshape(n, s, WIDTH_B) @ w_b
    merged = jax.nn.sigmoid(ga) * ya + jax.nn.sigmoid(gb) * yb
    h = x + merged @ w_o
    hn = rms_norm(h, g_ffn)
    return h, hn, state


def kernel(x_prompt, x_sample, cache_a_k, cache_a_v, cache_b_k, cache_b_v, norm_mix, w_in,
           w_branch_a, w_branch_b, w_out, sink_b, norm_ffn, w_peer_q, peer_sub_keys, peer_u,
           peer_v, norm_final):
    assert DEPTH == 1
    pos_p = jnp.arange(x_prompt.shape[1])
    pos_s = PAST_LEN + jnp.arange(x_sample.shape[1])
    weights = (norm_mix[0], w_in[0], w_branch_a[0], w_branch_b[0], w_out[0], norm_ffn[0])
    sink = sink_b[0].astype(jnp.float32).reshape(KV_HEADS_B, GROUP_B)
    hp, hnp, st_p = trunk_layer(x_prompt, pos_p, functools.partial(prompt_mixers, sink=sink), *weights)
    hs, hns, st_s = trunk_layer(x_sample, pos_s,
                                functools.partial(sample_mixers, sink=sink, ck_a=cache_a_k[0],
                                                  cv_a=cache_a_v[0], ck_b=cache_b_k[0],
                                                  cv_b=cache_b_v[0]),
                                *weights)
    n_p = hp.shape[0] * hp.shape[1]
    h_all = jnp.concatenate([hp.reshape(n_p, D_MODEL), hs.reshape(-1, D_MODEL)], axis=0)
    hn_all = jnp.concatenate([hnp.reshape(n_p, D_MODEL), hns.reshape(-1, D_MODEL)], axis=0)
    y_all = peer_block(h_all, hn_all, w_peer_q[0], peer_sub_keys[0], peer_u[0], peer_v[0], norm_final)
    y_prompt = y_all[:n_p].reshape(hp.shape)
    y_sample = y_all[n_p:].reshape(hs.shape)
    a_k_p, a_v_p, b_k_p, b_v_p = [arr[None] for arr in st_p]
    a_k_s, a_v_s, b_k_s, b_v_s = [arr[None] for arr in st_s]
    return (y_prompt, y_sample, a_k_p, a_v_p, b_k_p, b_v_p, a_k_s, a_v_s, b_k_s, b_v_s)
```

```python
import functools
import jax, jax.numpy as jnp
from jax import lax
import numpy as np
from jax.experimental import pallas as pl
from jax.experimental.pallas import tpu as pltpu

D_MODEL = 1024
BATCH = 8
SEQ = 2048
DEPTH = 1
DEC_BATCH = 128
DEC_SEQ = 1
PAST_LEN = 8192

HEAD_DIM = 64
HEADS_A = 8
DILATED_CONFIGS = ((128, 1), (512, 4), (2048, 16))
WINDOW_A = 2048
HEADS_B = 8
KV_HEADS_B = 2
GROUP_B = HEADS_B // KV_HEADS_B
WINDOW_B = 128
BLOCK = 128
ROPE_THETA = 10000.0
NORM_EPS = 1e-6
NEG_INF = -1e30
SCALE = HEAD_DIM ** -0.5

WIDTH_A = HEADS_A * HEAD_DIM
WIDTH_B = HEADS_B * HEAD_DIM
KV_WIDTH_B = KV_HEADS_B * HEAD_DIM
IN_WIDTHS = (WIDTH_A, WIDTH_A, WIDTH_A, WIDTH_B, KV_WIDTH_B, KV_WIDTH_B, D_MODEL, D_MODEL)
D_IN = sum(IN_WIDTHS)
SPLIT_POINTS = tuple(int(v) for v in np.cumsum(IN_WIDTHS)[:-1])

N_KEYS = 128
N_EXPERTS = N_KEYS * N_KEYS
PEER_HEADS = 8
PEER_TOPK = 16
PEER_HALF = 128
PEER_QUERY_DIM = 2 * PEER_HALF
PEER_CHUNK = 128


def rms_norm(x, g):
    xf = x.astype(jnp.float32)
    y = xf * lax.rsqrt(jnp.mean(xf * xf, axis=-1, keepdims=True) + NORM_EPS)
    return (y * g.astype(jnp.float32)).astype(x.dtype)


def rotary(x, pos):
    inv = ROPE_THETA ** (-jnp.arange(0, HEAD_DIM, 2, dtype=jnp.float32) / HEAD_DIM)
    ang = pos.astype(jnp.float32)[:, None] * inv[None, :]
    cos, sin = jnp.cos(ang)[:, None, :], jnp.sin(ang)[:, None, :]
    x1, x2 = jnp.split(x.astype(jnp.float32), 2, axis=-1)
    return jnp.concatenate([x1 * cos - x2 * sin, x2 * cos + x1 * sin], axis=-1).astype(x.dtype)


def masked_softmax(s, mask, sink):
    s = jnp.where(mask, s, NEG_INF)
    m = s.max(axis=-1)
    if sink is not None:
        m = jnp.maximum(m, sink)
    e = jnp.exp(s - m[..., None])
    denom = e.sum(axis=-1)
    if sink is not None:
        denom = denom + jnp.exp(sink - m)
    return e / denom[..., None], m + jnp.log(denom)


def band_attention(q, k, v, window, sink):
    n, L, kvh, g, dh = q.shape
    nb = -(-L // BLOCK)
    pad = nb * BLOCK - L
    qb = jnp.pad(q, ((0, 0), (0, pad), (0, 0), (0, 0), (0, 0))).reshape(n, nb, BLOCK, kvh, g, dh)
    kp = jnp.pad(k, ((0, 0), (BLOCK, pad), (0, 0), (0, 0))).reshape(n, nb + 1, BLOCK, kvh, dh)
    vp = jnp.pad(v, ((0, 0), (BLOCK, pad), (0, 0), (0, 0))).reshape(n, nb + 1, BLOCK, kvh, dh)
    kw = jnp.concatenate([kp[:, :-1], kp[:, 1:]], axis=2)
    vw = jnp.concatenate([vp[:, :-1], vp[:, 1:]], axis=2)
    s = jnp.einsum('nbqkgd,nbckd->nbkgqc', qb, kw, preferred_element_type=jnp.float32) * SCALE
    a = jnp.arange(BLOCK)[:, None]
    c = jnp.arange(2 * BLOCK)[None, :]
    off = BLOCK + a - c
    kpos = jnp.arange(nb)[:, None, None] * BLOCK + c[None] - BLOCK
    mask = ((off >= 0) & (off <= window))[None] & (kpos >= 0)
    p, lse = masked_softmax(s, mask[None, :, None, None], sink)
    o = jnp.einsum('nbkgqc,nbckd->nbqkgd', p.astype(vw.dtype), vw, preferred_element_type=jnp.float32)
    o = o.reshape(n, nb * BLOCK, kvh, g, dh)[:, :L].astype(q.dtype)
    lse = lse.transpose(0, 1, 4, 2, 3).reshape(n, nb * BLOCK, kvh, g)[:, :L]
    return o, lse


def gathered_attention(q, kc, vc, idx, sink):
    valid = idx >= 0
    safe = jnp.maximum(idx, 0)
    kg = jnp.take(kc, safe, axis=1)
    vg = jnp.take(vc, safe, axis=1)
    s = jnp.einsum('nskgd,nsjkd->nskgj', q, kg, preferred_element_type=jnp.float32) * SCALE
    p, lse = masked_softmax(s, valid[None, :, None, None, :], sink)
    o = jnp.einsum('nskgj,nsjkd->nskgd', p.astype(vg.dtype), vg, preferred_element_type=jnp.float32)
    return o.astype(q.dtype), lse


def to_strided(x, d):
    n, L = x.shape[:2]
    rest = x.shape[2:]
    return x.reshape(n, L // d, d, *rest).swapaxes(1, 2).reshape(n * d, L // d, *rest)


def from_strided(x, d):
    nd, Ld = x.shape[:2]
    rest = x.shape[2:]
    n = nd // d
    return x.reshape(n, d, Ld, *rest).swapaxes(1, 2).reshape(n, Ld * d, *rest)


def combine_by_denominator(outs, lses):
    w = jax.nn.softmax(jnp.stack(lses).astype(jnp.float32), axis=0)
    o = jnp.einsum('cnshg,cnshgd->nshgd', w, jnp.stack(outs).astype(jnp.float32))
    return o.astype(outs[0].dtype)


def prompt_mixers(qa, ka, va, qb, kb, vb, sink):
    s = qa.shape[1]
    outs, lses = [], []
    for window, dil in DILATED_CONFIGS:
        o, lse = band_attention(to_strided(qa, dil), to_strided(ka, dil), to_strided(va, dil),
                                window // dil, None)
        outs.append(from_strided(o, dil))
        lses.append(from_strided(lse, dil))
    oa = combine_by_denominator(outs, lses)
    ob, _ = band_attention(qb, kb, vb, WINDOW_B, sink[:, :, None])
    keep_a = min(WINDOW_A, s)
    keep_b = min(WINDOW_B, s)
    state = (ka[:, s - keep_a:], va[:, s - keep_a:], kb[:, s - keep_b:], vb[:, s - keep_b:])
    return oa, ob, state


SAMPLE_VMEM_LIMIT = 48 * 1024 * 1024
KEYS_PER_BRANCH = 128


def _bf16_round(x):
    return x.astype(jnp.bfloat16).astype(jnp.float32)


def _decode_softmax(s, s_new, sink):
    m = jnp.maximum(jnp.max(s, axis=-1, keepdims=True), s_new)
    if sink is not None:
        m = jnp.maximum(m, sink)
    e = jnp.exp(s - m)
    e_new = jnp.exp(s_new - m)
    denom = jnp.sum(e, axis=-1, keepdims=True) + e_new
    if sink is not None:
        denom = denom + jnp.exp(sink - m)
    return e / denom, e_new / denom, m + jnp.log(denom)


def _sample_mixer_body(qa_ref, ka_ref, va_ref, qb_ref, kb_ref, vb_ref, sink_ref,
                       cak_ref, cav_ref, cbk_ref, cbv_ref,
                       oa_ref, ob_ref, nak_ref, nav_ref, nbk_ref, nbv_ref, s_scr, o_scr):
    f32, bf16 = jnp.float32, jnp.bfloat16
    nt = (((1,), (1,)), ((), ()))
    win_a = cak_ref.shape[2]
    lane_tiles = win_a // LANES

    def as_column(row):
        return jnp.broadcast_to(row, (LANES, row.shape[1])).T

    def shifted(old, col):
        rolled = pltpu.roll(old, old.shape[1] - 1, axis=1)
        pos = lax.broadcasted_iota(jnp.int32, old.shape, 1)
        return jnp.where(pos == old.shape[1] - 1, jnp.tile(col, (1, old.shape[1] // LANES)), rolled)

    qa, ka, va = qa_ref[0], ka_ref[0], va_ref[0]
    own = (lax.broadcasted_iota(jnp.int32, (HEADS_A, WIDTH_A), 1) // HEAD_DIM
           == lax.broadcasted_iota(jnp.int32, (HEADS_A, WIDTH_A), 0))
    s_new = jnp.sum(jnp.where(own, qa * ka, 0.0), axis=-1, keepdims=True) * SCALE
    q_col = as_column(qa)
    k_col = as_column(ka)
    v_col = as_column(va)
    for h in range(HEADS_A):
        rows = slice(h * HEAD_DIM, (h + 1) * HEAD_DIM)
        qh = q_col[rows, :]

        def logits(t, carry, rows=rows, qh=qh, h=h):
            lanes = pl.ds(pl.multiple_of(t * LANES, LANES), LANES)
            s_scr[pl.ds(h, 1), lanes] = jnp.sum(cak_ref[0, rows, lanes] * qh, axis=0, keepdims=True)
            return carry

        lax.fori_loop(0, lane_tiles, logits, 0)
        nak_ref[0, rows, :] = shifted(cak_ref[0, rows, :], k_col[rows, :])

    s = s_scr[...] * SCALE
    pos = lax.broadcasted_iota(jnp.int32, s.shape, 1)
    ps, p_news, lses = [], [], []
    for window, dil in DILATED_CONFIGS:
        reach = (pos >= win_a - window) & (pos % dil == 0)
        p, p_new, lse = _decode_softmax(jnp.where(reach, s, NEG_INF), s_new, None)
        ps.append(p)
        p_news.append(p_new)
        lses.append(lse)
    top = jnp.maximum(jnp.maximum(lses[0], lses[1]), lses[2])
    ws = [jnp.exp(l - top) for l in lses]
    inv = 1.0 / (ws[0] + ws[1] + ws[2])
    s_scr[...] = (ws[0] * ps[0] + ws[1] * ps[1] + ws[2] * ps[2]) * inv
    p_new = (ws[0] * p_news[0] + ws[1] * p_news[1] + ws[2] * p_news[2]) * inv

    for h in range(HEADS_A):
        rows = slice(h * HEAD_DIM, (h + 1) * HEAD_DIM)

        def weighted(t, acc, rows=rows, h=h):
            lanes = pl.ds(pl.multiple_of(t * LANES, LANES), LANES)
            return acc + cav_ref[0, rows, lanes] * s_scr[pl.ds(h, 1), lanes]

        acc = lax.fori_loop(0, lane_tiles, weighted, jnp.zeros((HEAD_DIM, LANES), f32))
        o_scr[rows, :] = jnp.broadcast_to(jnp.sum(acc, axis=1, keepdims=True), (HEAD_DIM, LANES))
        nav_ref[0, rows, :] = shifted(cav_ref[0, rows, :], v_col[rows, :])
    p_new_lanes = jnp.sum(jnp.where(own, p_new, 0.0), axis=0, keepdims=True)
    oa_ref[0] = o_scr[...].T[0:1, :] + p_new_lanes * va

    qb, kb, vb = qb_ref[0], kb_ref[0], vb_ref[0]
    lane = lax.broadcasted_iota(jnp.int32, (1, KV_WIDTH_B), 1)
    heads_per_chunk = KV_WIDTH_B // HEAD_DIM
    q_rows = []
    for h in range(HEADS_B):
        c = h // heads_per_chunk
        piece = qb[:, c * KV_WIDTH_B:(c + 1) * KV_WIDTH_B]
        if h % heads_per_chunk != h // GROUP_B:
            piece = pltpu.roll(piece, HEAD_DIM, axis=1)
        q_rows.append(jnp.where(lane // HEAD_DIM == h // GROUP_B, piece, 0.0))
    q_rows = jnp.concatenate(q_rows, axis=0)
    s = jnp.dot(q_rows.astype(bf16), cbk_ref[0].astype(bf16), preferred_element_type=f32) * SCALE
    s_new = jnp.sum(_bf16_round(q_rows) * _bf16_round(kb), axis=-1, keepdims=True) * SCALE
    p, p_new, _ = _decode_softmax(s, s_new, sink_ref[...])
    o = (lax.dot_general(p.astype(bf16), cbv_ref[0].astype(bf16), nt, preferred_element_type=f32)
         + _bf16_round(p_new) * _bf16_round(vb))
    for c in range(WIDTH_B // KV_WIDTH_B):
        halves = []
        for slot in range(heads_per_chunk):
            h = c * heads_per_chunk + slot
            r = o[h:h + 1, :]
            if h // GROUP_B != slot:
                r = pltpu.roll(r, HEAD_DIM, axis=1)
            halves.append(r)
        ob_ref[0, :, c * KV_WIDTH_B:(c + 1) * KV_WIDTH_B] = jnp.where(lane < HEAD_DIM, halves[0], halves[1])

    nbk_ref[0] = shifted(cbk_ref[0], as_column(kb))
    nbv_ref[0] = shifted(cbv_ref[0], as_column(vb))


def sample_mixers(qa, ka, va, qb, kb, vb, sink, ck_a, cv_a, ck_b, cv_b):
    n, s = qa.shape[:2]
    assert s == 1 and ck_a.shape[1] == WINDOW_A and ck_b.shape[1] == WINDOW_B
    assert KV_WIDTH_B == 2 * HEAD_DIM == LANES
    row = lambda a, w: a.reshape(n, 1, w)
    win = lambda a, w: a.transpose(0, 2, 3, 1).reshape(n, w, a.shape[1])
    unwin = lambda a, like: a.reshape(n, like.shape[2], like.shape[3], like.shape[1]).transpose(0, 3, 1, 2)
    row_spec = lambda w: pl.BlockSpec((1, 1, w), lambda b: (b, 0, 0))
    win_spec = lambda r, w: pl.BlockSpec((1, w, r), lambda b: (b, 0, 0))
    f32 = jnp.float32
    oa, ob, nak, nav, nbk, nbv = pl.pallas_call(
        _sample_mixer_body,
        grid=(n,),
        in_specs=[row_spec(WIDTH_A), row_spec(WIDTH_A), row_spec(WIDTH_A),
                  row_spec(WIDTH_B), row_spec(KV_WIDTH_B), row_spec(KV_WIDTH_B),
                  pl.BlockSpec((HEADS_B, 1), lambda b: (0, 0)),
                  win_spec(WINDOW_A, WIDTH_A), win_spec(WINDOW_A, WIDTH_A),
                  win_spec(WINDOW_B, KV_WIDTH_B), win_spec(WINDOW_B, KV_WIDTH_B)],
        out_specs=[row_spec(WIDTH_A), row_spec(WIDTH_B),
                   win_spec(WINDOW_A, WIDTH_A), win_spec(WINDOW_A, WIDTH_A),
                   win_spec(WINDOW_B, KV_WIDTH_B), win_spec(WINDOW_B, KV_WIDTH_B)],
        out_shape=[jax.ShapeDtypeStruct((n, 1, WIDTH_A), f32), jax.ShapeDtypeStruct((n, 1, WIDTH_B), f32),
                   jax.ShapeDtypeStruct((n, WIDTH_A, WINDOW_A), f32), jax.ShapeDtypeStruct((n, WIDTH_A, WINDOW_A), f32),
                   jax.ShapeDtypeStruct((n, KV_WIDTH_B, WINDOW_B), f32),
                   jax.ShapeDtypeStruct((n, KV_WIDTH_B, WINDOW_B), f32)],
        scratch_shapes=[pltpu.VMEM((HEADS_A, WINDOW_A), f32), pltpu.VMEM((WIDTH_A, LANES), f32)],
        compiler_params=pltpu.CompilerParams(
            dimension_semantics=("parallel",),
            vmem_limit_bytes=SAMPLE_VMEM_LIMIT),
        name="sample_mixers",
    )(row(qa, WIDTH_A), row(ka, WIDTH_A), row(va, WIDTH_A), row(qb, WIDTH_B), row(kb, KV_WIDTH_B),
      row(vb, KV_WIDTH_B), sink.reshape(HEADS_B, 1),
      win(ck_a, WIDTH_A), win(cv_a, WIDTH_A), win(ck_b, KV_WIDTH_B), win(cv_b, KV_WIDTH_B))
    state = (unwin(nak, ck_a), unwin(nav, cv_a), unwin(nbk, ck_b), unwin(nbv, cv_b))
    return oa.reshape(qa.shape), ob.reshape(qb.shape), state


LANES = 128
PEER_TOKEN_BLOCK = 512
PEER_EXPERT_BLOCK = 1024
PEER_ROWS = 16
PEER_PIPE_ROWS = 256
ROUTE_STREAMS = 2
TOP_ROWS = 24
PEER_VMEM_LIMIT = 48 * 1024 * 1024
INV_SQRT2 = 0.7071067811865476


def _peer_route_body(hnT_ref, wqT_ref, keys_ref, a_ref, e_ref, q_scr, s_scr, top_scr, thr_scr, invz_scr):
    tb = PEER_TOKEN_BLOCK
    lane_tiles = tb // LANES
    q_scr[...] = jnp.dot(wqT_ref[...], hnT_ref[...], preferred_element_type=jnp.float32).astype(jnp.bfloat16)
    for hp in range(2 * PEER_HEADS):
        s_scr[hp] = jnp.dot(keys_ref[hp % 2], q_scr[hp * PEER_HALF:(hp + 1) * PEER_HALF, :],
                            preferred_element_type=jnp.float32)

    def take_max(x, iota, n):
        m = jnp.max(x, axis=0, keepdims=True)
        first = jnp.min(jnp.where(x == m, iota, float(n)), axis=0, keepdims=True)
        return m, jnp.where(iota == first, -jnp.inf, x)

    groups = lane_tiles // ROUTE_STREAMS

    def tile_lanes(u, k):
        return pl.ds(pl.multiple_of(((u % groups) * ROUTE_STREAMS + k) * LANES, LANES), LANES)

    def half_top(u, carry):
        hp = u // groups
        iota = lax.broadcasted_iota(jnp.int32, (N_KEYS, LANES), 0).astype(jnp.float32)
        lanes = [tile_lanes(u, k) for k in range(ROUTE_STREAMS)]
        xs = [s_scr[hp, :, ln] for ln in lanes]
        for ln in lanes:
            top_scr[hp, PEER_TOPK:, ln] = jnp.full((TOP_ROWS - PEER_TOPK, LANES), -jnp.inf, jnp.float32)
        for r in range(PEER_TOPK + 1):
            for k, ln in enumerate(lanes):
                m, xs[k] = take_max(xs[k], iota, N_KEYS)
                top_scr[hp, pl.ds(r, 1), ln] = m
        return carry

    lax.fori_loop(0, 2 * PEER_HEADS * groups, half_top, 0)

    def pair_top(u, carry):
        h = u // groups
        lanes = [tile_lanes(u, k) for k in range(ROUTE_STREAMS)]
        xs = []
        for ln in lanes:
            t1 = top_scr[2 * h, :, ln]
            t2 = top_scr[2 * h + 1, :, ln]
            xs.append(jnp.concatenate([t1[0:1, :] + t2] + [t1[k:k + 1, :] + t2[0:8, :] for k in range(1, 8)]
                                      + [t1[8:, :] + t2[0:1, :]], axis=0))
        n = xs[0].shape[0]
        iota = lax.broadcasted_iota(jnp.int32, (n, LANES), 0).astype(jnp.float32)
        best, v, z = [None] * ROUTE_STREAMS, [None] * ROUTE_STREAMS, [None] * ROUTE_STREAMS
        for r in range(PEER_TOPK):
            for k in range(ROUTE_STREAMS):
                v[k], xs[k] = take_max(xs[k], iota, n)
                if r == 0:
                    best[k], z[k] = v[k], jnp.ones_like(v[k])
                else:
                    z[k] = z[k] + jnp.exp(v[k] - best[k])
        for k, ln in enumerate(lanes):
            nxt, _ = take_max(xs[k], iota, n)
            thr_scr[h, :, ln] = 0.5 * (v[k] + nxt)
            invz_scr[h, :, ln] = 1.0 / z[k]
        return carry

    lax.fori_loop(0, PEER_HEADS * groups, pair_top, 0)

    def emit(u, carry):
        h = u // (N_KEYS // PEER_ROWS)
        rows = pl.ds(pl.multiple_of((u % (N_KEYS // PEER_ROWS)) * PEER_ROWS, PEER_ROWS), PEER_ROWS)
        s1 = s_scr[2 * h, rows, :]
        s2 = s_scr[2 * h + 1, rows, :]
        a_ref[2 * h, rows, :] = thr_scr[h] - s1
        a_ref[2 * h + 1, rows, :] = s2
        e_ref[2 * h, rows, :] = jnp.exp(s1 - top_scr[2 * h, pl.ds(0, 1), :]) * invz_scr[h]
        e_ref[2 * h + 1, rows, :] = jnp.exp(s2 - top_scr[2 * h + 1, pl.ds(0, 1), :])
        return carry

    lax.fori_loop(0, PEER_HEADS * (N_KEYS // PEER_ROWS), emit, 0)


def _peer_route(hnT, wqT, keys_bf):
    t_pad = hnT.shape[1]
    tb = PEER_TOKEN_BLOCK
    hp = 2 * PEER_HEADS
    tok3 = pl.BlockSpec((hp, N_KEYS, tb), lambda i: (0, 0, i))
    return pl.pallas_call(
        _peer_route_body,
        grid=(t_pad // tb,),
        in_specs=[
            pl.BlockSpec((D_MODEL, tb), lambda i: (0, i)),
            pl.BlockSpec((hp * PEER_HALF, D_MODEL), lambda i: (0, 0)),
            pl.BlockSpec((2, N_KEYS, PEER_HALF), lambda i: (0, 0, 0)),
        ],
        out_specs=[tok3, tok3],
        out_shape=[jax.ShapeDtypeStruct((hp, N_KEYS, t_pad), jnp.float32)] * 2,
        scratch_shapes=[
            pltpu.VMEM((hp * PEER_HALF, tb), jnp.bfloat16),
            pltpu.VMEM((hp, N_KEYS, tb), jnp.float32),
            pltpu.VMEM((hp, TOP_ROWS, tb), jnp.float32),
            pltpu.VMEM((PEER_HEADS, 1, tb), jnp.float32),
            pltpu.VMEM((PEER_HEADS, 1, tb), jnp.float32),
        ],
        compiler_params=pltpu.CompilerParams(
            dimension_semantics=("parallel",),
            vmem_limit_bytes=PEER_VMEM_LIMIT),
        name="peer_route",
    )(hnT, wqT, keys_bf)


def _peer_expert_body(hnT_ref, u_ref, vT_ref, a_ref, e_ref, res_ref, gfin_ref, o_ref,
                      h0_scr, h1_scr, a0_scr, a1_scr, acc_scr):
    j = pl.program_id(1)
    last = pl.num_programs(1) - 1
    f32 = jnp.float32
    slots = ((h0_scr, a0_scr), (h1_scr, a1_scr))

    @pl.when(j == 0)
    def _():
        acc_scr[...] = jnp.zeros_like(acc_scr)
        a1_scr[...] = jnp.zeros_like(a1_scr)
        h0_scr[...] = jnp.dot(u_ref[...], hnT_ref[...], preferred_element_type=f32)

    def steady(h_cur, a_cur, h_prv, a_prv):
        tile = PEER_PIPE_ROWS

        def gating(row0):
            i1 = (j - 1) * (PEER_EXPERT_BLOCK // N_KEYS) + row0 // N_KEYS
            r = row0 % N_KEYS
            gate = jnp.zeros((PEER_ROWS, PEER_TOKEN_BLOCK), f32)
            for h in range(PEER_HEADS):
                need = a_ref[2 * h, pl.ds(i1, 1), :]
                e1row = e_ref[2 * h, pl.ds(i1, 1), :]
                val = e_ref[2 * h + 1, r:r + PEER_ROWS, :] * e1row
                gate = gate + jnp.where(a_ref[2 * h + 1, r:r + PEER_ROWS, :] >= need, val, 0.0)
            x = h_prv[row0:row0 + PEER_ROWS, :]
            act = 0.5 * x * (1.0 + lax.erf(x * INV_SQRT2))
            a_prv[row0:row0 + PEER_ROWS, :] = (act * gate).astype(jnp.bfloat16)

        def pre_activation(span, cols):
            h_cur[span, cols] = jnp.dot(u_ref[span, :], hnT_ref[:, cols], preferred_element_type=f32)

        def accumulate(span, out_rows):
            acc_scr[out_rows, :] += jnp.dot(vT_ref[out_rows, span], a_cur[span, :], preferred_element_type=f32)

        chunks_per_tile = tile // PEER_ROWS
        for s in range(PEER_EXPERT_BLOCK // tile):
            span = slice(s * tile, (s + 1) * tile)
            mxu_work = [functools.partial(pre_activation, span, slice(c * tile, (c + 1) * tile))
                        for c in range(PEER_TOKEN_BLOCK // tile)]
            mxu_work += [functools.partial(accumulate, span, slice(m * tile, (m + 1) * tile))
                         for m in range(D_MODEL // tile)]
            every = chunks_per_tile // len(mxu_work)
            for c in range(chunks_per_tile):
                if c % every == 0 and c // every < len(mxu_work):
                    mxu_work[c // every]()
                gating(s * tile + c * PEER_ROWS)

    for parity in (0, 1):
        pl.when((j > 0) & (j < last) & (j % 2 == parity))(
            functools.partial(steady, *slots[parity], *slots[1 - parity]))

    @pl.when(j == last)
    def _():
        acc = acc_scr[...] + jnp.dot(vT_ref[...], a1_scr[...], preferred_element_type=f32)
        y = res_ref[...] + acc.T
        y = y * lax.rsqrt(jnp.mean(y * y, axis=-1, keepdims=True) + NORM_EPS)
        o_ref[...] = y * gfin_ref[...]


def _peer_experts(hnT, a, e, res, g_final, u_bf, vT_bf):
    t_pad = hnT.shape[1]
    tb, eb = PEER_TOKEN_BLOCK, PEER_EXPERT_BLOCK
    n_blocks = N_EXPERTS // eb
    tok3 = pl.BlockSpec((2 * PEER_HEADS, N_KEYS, tb), lambda i, j: (0, 0, i))
    return pl.pallas_call(
        _peer_expert_body,
        grid=(t_pad // tb, n_blocks + 2),
        in_specs=[
            pl.BlockSpec((D_MODEL, tb), lambda i, j: (0, i)),
            pl.BlockSpec((eb, D_MODEL), lambda i, j: (jnp.minimum(j, n_blocks - 1), 0)),
            pl.BlockSpec((D_MODEL, eb), lambda i, j: (0, jnp.clip(j - 2, 0, n_blocks - 1))),
            tok3, tok3,
            pl.BlockSpec((tb, D_MODEL), lambda i, j: (i, 0)),
            pl.BlockSpec((1, D_MODEL), lambda i, j: (0, 0)),
        ],
        out_specs=pl.BlockSpec((tb, D_MODEL), lambda i, j: (i, 0)),
        out_shape=jax.ShapeDtypeStruct((t_pad, D_MODEL), jnp.float32),
        scratch_shapes=[
            pltpu.VMEM((eb, tb), jnp.float32), pltpu.VMEM((eb, tb), jnp.float32),
            pltpu.VMEM((eb, tb), jnp.bfloat16), pltpu.VMEM((eb, tb), jnp.bfloat16),
            pltpu.VMEM((D_MODEL, tb), jnp.float32),
        ],
        compiler_params=pltpu.CompilerParams(
            dimension_semantics=("parallel", "arbitrary"),
            vmem_limit_bytes=PEER_VMEM_LIMIT),
        name="peer_experts",
    )(hnT, u_bf, vT_bf, a, e, res, g_final.reshape(1, D_MODEL))


def peer_block(h, hn, w_q, sub_keys, u_tab, v_tab, g_final):
    t = h.shape[0]
    pad = -t % PEER_TOKEN_BLOCK
    hnT = jnp.pad(hn, ((0, pad), (0, 0))).astype(jnp.bfloat16).T
    a, e = _peer_route(hnT, w_q.astype(jnp.bfloat16).T, sub_keys.astype(jnp.bfloat16))
    out = _peer_experts(hnT, a, e, jnp.pad(h, ((0, pad), (0, 0))), g_final,
                        u_tab.astype(jnp.bfloat16), v_tab.astype(jnp.bfloat16).T)
    return out[:t]


def trunk_layer(x, pos, mix_fn, g_mix, w_in, w_a, w_b, w_o, g_ffn):
    n, s, _ = x.shape
    xn = rms_norm(x, g_mix)
    z = xn @ w_in
    qa, ka, va, qb, kb, vb, ga, gb = jnp.split(z, SPLIT_POINTS, axis=-1)
    qa = rotary(qa.reshape(n, s, HEADS_A, HEAD_DIM), pos).reshape(n, s, HEADS_A, 1, HEAD_DIM)
    ka = rotary(ka.reshape(n, s, HEADS_A, HEAD_DIM), pos)
    va = va.reshape(n, s, HEADS_A, HEAD_DIM)
    qb = rotary(qb.reshape(n, s, HEADS_B, HEAD_DIM), pos).reshape(n, s, KV_HEADS_B, GROUP_B, HEAD_DIM)
    kb = rotary(kb.reshape(n, s, KV_HEADS_B, HEAD_DIM), pos)
    vb = vb.reshape(n, s, KV_HEADS_B, HEAD_DIM)
    oa, ob, state = mix_fn(qa, ka, va, qb, kb, vb)
    ya = oa.reshape(n, s, WIDTH_A) @ w_a
    yb = ob.reshape(n, s, WIDTH_B) @ w_b
    merged = jax.nn.sigmoid(ga) * ya + jax.nn.sigmoid(gb) * yb
    h = x + merged @ w_o
    hn = rms_norm(h, g_ffn)
    return h, hn, state


def kernel(x_prompt, x_sample, cache_a_k, cache_a_v, cache_b_k, cache_b_v, norm_mix, w_in,
           w_branch_a, w_branch_b, w_out, sink_b, norm_ffn, w_peer_q, peer_sub_keys, peer_u,
           peer_v, norm_final):
    assert DEPTH == 1
    pos_p = jnp.arange(x_prompt.shape[1])
    pos_s = PAST_LEN + jnp.arange(x_sample.shape[1])
    weights = (norm_mix[0], w_in[0], w_branch_a[0], w_branch_b[0], w_out[0], norm_ffn[0])
    sink = sink_b[0].astype(jnp.float32).reshape(KV_HEADS_B, GROUP_B)
    hp, hnp, st_p = trunk_layer(x_prompt, pos_p, functools.partial(prompt_mixers, sink=sink), *weights)
    hs, hns, st_s = trunk_layer(x_sample, pos_s,
                                functools.partial(sample_mixers, sink=sink, ck_a=cache_a_k[0],
                                                  cv_a=cache_a_v[0], ck_b=cache_b_k[0],
                                                  cv_b=cache_b_v[0]),
                                *weights)
    n_p = hp.shape[0] * hp.shape[1]
    h_all = jnp.concatenate([hp.reshape(n_p, D_MODEL), hs.reshape(-1, D_MODEL)], axis=0)
    hn_all = jnp.concatenate([hnp.reshape(n_p, D_MODEL), hns.reshape(-1, D_MODEL)], axis=0)
    y_all = peer_block(h_all, hn_all, w_peer_q[0], peer_sub_keys[0], peer_u[0], peer_v[0], norm_final)
    y_prompt = y_all[:n_p].reshape(hp.shape)
    y_sample = y_all[n_p:].reshape(hs.shape)
    a_k_p, a_v_p, b_k_p, b_v_p = [arr[None] for arr in st_p]
    a_k_s, a_v_s, b_k_s, b_v_s = [arr[None] for arr in st_s]
    return (y_prompt, y_sample, a_k_p, a_v_p, b_k_p, b_v_p, a_k_s, a_v_s, b_k_s, b_v_s)
```

```python
import functools
import jax, jax.numpy as jnp
from jax import lax
import numpy as np
from jax.experimental import pallas as pl
from jax.experimental.pallas import tpu as pltpu

D_MODEL = 1024
BATCH = 8
SEQ = 2048
DEPTH = 1
DEC_BATCH = 128
DEC_SEQ = 1
PAST_LEN = 8192

HEAD_DIM = 64
HEADS_A = 8
DILATED_CONFIGS = ((128, 1), (512, 4), (2048, 16))
WINDOW_A = 2048
HEADS_B = 8
KV_HEADS_B = 2
GROUP_B = HEADS_B // KV_HEADS_B
WINDOW_B = 128
BLOCK = 128
ROPE_THETA = 10000.0
NORM_EPS = 1e-6
NEG_INF = -1e30
SCALE = HEAD_DIM ** -0.5

WIDTH_A = HEADS_A * HEAD_DIM
WIDTH_B = HEADS_B * HEAD_DIM
KV_WIDTH_B = KV_HEADS_B * HEAD_DIM
IN_WIDTHS = (WIDTH_A, WIDTH_A, WIDTH_A, WIDTH_B, KV_WIDTH_B, KV_WIDTH_B, D_MODEL, D_MODEL)
D_IN = sum(IN_WIDTHS)
SPLIT_POINTS = tuple(int(v) for v in np.cumsum(IN_WIDTHS)[:-1])

N_KEYS = 128
N_EXPERTS = N_KEYS * N_KEYS
PEER_HEADS = 8
PEER_TOPK = 16
PEER_HALF = 128
PEER_QUERY_DIM = 2 * PEER_HALF
PEER_CHUNK = 128


LANES = 128
SAMPLE_VMEM_LIMIT = 48 * 1024 * 1024
KEYS_PER_BRANCH = 128


def _bf16_round(x):
    return x.astype(jnp.bfloat16).astype(jnp.float32)


def _decode_softmax(s, s_new, sink):
    m = jnp.maximum(jnp.max(s, axis=-1, keepdims=True), s_new)
    if sink is not None:
        m = jnp.maximum(m, sink)
    e = jnp.exp(s - m)
    e_new = jnp.exp(s_new - m)
    denom = jnp.sum(e, axis=-1, keepdims=True) + e_new
    if sink is not None:
        denom = denom + jnp.exp(sink - m)
    return e / denom, e_new / denom, m + jnp.log(denom)


def _sample_mixer_body(qa_ref, ka_ref, va_ref, qb_ref, kb_ref, vb_ref, sink_ref,
                       cak_ref, cav_ref, cbk_ref, cbv_ref,
                       oa_ref, ob_ref, nak_ref, nav_ref, nbk_ref, nbv_ref, s_scr, o_scr):
    f32, bf16 = jnp.float32, jnp.bfloat16
    nt = (((1,), (1,)), ((), ()))
    win_a = cak_ref.shape[2]
    lane_tiles = win_a // LANES

    def as_column(row):
        return jnp.broadcast_to(row, (LANES, row.shape[1])).T

    def shifted(old, col):
        rolled = pltpu.roll(old, old.shape[1] - 1, axis=1)
        pos = lax.broadcasted_iota(jnp.int32, old.shape, 1)
        return jnp.where(pos == old.shape[1] - 1, jnp.tile(col, (1, old.shape[1] // LANES)), rolled)

    qa, ka, va = qa_ref[0], ka_ref[0], va_ref[0]
    own = (lax.broadcasted_iota(jnp.int32, (HEADS_A, WIDTH_A), 1) // HEAD_DIM
           == lax.broadcasted_iota(jnp.int32, (HEADS_A, WIDTH_A), 0))
    s_new = jnp.sum(jnp.where(own, qa * ka, 0.0), axis=-1, keepdims=True) * SCALE
    q_col = as_column(qa)
    k_col = as_column(ka)
    v_col = as_column(va)
    for h in range(HEADS_A):
        rows = slice(h * HEAD_DIM, (h + 1) * HEAD_DIM)
        qh = q_col[rows, :]

        def logits(t, carry, rows=rows, qh=qh, h=h):
            lanes = pl.ds(pl.multiple_of(t * LANES, LANES), LANES)
            s_scr[pl.ds(h, 1), lanes] = jnp.sum(cak_ref[0, rows, lanes] * qh, axis=0, keepdims=True)
            return carry

        lax.fori_loop(0, lane_tiles, logits, 0)
        nak_ref[0, rows, :] = shifted(cak_ref[0, rows, :], k_col[rows, :])

    s = s_scr[...] * SCALE
    pos = lax.broadcasted_iota(jnp.int32, s.shape, 1)
    ps, p_news, lses = [], [], []
    for window, dil in DILATED_CONFIGS:
        reach = (pos >= win_a - window) & (pos % dil == 0)
        p, p_new, lse = _decode_softmax(jnp.where(reach, s, NEG_INF), s_new, None)
        ps.append(p)
        p_news.append(p_new)
        lses.append(lse)
    top = jnp.maximum(jnp.maximum(lses[0], lses[1]), lses[2])
    ws = [jnp.exp(l - top) for l in lses]
    inv = 1.0 / (ws[0] + ws[1] + ws[2])
    s_scr[...] = (ws[0] * ps[0] + ws[1] * ps[1] + ws[2] * ps[2]) * inv
    p_new = (ws[0] * p_news[0] + ws[1] * p_news[1] + ws[2] * p_news[2]) * inv

    for h in range(HEADS_A):
        rows = slice(h * HEAD_DIM, (h + 1) * HEAD_DIM)

        def weighted(t, acc, rows=rows, h=h):
            lanes = pl.ds(pl.multiple_of(t * LANES, LANES), LANES)
            return acc + cav_ref[0, rows, lanes] * s_scr[pl.ds(h, 1), lanes]

        acc = lax.fori_loop(0, lane_tiles, weighted, jnp.zeros((HEAD_DIM, LANES), f32))
        o_scr[rows, :] = jnp.broadcast_to(jnp.sum(acc, axis=1, keepdims=True), (HEAD_DIM, LANES))
        nav_ref[0, rows, :] = shifted(cav_ref[0, rows, :], v_col[rows, :])
    p_new_lanes = jnp.sum(jnp.where(own, p_new, 0.0), axis=0, keepdims=True)
    oa_ref[0] = o_scr[...].T[0:1, :] + p_new_lanes * va

    qb, kb, vb = qb_ref[0], kb_ref[0], vb_ref[0]
    lane = lax.broadcasted_iota(jnp.int32, (1, KV_WIDTH_B), 1)
    heads_per_chunk = KV_WIDTH_B // HEAD_DIM
    q_rows = []
    for h in range(HEADS_B):
        c = h // heads_per_chunk
        piece = qb[:, c * KV_WIDTH_B:(c + 1) * KV_WIDTH_B]
        if h % heads_per_chunk != h // GROUP_B:
            piece = pltpu.roll(piece, HEAD_DIM, axis=1)
        q_rows.append(jnp.where(lane // HEAD_DIM == h // GROUP_B, piece, 0.0))
    q_rows = jnp.concatenate(q_rows, axis=0)
    s = jnp.dot(q_rows.astype(bf16), cbk_ref[0].astype(bf16), preferred_element_type=f32) * SCALE
    s_new = jnp.sum(_bf16_round(q_rows) * _bf16_round(kb), axis=-1, keepdims=True) * SCALE
    p, p_new, _ = _decode_softmax(s, s_new, sink_ref[...])
    o = (lax.dot_general(p.astype(bf16), cbv_ref[0].astype(bf16), nt, preferred_element_type=f32)
         + _bf16_round(p_new) * _bf16_round(vb))
    for c in range(WIDTH_B // KV_WIDTH_B):
        halves = []
        for slot in range(heads_per_chunk):
            h = c * heads_per_chunk + slot
            r = o[h:h + 1, :]
            if h // GROUP_B != slot:
                r = pltpu.roll(r, HEAD_DIM, axis=1)
            halves.append(r)
        ob_ref[0, :, c * KV_WIDTH_B:(c + 1) * KV_WIDTH_B] = jnp.where(lane < HEAD_DIM, halves[0], halves[1])

    nbk_ref[0] = shifted(cbk_ref[0], as_column(kb))
    nbv_ref[0] = shifted(cbv_ref[0], as_column(vb))


def sample_mixers(qa, ka, va, qb, kb, vb, sink, ck_a, cv_a, ck_b, cv_b):
    n = qa.shape[0]
    assert ck_a.shape[1] == WINDOW_A and ck_b.shape[1] == WINDOW_B
    assert KV_WIDTH_B == 2 * HEAD_DIM == LANES
    row = lambda a, w: a.reshape(n, 1, w)
    win = lambda a, w: a.transpose(0, 2, 3, 1).reshape(n, w, a.shape[1])
    unwin = lambda a, like: a.reshape(n, like.shape[2], like.shape[3], like.shape[1]).transpose(0, 3, 1, 2)
    row_spec = lambda w: pl.BlockSpec((1, 1, w), lambda b: (b, 0, 0))
    win_spec = lambda r, w: pl.BlockSpec((1, w, r), lambda b: (b, 0, 0))
    f32 = jnp.float32
    oa, ob, nak, nav, nbk, nbv = pl.pallas_call(
        _sample_mixer_body,
        grid=(n,),
        in_specs=[row_spec(WIDTH_A), row_spec(WIDTH_A), row_spec(WIDTH_A),
                  row_spec(WIDTH_B), row_spec(KV_WIDTH_B), row_spec(KV_WIDTH_B),
                  pl.BlockSpec((HEADS_B, 1), lambda b: (0, 0)),
                  win_spec(WINDOW_A, WIDTH_A), win_spec(WINDOW_A, WIDTH_A),
                  win_spec(WINDOW_B, KV_WIDTH_B), win_spec(WINDOW_B, KV_WIDTH_B)],
        out_specs=[row_spec(WIDTH_A), row_spec(WIDTH_B),
                   win_spec(WINDOW_A, WIDTH_A), win_spec(WINDOW_A, WIDTH_A),
                   win_spec(WINDOW_B, KV_WIDTH_B), win_spec(WINDOW_B, KV_WIDTH_B)],
        out_shape=[jax.ShapeDtypeStruct((n, 1, WIDTH_A), f32), jax.ShapeDtypeStruct((n, 1, WIDTH_B), f32),
                   jax.ShapeDtypeStruct((n, WIDTH_A, WINDOW_A), f32), jax.ShapeDtypeStruct((n, WIDTH_A, WINDOW_A), f32),
                   jax.ShapeDtypeStruct((n, KV_WIDTH_B, WINDOW_B), f32),
                   jax.ShapeDtypeStruct((n, KV_WIDTH_B, WINDOW_B), f32)],
        scratch_shapes=[pltpu.VMEM((HEADS_A, WINDOW_A), f32), pltpu.VMEM((WIDTH_A, LANES), f32)],
        compiler_params=pltpu.CompilerParams(
            dimension_semantics=("parallel",),
            vmem_limit_bytes=SAMPLE_VMEM_LIMIT),
        name="sample_mixers",
    )(row(qa, WIDTH_A), row(ka, WIDTH_A), row(va, WIDTH_A), row(qb, WIDTH_B), row(kb, KV_WIDTH_B),
      row(vb, KV_WIDTH_B), sink.reshape(HEADS_B, 1),
      win(ck_a, WIDTH_A), win(cv_a, WIDTH_A), win(ck_b, KV_WIDTH_B), win(cv_b, KV_WIDTH_B))
    state = (unwin(nak, ck_a), unwin(nav, cv_a), unwin(nbk, ck_b), unwin(nbv, cv_b))
    return oa.reshape(n, WIDTH_A), ob.reshape(n, WIDTH_B), state


PEER_TOKEN_BLOCK = 512
PEER_EXPERT_BLOCK = 1024
PEER_ROWS = 16
PEER_PIPE_ROWS = 256
ROUTE_STREAMS = 2
TOP_ROWS = 24
PEER_VMEM_LIMIT = 48 * 1024 * 1024
INV_SQRT2 = 0.7071067811865476


def _peer_route_body(hnT_ref, wqT_ref, keys_ref, a_ref, e_ref, q_scr, s_scr, top_scr, thr_scr, invz_scr):
    tb = PEER_TOKEN_BLOCK
    lane_tiles = tb // LANES
    q_scr[...] = jnp.dot(wqT_ref[...], hnT_ref[...], preferred_element_type=jnp.float32).astype(jnp.bfloat16)
    for hp in range(2 * PEER_HEADS):
        s_scr[hp] = jnp.dot(keys_ref[hp % 2], q_scr[hp * PEER_HALF:(hp + 1) * PEER_HALF, :],
                            preferred_element_type=jnp.float32)

    def take_max(x, iota, n):
        m = jnp.max(x, axis=0, keepdims=True)
        first = jnp.min(jnp.where(x == m, iota, float(n)), axis=0, keepdims=True)
        return m, jnp.where(iota == first, -jnp.inf, x)

    groups = lane_tiles // ROUTE_STREAMS

    def tile_lanes(u, k):
        return pl.ds(pl.multiple_of(((u % groups) * ROUTE_STREAMS + k) * LANES, LANES), LANES)

    def half_top(u, carry):
        hp = u // groups
        iota = lax.broadcasted_iota(jnp.int32, (N_KEYS, LANES), 0).astype(jnp.float32)
        lanes = [tile_lanes(u, k) for k in range(ROUTE_STREAMS)]
        xs = [s_scr[hp, :, ln] for ln in lanes]
        for ln in lanes:
            top_scr[hp, PEER_TOPK:, ln] = jnp.full((TOP_ROWS - PEER_TOPK, LANES), -jnp.inf, jnp.float32)
        for r in range(PEER_TOPK + 1):
            for k, ln in enumerate(lanes):
                m, xs[k] = take_max(xs[k], iota, N_KEYS)
                top_scr[hp, pl.ds(r, 1), ln] = m
        return carry

    lax.fori_loop(0, 2 * PEER_HEADS * groups, half_top, 0)

    def pair_top(u, carry):
        h = u // groups
        lanes = [tile_lanes(u, k) for k in range(ROUTE_STREAMS)]
        xs = []
        for ln in lanes:
            t1 = top_scr[2 * h, :, ln]
            t2 = top_scr[2 * h + 1, :, ln]
            xs.append(jnp.concatenate([t1[0:1, :] + t2] + [t1[k:k + 1, :] + t2[0:8, :] for k in range(1, 8)]
                                      + [t1[8:, :] + t2[0:1, :]], axis=0))
        n = xs[0].shape[0]
        iota = lax.broadcasted_iota(jnp.int32, (n, LANES), 0).astype(jnp.float32)
        best, v, z = [None] * ROUTE_STREAMS, [None] * ROUTE_STREAMS, [None] * ROUTE_STREAMS
        for r in range(PEER_TOPK):
            for k in range(ROUTE_STREAMS):
                v[k], xs[k] = take_max(xs[k], iota, n)
                if r == 0:
                    best[k], z[k] = v[k], jnp.ones_like(v[k])
                else:
                    z[k] = z[k] + jnp.exp(v[k] - best[k])
        for k, ln in enumerate(lanes):
            nxt, _ = take_max(xs[k], iota, n)
            thr_scr[h, :, ln] = 0.5 * (v[k] + nxt)
            invz_scr[h, :, ln] = 1.0 / z[k]
        return carry

    lax.fori_loop(0, PEER_HEADS * groups, pair_top, 0)

    def emit(u, carry):
        h = u // (N_KEYS // PEER_ROWS)
        rows = pl.ds(pl.multiple_of((u % (N_KEYS // PEER_ROWS)) * PEER_ROWS, PEER_ROWS), PEER_ROWS)
        s1 = s_scr[2 * h, rows, :]
        s2 = s_scr[2 * h + 1, rows, :]
        a_ref[2 * h, rows, :] = thr_scr[h] - s1
        a_ref[2 * h + 1, rows, :] = s2
        e_ref[2 * h, rows, :] = jnp.exp(s1 - top_scr[2 * h, pl.ds(0, 1), :]) * invz_scr[h]
        e_ref[2 * h + 1, rows, :] = jnp.exp(s2 - top_scr[2 * h + 1, pl.ds(0, 1), :])
        return carry

    lax.fori_loop(0, PEER_HEADS * (N_KEYS // PEER_ROWS), emit, 0)


def _peer_route(hnT, wqT, keys_bf):
    t_pad = hnT.shape[1]
    tb = PEER_TOKEN_BLOCK
    hp = 2 * PEER_HEADS
    tok3 = pl.BlockSpec((hp, N_KEYS, tb), lambda i: (0, 0, i))
    return pl.pallas_call(
        _peer_route_body,
        grid=(t_pad // tb,),
        in_specs=[
            pl.BlockSpec((D_MODEL, tb), lambda i: (0, i)),
            pl.BlockSpec((hp * PEER_HALF, D_MODEL), lambda i: (0, 0)),
            pl.BlockSpec((2, N_KEYS, PEER_HALF), lambda i: (0, 0, 0)),
        ],
        out_specs=[tok3, tok3],
        out_shape=[jax.ShapeDtypeStruct((hp, N_KEYS, t_pad), jnp.float32)] * 2,
        scratch_shapes=[
            pltpu.VMEM((hp * PEER_HALF, tb), jnp.bfloat16),
            pltpu.VMEM((hp, N_KEYS, tb), jnp.float32),
            pltpu.VMEM((hp, TOP_ROWS, tb), jnp.float32),
            pltpu.VMEM((PEER_HEADS, 1, tb), jnp.float32),
            pltpu.VMEM((PEER_HEADS, 1, tb), jnp.float32),
        ],
        compiler_params=pltpu.CompilerParams(
            dimension_semantics=("parallel",),
            vmem_limit_bytes=PEER_VMEM_LIMIT),
        name="peer_route",
    )(hnT, wqT, keys_bf)


def _peer_expert_body(hnT_ref, u_ref, vT_ref, a_ref, e_ref, res_ref, gfin_ref, o_ref,
                      h0_scr, h1_scr, a0_scr, a1_scr, acc_scr):
    j = pl.program_id(1)
    last = pl.num_programs(1) - 1
    f32 = jnp.float32
    slots = ((h0_scr, a0_scr), (h1_scr, a1_scr))

    @pl.when(j == 0)
    def _():
        acc_scr[...] = jnp.zeros_like(acc_scr)
        a1_scr[...] = jnp.zeros_like(a1_scr)
        h0_scr[...] = jnp.dot(u_ref[...], hnT_ref[...], preferred_element_type=f32)

    def steady(h_cur, a_cur, h_prv, a_prv):
        tile = PEER_PIPE_ROWS

        def gating(row0):
            i1 = (j - 1) * (PEER_EXPERT_BLOCK // N_KEYS) + row0 // N_KEYS
            r = row0 % N_KEYS
            gate = jnp.zeros((PEER_ROWS, PEER_TOKEN_BLOCK), f32)
            for h in range(PEER_HEADS):
                need = a_ref[2 * h, pl.ds(i1, 1), :]
                e1row = e_ref[2 * h, pl.ds(i1, 1), :]
                val = e_ref[2 * h + 1, r:r + PEER_ROWS, :] * e1row
                gate = gate + jnp.where(a_ref[2 * h + 1, r:r + PEER_ROWS, :] >= need, val, 0.0)
            x = h_prv[row0:row0 + PEER_ROWS, :]
            act = 0.5 * x * (1.0 + lax.erf(x * INV_SQRT2))
            a_prv[row0:row0 + PEER_ROWS, :] = (act * gate).astype(jnp.bfloat16)

        def pre_activation(span, cols):
            h_cur[span, cols] = jnp.dot(u_ref[span, :], hnT_ref[:, cols], preferred_element_type=f32)

        def accumulate(span, out_rows):
            acc_scr[out_rows, :] += jnp.dot(vT_ref[out_rows, span], a_cur[span, :], preferred_element_type=f32)

        chunks_per_tile = tile // PEER_ROWS
        for s in range(PEER_EXPERT_BLOCK // tile):
            span = slice(s * tile, (s + 1) * tile)
            mxu_work = [functools.partial(pre_activation, span, slice(c * tile, (c + 1) * tile))
                        for c in range(PEER_TOKEN_BLOCK // tile)]
            mxu_work += [functools.partial(accumulate, span, slice(m * tile, (m + 1) * tile))
                         for m in range(D_MODEL // tile)]
            every = chunks_per_tile // len(mxu_work)
            for c in range(chunks_per_tile):
                if c % every == 0 and c // every < len(mxu_work):
                    mxu_work[c // every]()
                gating(s * tile + c * PEER_ROWS)

    for parity in (0, 1):
        pl.when((j > 0) & (j < last) & (j % 2 == parity))(
            functools.partial(steady, *slots[parity], *slots[1 - parity]))

    @pl.when(j == last)
    def _():
        acc = acc_scr[...] + jnp.dot(vT_ref[...], a1_scr[...], preferred_element_type=f32)
        y = res_ref[...] + acc.T
        y = y * lax.rsqrt(jnp.mean(y * y, axis=-1, keepdims=True) + NORM_EPS)
        o_ref[...] = y * gfin_ref[...]


def _peer_experts(hnT, a, e, res, g_final, u_bf, vT_bf):
    t_pad = hnT.shape[1]
    tb, eb = PEER_TOKEN_BLOCK, PEER_EXPERT_BLOCK
    n_blocks = N_EXPERTS // eb
    tok3 = pl.BlockSpec((2 * PEER_HEADS, N_KEYS, tb), lambda i, j: (0, 0, i))
    return pl.pallas_call(
        _peer_expert_body,
        grid=(t_pad // tb, n_blocks + 2),
        in_specs=[
            pl.BlockSpec((D_MODEL, tb), lambda i, j: (0, i)),
            pl.BlockSpec((eb, D_MODEL), lambda i, j: (jnp.minimum(j, n_blocks - 1), 0)),
            pl.BlockSpec((D_MODEL, eb), lambda i, j: (0, jnp.clip(j - 2, 0, n_blocks - 1))),
            tok3, tok3,
            pl.BlockSpec((tb, D_MODEL), lambda i, j: (i, 0)),
            pl.BlockSpec((1, D_MODEL), lambda i, j: (0, 0)),
        ],
        out_specs=pl.BlockSpec((tb, D_MODEL), lambda i, j: (i, 0)),
        out_shape=jax.ShapeDtypeStruct((t_pad, D_MODEL), jnp.float32),
        scratch_shapes=[
            pltpu.VMEM((eb, tb), jnp.float32), pltpu.VMEM((eb, tb), jnp.float32),
            pltpu.VMEM((eb, tb), jnp.bfloat16), pltpu.VMEM((eb, tb), jnp.bfloat16),
            pltpu.VMEM((D_MODEL, tb), jnp.float32),
        ],
        compiler_params=pltpu.CompilerParams(
            dimension_semantics=("parallel", "arbitrary"),
            vmem_limit_bytes=PEER_VMEM_LIMIT),
        name="peer_experts",
    )(hnT, u_bf, vT_bf, a, e, res, g_final.reshape(1, D_MODEL))


def peer_block(h, hnT, w_q, sub_keys, u_tab, v_tab, g_final):
    t = h.shape[0]
    pad = -t % PEER_TOKEN_BLOCK
    hnT = jnp.pad(hnT, ((0, 0), (0, pad)))
    a, e = _peer_route(hnT, w_q.astype(jnp.bfloat16).T, sub_keys.astype(jnp.bfloat16))
    out = _peer_experts(hnT, a, e, jnp.pad(h, ((0, pad), (0, 0))), g_final,
                        u_tab.astype(jnp.bfloat16), v_tab.astype(jnp.bfloat16).T)
    return out[:t]


ATTN_VMEM_LIMIT = 40 * 1024 * 1024
HEADS_PER_TILE = LANES // HEAD_DIM
ATTN_STREAMS = 4


def _band_units(units, sinks):
    f32, bf16 = jnp.float32, jnp.bfloat16
    nt = (((1,), (1,)), ((), ()))
    rows = HEADS_PER_TILE * BLOCK
    qi = lax.broadcasted_iota(jnp.int32, (rows, 2 * BLOCK), 0) % BLOCK
    kj = lax.broadcasted_iota(jnp.int32, (rows, 2 * BLOCK), 1)
    off = BLOCK + qi - kj
    band = (off >= 0) & (off <= BLOCK)
    own_block = kj >= BLOCK
    ss = []
    for qs, load_k, _, key_lanes, _ in units:
        lhs = jnp.concatenate([jnp.where(key_lanes[i], qs[i], 0.0) for i in range(HEADS_PER_TILE)], axis=0)
        ss.append(lax.dot_general(lhs.astype(bf16), load_k().astype(bf16), nt, preferred_element_type=f32) * SCALE)
    ss = [jnp.where(band & (own_block | jnp.logical_not(unit[4])), s, NEG_INF) for s, unit in zip(ss, units)]
    ms = [jnp.max(s, axis=-1, keepdims=True) for s in ss]
    if sinks is not None:
        ms = [jnp.maximum(m, sinks) for m in ms]
    es = [jnp.exp(s - m) for s, m in zip(ss, ms)]
    denoms = [jnp.sum(e, axis=-1, keepdims=True) for e in es]
    if sinks is not None:
        denoms = [d + jnp.exp(sinks - m) for d, m in zip(denoms, ms)]
    results = []
    for e, d, m, unit in zip(es, denoms, ms, units):
        o = jnp.dot((e / d).astype(bf16), unit[2]().astype(bf16), preferred_element_type=f32)
        lse = m + jnp.log(d)
        results.append([(o[i * BLOCK:(i + 1) * BLOCK, :], lse[i * BLOCK:(i + 1) * BLOCK, :])
                        for i in range(HEADS_PER_TILE)])
    return results


def _prompt_attention_body(qa_ref, ka_ref, va_ref, qb_ref, kb_ref, vb_ref, sink_ref, oa_ref, ob_ref,
                           o_scr, lse_scr):
    f32 = jnp.float32
    seq = qa_ref.shape[1]
    lane = lax.broadcasted_iota(jnp.int32, (1, LANES), 1)
    low = lane < HEAD_DIM
    own = [low, jnp.logical_not(low)]

    def window(ref, prev, cur):
        return lambda: jnp.concatenate([ref[0, prev, :], ref[0, cur, :]], axis=0)

    for c, (reach, dil) in enumerate(DILATED_CONFIGS):
        assert reach // dil == BLOCK
        blocks = seq // (dil * BLOCK)

        def step(it, carry, c=c, dil=dil, blocks=blocks):
            units, curs = [], []
            for k in range(ATTN_STREAMS):
                u = it * ATTN_STREAMS + k
                res, blk = u // blocks, u % blocks
                cur = pl.ds(res + dil * BLOCK * blk, BLOCK, stride=dil)
                prev = pl.ds(res + dil * BLOCK * jnp.maximum(blk - 1, 0), BLOCK, stride=dil)
                q = qa_ref[0, cur, :]
                units.append(([q, q], window(ka_ref, prev, cur), window(va_ref, prev, cur), own, blk == 0))
                curs.append(cur)
            for cur, ((o0, l0), (o1, l1)) in zip(curs, _band_units(units, None)):
                o_scr[c, cur, :] = jnp.where(low, o0, o1)
                lse_scr[c, cur, :] = jnp.where(low, l0, l1)
            return carry

        lax.fori_loop(0, dil * blocks // ATTN_STREAMS, step, 0)

    def merge(t, carry):
        rows = pl.ds(pl.multiple_of(t * BLOCK, BLOCK), BLOCK)
        ls = [lse_scr[c, rows, :] for c in range(len(DILATED_CONFIGS))]
        top = jnp.maximum(jnp.maximum(ls[0], ls[1]), ls[2])
        ws = [jnp.exp(l - top) for l in ls]
        num = ws[0] * o_scr[0, rows, :] + ws[1] * o_scr[1, rows, :] + ws[2] * o_scr[2, rows, :]
        oa_ref[0, rows, :] = num / (ws[0] + ws[1] + ws[2])
        return carry

    lax.fori_loop(0, seq // BLOCK, merge, 0)

    slab = pl.program_id(1)
    kv = slab * HEADS_PER_TILE // GROUP_B
    kv_lanes = (lane // HEAD_DIM) == kv
    sinks = jnp.concatenate(
        [jnp.broadcast_to(sink_ref[pl.ds(slab * HEADS_PER_TILE + i, 1), :], (BLOCK, 1))
         for i in range(HEADS_PER_TILE)], axis=0)

    def step_b(it, carry):
        units, curs = [], []
        for k in range(ATTN_STREAMS):
            blk = it * ATTN_STREAMS + k
            cur = pl.ds(pl.multiple_of(blk * BLOCK, BLOCK), BLOCK)
            prev = pl.ds(pl.multiple_of(jnp.maximum(blk - 1, 0) * BLOCK, BLOCK), BLOCK)
            q = qb_ref[0, cur, :]
            q_swapped = pltpu.roll(q, HEAD_DIM, axis=1)
            qs = [jnp.where(kv == i, q, q_swapped) for i in range(HEADS_PER_TILE)]
            units.append((qs, window(kb_ref, prev, cur), window(vb_ref, prev, cur), [kv_lanes, kv_lanes], blk == 0))
            curs.append(cur)
        for cur, outs in zip(curs, _band_units(units, sinks)):
            homes = [jnp.where(kv == i, outs[i][0], pltpu.roll(outs[i][0], HEAD_DIM, axis=1))
                     for i in range(HEADS_PER_TILE)]
            ob_ref[0, cur, :] = jnp.where(low, homes[0], homes[1])
        return carry

    lax.fori_loop(0, seq // BLOCK // ATTN_STREAMS, step_b, 0)


def prompt_attention(qa, ka, va, qb, kb, vb, sink):
    n, seq, _ = qa.shape
    assert KV_WIDTH_B == LANES and seq % (BLOCK * max(d for _, d in DILATED_CONFIGS)) == 0
    slab = pl.BlockSpec((1, seq, LANES), lambda b, p: (b, 0, p))
    whole = pl.BlockSpec((1, seq, LANES), lambda b, p: (b, 0, 0))
    f32 = jnp.float32
    return pl.pallas_call(
        _prompt_attention_body,
        grid=(n, WIDTH_A // LANES),
        in_specs=[slab, slab, slab, slab, whole, whole, pl.BlockSpec((HEADS_B, 1), lambda b, p: (0, 0))],
        out_specs=[slab, slab],
        out_shape=[jax.ShapeDtypeStruct((n, seq, WIDTH_A), f32), jax.ShapeDtypeStruct((n, seq, WIDTH_B), f32)],
        scratch_shapes=[pltpu.VMEM((len(DILATED_CONFIGS), seq, LANES), f32),
                        pltpu.VMEM((len(DILATED_CONFIGS), seq, LANES), f32)],
        compiler_params=pltpu.CompilerParams(
            dimension_semantics=("parallel", "parallel"),
            vmem_limit_bytes=ATTN_VMEM_LIMIT),
        name="prompt_attention",
    )(qa, ka, va, qb, kb, vb, sink.reshape(HEADS_B, 1))


PROJ_VMEM_LIMIT = 48 * 1024 * 1024
PROJ_TOKEN_BLOCK = 512
QKV_WIDTHS = IN_WIDTHS[:6]
QKV_WIDTH = sum(QKV_WIDTHS)
QKV_ROTATED = (True, True, False, True, True, False)


def _rms_normed(x, g):
    return x * lax.rsqrt(jnp.mean(x * x, axis=-1, keepdims=True) + NORM_EPS) * g


def _rope_slab(x, cos, sin_signed):
    lane = lax.broadcasted_iota(jnp.int32, (1, LANES), 1)
    half = HEAD_DIM // 2
    partner = jnp.where(lane % HEAD_DIM < half, pltpu.roll(x, LANES - half, axis=1), pltpu.roll(x, half, axis=1))
    return x * cos + partner * sin_signed


def _in_proj_body(x_ref, g_ref, w_ref, cos_ref, sin_ref, *out_refs, channel_major):
    xn = _rms_normed(x_ref[...], g_ref[...]).astype(jnp.bfloat16)
    cos, sin = cos_ref[...], sin_ref[...]
    groups = []
    c0 = 0
    for width, rotated in zip(QKV_WIDTHS, QKV_ROTATED):
        z = jnp.dot(xn, w_ref[:, c0:c0 + width], preferred_element_type=jnp.float32)
        if rotated:
            z = jnp.concatenate([_rope_slab(z[:, c:c + LANES], cos, sin) for c in range(0, width, LANES)], axis=1)
        groups.append(z)
        c0 += width
    for ref, z in zip(out_refs[:6], groups):
        ref[...] = z
    if channel_major:
        kaT_ref, vaT_ref, kbT_ref, vbT_ref = out_refs[6:]
        kaT_ref[0] = groups[1].T
        vaT_ref[0] = groups[2].T
        tb = x_ref.shape[0]
        kbT_ref[0] = groups[4][tb - WINDOW_B:, :].T
        vbT_ref[0] = groups[5][tb - WINDOW_B:, :].T


def in_proj(x, g_mix, w_qkv_bf, cos, sin, tb, seq):
    t = x.shape[0]
    channel_major = seq is not None
    per_seq = seq // tb if channel_major else None
    f32 = jnp.float32
    tok = lambda w: pl.BlockSpec((tb, w), lambda i: (i, 0))
    out_specs = [tok(w) for w in QKV_WIDTHS]
    out_shape = [jax.ShapeDtypeStruct((t, w), f32) for w in QKV_WIDTHS]
    if channel_major:
        assert seq == WINDOW_A and tb >= WINDOW_B
        n = t // seq
        out_specs += [pl.BlockSpec((1, WIDTH_A, tb), lambda i: (i // per_seq, 0, i % per_seq))] * 2
        out_specs += [pl.BlockSpec((1, KV_WIDTH_B, WINDOW_B), lambda i: (i // per_seq, 0, 0))] * 2
        out_shape += [jax.ShapeDtypeStruct((n, WIDTH_A, seq), f32)] * 2
        out_shape += [jax.ShapeDtypeStruct((n, KV_WIDTH_B, WINDOW_B), f32)] * 2
    return pl.pallas_call(
        functools.partial(_in_proj_body, channel_major=channel_major),
        grid=(t // tb,),
        in_specs=[tok(D_MODEL), pl.BlockSpec((1, D_MODEL), lambda i: (0, 0)),
                  pl.BlockSpec((D_MODEL, QKV_WIDTH), lambda i: (0, 0)), tok(LANES), tok(LANES)],
        out_specs=out_specs,
        out_shape=out_shape,
        compiler_params=pltpu.CompilerParams(
            dimension_semantics=("arbitrary",),
            vmem_limit_bytes=PROJ_VMEM_LIMIT),
        name="in_proj",
    )(x, g_mix.reshape(1, D_MODEL), w_qkv_bf, cos, sin)


def _out_proj_body(x_ref, oa_ref, ob_ref, gmix_ref, wg_ref, wa_ref, wb_ref, wo_ref, gffn_ref, h_ref, hnT_ref):
    f32, bf16 = jnp.float32, jnp.bfloat16
    x = x_ref[...]
    xn = _rms_normed(x, gmix_ref[...]).astype(bf16)
    ya = jnp.dot(oa_ref[...].astype(bf16), wa_ref[...], preferred_element_type=f32)
    merged = jax.nn.sigmoid(jnp.dot(xn, wg_ref[:, :D_MODEL], preferred_element_type=f32)) * ya
    yb = jnp.dot(ob_ref[...].astype(bf16), wb_ref[...], preferred_element_type=f32)
    merged = merged + jax.nn.sigmoid(jnp.dot(xn, wg_ref[:, D_MODEL:], preferred_element_type=f32)) * yb
    h = x + jnp.dot(merged.astype(bf16), wo_ref[...], preferred_element_type=f32)
    h_ref[...] = h
    hnT_ref[...] = _rms_normed(h, gffn_ref[...]).T.astype(bf16)


def out_proj(x, oa, ob, g_mix, w_gate_bf, w_a_bf, w_b_bf, w_o_bf, g_ffn, tb):
    t = x.shape[0]
    tok = lambda w: pl.BlockSpec((tb, w), lambda i: (i, 0))
    full = lambda a: pl.BlockSpec(a.shape, lambda i: (0, 0))
    g_mix, g_ffn = g_mix.reshape(1, D_MODEL), g_ffn.reshape(1, D_MODEL)
    return pl.pallas_call(
        _out_proj_body,
        grid=(t // tb,),
        in_specs=[tok(D_MODEL), tok(WIDTH_A), tok(WIDTH_B), full(g_mix), full(w_gate_bf), full(w_a_bf),
                  full(w_b_bf), full(w_o_bf), full(g_ffn)],
        out_specs=[tok(D_MODEL), pl.BlockSpec((D_MODEL, tb), lambda i: (0, i))],
        out_shape=[jax.ShapeDtypeStruct((t, D_MODEL), jnp.float32),
                   jax.ShapeDtypeStruct((D_MODEL, t), jnp.bfloat16)],
        compiler_params=pltpu.CompilerParams(
            dimension_semantics=("parallel",),
            vmem_limit_bytes=PROJ_VMEM_LIMIT),
        name="out_proj",
    )(x, oa, ob, g_mix, w_gate_bf, w_a_bf, w_b_bf, w_o_bf, g_ffn)


def rotary_tables(pos):
    inv = ROPE_THETA ** (-jnp.arange(0, HEAD_DIM, 2, dtype=jnp.float32) / HEAD_DIM)
    ang = pos.astype(jnp.float32)[:, None] * inv[None, :]
    cos, sin = jnp.cos(ang), jnp.sin(ang)
    reps = LANES // HEAD_DIM
    return jnp.tile(jnp.concatenate([cos, cos], axis=1), (1, reps)), jnp.tile(jnp.concatenate([-sin, sin], axis=1), (1, reps))


def kernel(x_prompt, x_sample, cache_a_k, cache_a_v, cache_b_k, cache_b_v, norm_mix, w_in,
           w_branch_a, w_branch_b, w_out, sink_b, norm_ffn, w_peer_q, peer_sub_keys, peer_u,
           peer_v, norm_final):
    assert DEPTH == 1
    bf16 = jnp.bfloat16
    n, seq, _ = x_prompt.shape
    ns, dec = x_sample.shape[:2]
    assert dec == 1
    w_in_bf = w_in[0].astype(bf16)
    w_qkv, w_gate = w_in_bf[:, :QKV_WIDTH], w_in_bf[:, QKV_WIDTH:]
    proj_weights = (norm_mix[0], w_gate, w_branch_a[0].astype(bf16), w_branch_b[0].astype(bf16),
                    w_out[0].astype(bf16), norm_ffn[0])
    sink = sink_b[0].astype(jnp.float32)

    xp = x_prompt.reshape(n * seq, D_MODEL)
    cos, sin = rotary_tables(jnp.tile(jnp.arange(seq), n))
    qa, ka, va, qb, kb, vb, ka_t, va_t, kb_t, vb_t = in_proj(xp, norm_mix[0], w_qkv, cos, sin, PROJ_TOKEN_BLOCK, seq)
    per_seq = lambda a: a.reshape(n, seq, a.shape[-1])
    oa, ob = prompt_attention(per_seq(qa), per_seq(ka), per_seq(va), per_seq(qb), per_seq(kb), per_seq(vb), sink)
    hp, hnt_p = out_proj(xp, oa.reshape(n * seq, WIDTH_A), ob.reshape(n * seq, WIDTH_B), *proj_weights,
                         PROJ_TOKEN_BLOCK)
    windows = lambda a, heads: a.reshape(n, heads, HEAD_DIM, a.shape[-1]).transpose(0, 3, 1, 2)[None]
    state_p = (windows(ka_t, HEADS_A), windows(va_t, HEADS_A), windows(kb_t, KV_HEADS_B), windows(vb_t, KV_HEADS_B))

    xs = x_sample.reshape(ns, D_MODEL)
    cos, sin = rotary_tables(jnp.full((ns,), PAST_LEN))
    qa, ka, va, qb, kb, vb = in_proj(xs, norm_mix[0], w_qkv, cos, sin, ns, None)
    oa, ob, state_s = sample_mixers(qa, ka, va, qb, kb, vb, sink, cache_a_k[0], cache_a_v[0],
                                    cache_b_k[0], cache_b_v[0])
    hs, hnt_s = out_proj(xs, oa, ob, *proj_weights, ns)

    y_all = peer_block(jnp.concatenate([hp, hs], axis=0), jnp.concatenate([hnt_p, hnt_s], axis=1),
                       w_peer_q[0], peer_sub_keys[0], peer_u[0], peer_v[0], norm_final)
    y_prompt = y_all[:n * seq].reshape(x_prompt.shape)
    y_sample = y_all[n * seq:].reshape(x_sample.shape)
    return (y_prompt, y_sample, *state_p, *[a[None] for a in state_s])
```

```python
import functools
import jax, jax.numpy as jnp
from jax import lax
import numpy as np
from jax.experimental import pallas as pl
from jax.experimental.pallas import tpu as pltpu

D_MODEL = 1024
BATCH = 8
SEQ = 2048
DEPTH = 1
DEC_BATCH = 128
DEC_SEQ = 1
PAST_LEN = 8192

HEAD_DIM = 64
HEADS_A = 8
DILATED_CONFIGS = ((128, 1), (512, 4), (2048, 16))
WINDOW_A = 2048
HEADS_B = 8
KV_HEADS_B = 2
GROUP_B = HEADS_B // KV_HEADS_B
WINDOW_B = 128
BLOCK = 128
ROPE_THETA = 10000.0
NORM_EPS = 1e-6
NEG_INF = -1e30
SCALE = HEAD_DIM ** -0.5

WIDTH_A = HEADS_A * HEAD_DIM
WIDTH_B = HEADS_B * HEAD_DIM
KV_WIDTH_B = KV_HEADS_B * HEAD_DIM
IN_WIDTHS = (WIDTH_A, WIDTH_A, WIDTH_A, WIDTH_B, KV_WIDTH_B, KV_WIDTH_B, D_MODEL, D_MODEL)
D_IN = sum(IN_WIDTHS)
SPLIT_POINTS = tuple(int(v) for v in np.cumsum(IN_WIDTHS)[:-1])

N_KEYS = 128
N_EXPERTS = N_KEYS * N_KEYS
PEER_HEADS = 8
PEER_TOPK = 16
PEER_HALF = 128
PEER_QUERY_DIM = 2 * PEER_HALF
PEER_CHUNK = 128


LANES = 128
SAMPLE_VMEM_LIMIT = 48 * 1024 * 1024
KEYS_PER_BRANCH = 128


def _bf16_round(x):
    return x.astype(jnp.bfloat16).astype(jnp.float32)


def _decode_softmax(s, s_new, sink):
    m = jnp.maximum(jnp.max(s, axis=-1, keepdims=True), s_new)
    if sink is not None:
        m = jnp.maximum(m, sink)
    e = jnp.exp(s - m)
    e_new = jnp.exp(s_new - m)
    denom = jnp.sum(e, axis=-1, keepdims=True) + e_new
    if sink is not None:
        denom = denom + jnp.exp(sink - m)
    return e / denom, e_new / denom, m + jnp.log(denom)


def _sample_mixer_body(qa_ref, ka_ref, va_ref, qb_ref, kb_ref, vb_ref, sink_ref,
                       cak_ref, cav_ref, cbk_ref, cbv_ref,
                       oa_ref, ob_ref, nak_ref, nav_ref, nbk_ref, nbv_ref, s_scr, o_scr):
    f32, bf16 = jnp.float32, jnp.bfloat16
    nt = (((1,), (1,)), ((), ()))
    win_a = cak_ref.shape[2]
    lane_tiles = win_a // LANES

    def as_column(row):
        return jnp.broadcast_to(row, (LANES, row.shape[1])).T

    def shifted(old, col):
        rolled = pltpu.roll(old, old.shape[1] - 1, axis=1)
        pos = lax.broadcasted_iota(jnp.int32, old.shape, 1)
        return jnp.where(pos == old.shape[1] - 1, jnp.tile(col, (1, old.shape[1] // LANES)), rolled)

    qa, ka, va = qa_ref[0], ka_ref[0], va_ref[0]
    own = (lax.broadcasted_iota(jnp.int32, (HEADS_A, WIDTH_A), 1) // HEAD_DIM
           == lax.broadcasted_iota(jnp.int32, (HEADS_A, WIDTH_A), 0))
    s_new = jnp.sum(jnp.where(own, qa * ka, 0.0), axis=-1, keepdims=True) * SCALE
    q_col = as_column(qa)
    k_col = as_column(ka)
    v_col = as_column(va)
    for h in range(HEADS_A):
        rows = slice(h * HEAD_DIM, (h + 1) * HEAD_DIM)
        qh = q_col[rows, :]

        def logits(t, carry, rows=rows, qh=qh, h=h):
            lanes = pl.ds(pl.multiple_of(t * LANES, LANES), LANES)
            s_scr[pl.ds(h, 1), lanes] = jnp.sum(cak_ref[0, rows, lanes] * qh, axis=0, keepdims=True)
            return carry

        lax.fori_loop(0, lane_tiles, logits, 0)
        nak_ref[0, rows, :] = shifted(cak_ref[0, rows, :], k_col[rows, :])

    s = s_scr[...] * SCALE
    pos = lax.broadcasted_iota(jnp.int32, s.shape, 1)
    ps, p_news, lses = [], [], []
    for window, dil in DILATED_CONFIGS:
        reach = (pos >= win_a - window) & (pos % dil == 0)
        p, p_new, lse = _decode_softmax(jnp.where(reach, s, NEG_INF), s_new, None)
        ps.append(p)
        p_news.append(p_new)
        lses.append(lse)
    top = jnp.maximum(jnp.maximum(lses[0], lses[1]), lses[2])
    ws = [jnp.exp(l - top) for l in lses]
    inv = 1.0 / (ws[0] + ws[1] + ws[2])
    s_scr[...] = (ws[0] * ps[0] + ws[1] * ps[1] + ws[2] * ps[2]) * inv
    p_new = (ws[0] * p_news[0] + ws[1] * p_news[1] + ws[2] * p_news[2]) * inv

    for h in range(HEADS_A):
        rows = slice(h * HEAD_DIM, (h + 1) * HEAD_DIM)

        def weighted(t, acc, rows=rows, h=h):
            lanes = pl.ds(pl.multiple_of(t * LANES, LANES), LANES)
            return acc + cav_ref[0, rows, lanes] * s_scr[pl.ds(h, 1), lanes]

        acc = lax.fori_loop(0, lane_tiles, weighted, jnp.zeros((HEAD_DIM, LANES), f32))
        o_scr[rows, :] = jnp.broadcast_to(jnp.sum(acc, axis=1, keepdims=True), (HEAD_DIM, LANES))
        nav_ref[0, rows, :] = shifted(cav_ref[0, rows, :], v_col[rows, :])
    p_new_lanes = jnp.sum(jnp.where(own, p_new, 0.0), axis=0, keepdims=True)
    oa_ref[0] = o_scr[...].T[0:1, :] + p_new_lanes * va

    qb, kb, vb = qb_ref[0], kb_ref[0], vb_ref[0]
    lane = lax.broadcasted_iota(jnp.int32, (1, KV_WIDTH_B), 1)
    heads_per_chunk = KV_WIDTH_B // HEAD_DIM
    q_rows = []
    for h in range(HEADS_B):
        c = h // heads_per_chunk
        piece = qb[:, c * KV_WIDTH_B:(c + 1) * KV_WIDTH_B]
        if h % heads_per_chunk != h // GROUP_B:
            piece = pltpu.roll(piece, HEAD_DIM, axis=1)
        q_rows.append(jnp.where(lane // HEAD_DIM == h // GROUP_B, piece, 0.0))
    q_rows = jnp.concatenate(q_rows, axis=0)
    s = jnp.dot(q_rows.astype(bf16), cbk_ref[0].astype(bf16), preferred_element_type=f32) * SCALE
    s_new = jnp.sum(_bf16_round(q_rows) * _bf16_round(kb), axis=-1, keepdims=True) * SCALE
    p, p_new, _ = _decode_softmax(s, s_new, sink_ref[...])
    o = (lax.dot_general(p.astype(bf16), cbv_ref[0].astype(bf16), nt, preferred_element_type=f32)
         + _bf16_round(p_new) * _bf16_round(vb))
    for c in range(WIDTH_B // KV_WIDTH_B):
        halves = []
        for slot in range(heads_per_chunk):
            h = c * heads_per_chunk + slot
            r = o[h:h + 1, :]
            if h // GROUP_B != slot:
                r = pltpu.roll(r, HEAD_DIM, axis=1)
            halves.append(r)
        ob_ref[0, :, c * KV_WIDTH_B:(c + 1) * KV_WIDTH_B] = jnp.where(lane < HEAD_DIM, halves[0], halves[1])

    nbk_ref[0] = shifted(cbk_ref[0], as_column(kb))
    nbv_ref[0] = shifted(cbv_ref[0], as_column(vb))


def sample_mixers(qa, ka, va, qb, kb, vb, sink, ck_a, cv_a, ck_b, cv_b):
    n = qa.shape[0]
    assert ck_a.shape[1] == WINDOW_A and ck_b.shape[1] == WINDOW_B
    assert KV_WIDTH_B == 2 * HEAD_DIM == LANES
    row = lambda a, w: a.reshape(n, 1, w)
    win = lambda a, w: a.transpose(0, 2, 3, 1).reshape(n, w, a.shape[1])
    unwin = lambda a, like: a.reshape(n, like.shape[2], like.shape[3], like.shape[1]).transpose(0, 3, 1, 2)
    row_spec = lambda w: pl.BlockSpec((1, 1, w), lambda b: (b, 0, 0))
    win_spec = lambda r, w: pl.BlockSpec((1, w, r), lambda b: (b, 0, 0))
    f32 = jnp.float32
    oa, ob, nak, nav, nbk, nbv = pl.pallas_call(
        _sample_mixer_body,
        grid=(n,),
        in_specs=[row_spec(WIDTH_A), row_spec(WIDTH_A), row_spec(WIDTH_A),
                  row_spec(WIDTH_B), row_spec(KV_WIDTH_B), row_spec(KV_WIDTH_B),
                  pl.BlockSpec((HEADS_B, 1), lambda b: (0, 0)),
                  win_spec(WINDOW_A, WIDTH_A), win_spec(WINDOW_A, WIDTH_A),
                  win_spec(WINDOW_B, KV_WIDTH_B), win_spec(WINDOW_B, KV_WIDTH_B)],
        out_specs=[row_spec(WIDTH_A), row_spec(WIDTH_B),
                   win_spec(WINDOW_A, WIDTH_A), win_spec(WINDOW_A, WIDTH_A),
                   win_spec(WINDOW_B, KV_WIDTH_B), win_spec(WINDOW_B, KV_WIDTH_B)],
        out_shape=[jax.ShapeDtypeStruct((n, 1, WIDTH_A), f32), jax.ShapeDtypeStruct((n, 1, WIDTH_B), f32),
                   jax.ShapeDtypeStruct((n, WIDTH_A, WINDOW_A), f32), jax.ShapeDtypeStruct((n, WIDTH_A, WINDOW_A), f32),
                   jax.ShapeDtypeStruct((n, KV_WIDTH_B, WINDOW_B), f32),
                   jax.ShapeDtypeStruct((n, KV_WIDTH_B, WINDOW_B), f32)],
        scratch_shapes=[pltpu.VMEM((HEADS_A, WINDOW_A), f32), pltpu.VMEM((WIDTH_A, LANES), f32)],
        compiler_params=pltpu.CompilerParams(
            dimension_semantics=("parallel",),
            vmem_limit_bytes=SAMPLE_VMEM_LIMIT),
        name="sample_mixers",
    )(row(qa, WIDTH_A), row(ka, WIDTH_A), row(va, WIDTH_A), row(qb, WIDTH_B), row(kb, KV_WIDTH_B),
      row(vb, KV_WIDTH_B), sink.reshape(HEADS_B, 1),
      win(ck_a, WIDTH_A), win(cv_a, WIDTH_A), win(ck_b, KV_WIDTH_B), win(cv_b, KV_WIDTH_B))
    state = (unwin(nak, ck_a), unwin(nav, cv_a), unwin(nbk, ck_b), unwin(nbv, cv_b))
    return oa.reshape(n, WIDTH_A), ob.reshape(n, WIDTH_B), state


PEER_TOKEN_BLOCK = 512
PEER_EXPERT_BLOCK = 1024
PEER_ROWS = 16
PEER_PIPE_ROWS = 256
ROUTE_STREAMS = 2
TOP_ROWS = 24
PEER_VMEM_LIMIT = 48 * 1024 * 1024
INV_SQRT2 = 0.7071067811865476


def _peer_route_body(hnT_ref, wqT_ref, keys_ref, a_ref, e_ref, q_scr, s_scr, top_scr, thr_scr, invz_scr):
    tb = PEER_TOKEN_BLOCK
    lane_tiles = tb // LANES
    q_scr[...] = jnp.dot(wqT_ref[...], hnT_ref[...], preferred_element_type=jnp.float32).astype(jnp.bfloat16)
    for hp in range(2 * PEER_HEADS):
        s_scr[hp] = jnp.dot(keys_ref[hp % 2], q_scr[hp * PEER_HALF:(hp + 1) * PEER_HALF, :],
                            preferred_element_type=jnp.float32)

    def take_max(x, iota, n):
        m = jnp.max(x, axis=0, keepdims=True)
        first = jnp.min(jnp.where(x == m, iota, float(n)), axis=0, keepdims=True)
        return m, jnp.where(iota == first, -jnp.inf, x)

    groups = lane_tiles // ROUTE_STREAMS

    def tile_lanes(u, k):
        return pl.ds(pl.multiple_of(((u % groups) * ROUTE_STREAMS + k) * LANES, LANES), LANES)

    def half_top(u, carry):
        hp = u // groups
        iota = lax.broadcasted_iota(jnp.int32, (N_KEYS, LANES), 0).astype(jnp.float32)
        lanes = [tile_lanes(u, k) for k in range(ROUTE_STREAMS)]
        xs = [s_scr[hp, :, ln] for ln in lanes]
        for ln in lanes:
            top_scr[hp, PEER_TOPK:, ln] = jnp.full((TOP_ROWS - PEER_TOPK, LANES), -jnp.inf, jnp.float32)
        for r in range(PEER_TOPK + 1):
            for k, ln in enumerate(lanes):
                m, xs[k] = take_max(xs[k], iota, N_KEYS)
                top_scr[hp, pl.ds(r, 1), ln] = m
        return carry

    lax.fori_loop(0, 2 * PEER_HEADS * groups, half_top, 0)

    def pair_top(u, carry):
        h = u // groups
        lanes = [tile_lanes(u, k) for k in range(ROUTE_STREAMS)]
        xs = []
        for ln in lanes:
            t1 = top_scr[2 * h, :, ln]
            t2 = top_scr[2 * h + 1, :, ln]
            xs.append(jnp.concatenate([t1[0:1, :] + t2] + [t1[k:k + 1, :] + t2[0:8, :] for k in range(1, 8)]
                                      + [t1[8:, :] + t2[0:1, :]], axis=0))
        n = xs[0].shape[0]
        iota = lax.broadcasted_iota(jnp.int32, (n, LANES), 0).astype(jnp.float32)
        best, v, z = [None] * ROUTE_STREAMS, [None] * ROUTE_STREAMS, [None] * ROUTE_STREAMS
        for r in range(PEER_TOPK):
            for k in range(ROUTE_STREAMS):
                v[k], xs[k] = take_max(xs[k], iota, n)
                if r == 0:
                    best[k], z[k] = v[k], jnp.ones_like(v[k])
                else:
                    z[k] = z[k] + jnp.exp(v[k] - best[k])
        for k, ln in enumerate(lanes):
            nxt, _ = take_max(xs[k], iota, n)
            thr_scr[h, :, ln] = 0.5 * (v[k] + nxt)
            invz_scr[h, :, ln] = 1.0 / z[k]
        return carry

    lax.fori_loop(0, PEER_HEADS * groups, pair_top, 0)

    def emit(u, carry):
        h = u // (N_KEYS // PEER_ROWS)
        rows = pl.ds(pl.multiple_of((u % (N_KEYS // PEER_ROWS)) * PEER_ROWS, PEER_ROWS), PEER_ROWS)
        s1 = s_scr[2 * h, rows, :]
        s2 = s_scr[2 * h + 1, rows, :]
        a_ref[2 * h, rows, :] = thr_scr[h] - s1
        a_ref[2 * h + 1, rows, :] = s2
        e_ref[2 * h, rows, :] = jnp.exp(s1 - top_scr[2 * h, pl.ds(0, 1), :]) * invz_scr[h]
        e_ref[2 * h + 1, rows, :] = jnp.exp(s2 - top_scr[2 * h + 1, pl.ds(0, 1), :])
        return carry

    lax.fori_loop(0, PEER_HEADS * (N_KEYS // PEER_ROWS), emit, 0)


def _peer_route(hnT, wqT, keys_bf):
    t_pad = hnT.shape[1]
    tb = PEER_TOKEN_BLOCK
    hp = 2 * PEER_HEADS
    tok3 = pl.BlockSpec((None, hp, N_KEYS, tb), lambda i: (i, 0, 0, 0))
    return pl.pallas_call(
        _peer_route_body,
        grid=(t_pad // tb,),
        in_specs=[
            pl.BlockSpec((D_MODEL, tb), lambda i: (0, i)),
            pl.BlockSpec((hp * PEER_HALF, D_MODEL), lambda i: (0, 0)),
            pl.BlockSpec((2, N_KEYS, PEER_HALF), lambda i: (0, 0, 0)),
        ],
        out_specs=[tok3, tok3],
        out_shape=[jax.ShapeDtypeStruct((t_pad // tb, hp, N_KEYS, tb), jnp.float32)] * 2,
        scratch_shapes=[
            pltpu.VMEM((hp * PEER_HALF, tb), jnp.bfloat16),
            pltpu.VMEM((hp, N_KEYS, tb), jnp.float32),
            pltpu.VMEM((hp, TOP_ROWS, tb), jnp.float32),
            pltpu.VMEM((PEER_HEADS, 1, tb), jnp.float32),
            pltpu.VMEM((PEER_HEADS, 1, tb), jnp.float32),
        ],
        compiler_params=pltpu.CompilerParams(
            dimension_semantics=("parallel",),
            vmem_limit_bytes=PEER_VMEM_LIMIT),
        name="peer_route",
    )(hnT, wqT, keys_bf)


def _peer_expert_body(hnT_ref, u_ref, vT_ref, a_ref, e_ref, res_ref, gfin_ref, o_ref,
                      h0_scr, h1_scr, a0_scr, a1_scr, acc_scr):
    j = pl.program_id(1)
    last = pl.num_programs(1) - 1
    f32 = jnp.float32
    slots = ((h0_scr, a0_scr), (h1_scr, a1_scr))

    @pl.when(j == 0)
    def _():
        acc_scr[...] = jnp.zeros_like(acc_scr)
        a1_scr[...] = jnp.zeros_like(a1_scr)
        h0_scr[...] = jnp.dot(u_ref[...], hnT_ref[...], preferred_element_type=f32)

    def steady(h_cur, a_cur, h_prv, a_prv):
        tile = PEER_PIPE_ROWS

        def gating(row0):
            i1 = (j - 1) * (PEER_EXPERT_BLOCK // N_KEYS) + row0 // N_KEYS
            r = row0 % N_KEYS
            gate = jnp.zeros((PEER_ROWS, PEER_TOKEN_BLOCK), f32)
            for h in range(PEER_HEADS):
                need = a_ref[2 * h, pl.ds(i1, 1), :]
                e1row = e_ref[2 * h, pl.ds(i1, 1), :]
                val = e_ref[2 * h + 1, r:r + PEER_ROWS, :] * e1row
                gate = gate + jnp.where(a_ref[2 * h + 1, r:r + PEER_ROWS, :] >= need, val, 0.0)
            x = h_prv[row0:row0 + PEER_ROWS, :]
            act = 0.5 * x * (1.0 + lax.erf(x * INV_SQRT2))
            a_prv[row0:row0 + PEER_ROWS, :] = (act * gate).astype(jnp.bfloat16)

        def pre_activation(span, cols):
            h_cur[span, cols] = jnp.dot(u_ref[span, :], hnT_ref[:, cols], preferred_element_type=f32)

        def accumulate(span, out_rows):
            acc_scr[out_rows, :] += jnp.dot(vT_ref[out_rows, span], a_cur[span, :], preferred_element_type=f32)

        chunks_per_tile = tile // PEER_ROWS
        for s in range(PEER_EXPERT_BLOCK // tile):
            span = slice(s * tile, (s + 1) * tile)
            mxu_work = [functools.partial(pre_activation, span, slice(c * tile, (c + 1) * tile))
                        for c in range(PEER_TOKEN_BLOCK // tile)]
            mxu_work += [functools.partial(accumulate, span, slice(m * tile, (m + 1) * tile))
                         for m in range(D_MODEL // tile)]
            every = chunks_per_tile // len(mxu_work)
            for c in range(chunks_per_tile):
                if c % every == 0 and c // every < len(mxu_work):
                    mxu_work[c // every]()
                gating(s * tile + c * PEER_ROWS)

    for parity in (0, 1):
        pl.when((j > 0) & (j < last) & (j % 2 == parity))(
            functools.partial(steady, *slots[parity], *slots[1 - parity]))

    @pl.when(j == last)
    def _():
        acc = acc_scr[...] + jnp.dot(vT_ref[...], a1_scr[...], preferred_element_type=f32)
        y = res_ref[...] + acc.T
        y = y * lax.rsqrt(jnp.mean(y * y, axis=-1, keepdims=True) + NORM_EPS)
        o_ref[...] = y * gfin_ref[...]


def _peer_experts(hnT, a, e, res, g_final, u_bf, vT_bf):
    t_pad = hnT.shape[1]
    tb, eb = PEER_TOKEN_BLOCK, PEER_EXPERT_BLOCK
    n_blocks = N_EXPERTS // eb
    tok3 = pl.BlockSpec((None, 2 * PEER_HEADS, N_KEYS, tb), lambda i, j: (i, 0, 0, 0))
    return pl.pallas_call(
        _peer_expert_body,
        grid=(t_pad // tb, n_blocks + 2),
        in_specs=[
            pl.BlockSpec((D_MODEL, tb), lambda i, j: (0, i)),
            pl.BlockSpec((eb, D_MODEL), lambda i, j: (jnp.minimum(j, n_blocks - 1), 0)),
            pl.BlockSpec((None, D_MODEL, eb), lambda i, j: (jnp.clip(j - 2, 0, n_blocks - 1), 0, 0)),
            tok3, tok3,
            pl.BlockSpec((tb, D_MODEL), lambda i, j: (i, 0)),
            pl.BlockSpec((1, D_MODEL), lambda i, j: (0, 0)),
        ],
        out_specs=pl.BlockSpec((tb, D_MODEL), lambda i, j: (i, 0)),
        out_shape=jax.ShapeDtypeStruct((t_pad, D_MODEL), jnp.float32),
        scratch_shapes=[
            pltpu.VMEM((eb, tb), jnp.float32), pltpu.VMEM((eb, tb), jnp.float32),
            pltpu.VMEM((eb, tb), jnp.bfloat16), pltpu.VMEM((eb, tb), jnp.bfloat16),
            pltpu.VMEM((D_MODEL, tb), jnp.float32),
        ],
        compiler_params=pltpu.CompilerParams(
            dimension_semantics=("parallel", "arbitrary"),
            vmem_limit_bytes=PEER_VMEM_LIMIT),
        name="peer_experts",
    )(hnT, u_bf, vT_bf, a, e, res, g_final.reshape(1, D_MODEL))


def peer_block(h, hnT, w_q, sub_keys, u_tab, v_tab, g_final):
    t = h.shape[0]
    pad = -t % PEER_TOKEN_BLOCK
    hnT = jnp.pad(hnT, ((0, 0), (0, pad)))
    a, e = _peer_route(hnT, w_q.astype(jnp.bfloat16).T, sub_keys.astype(jnp.bfloat16))
    out = _peer_experts(hnT, a, e, jnp.pad(h, ((0, pad), (0, 0))), g_final,
                        u_tab.astype(jnp.bfloat16),
                        v_tab.astype(jnp.bfloat16).reshape(-1, PEER_EXPERT_BLOCK, D_MODEL).transpose(0, 2, 1))
    return out[:t]


ATTN_VMEM_LIMIT = 40 * 1024 * 1024
HEADS_PER_TILE = LANES // HEAD_DIM
ATTN_STREAMS = 4


def _band_units(units, sinks):
    f32, bf16 = jnp.float32, jnp.bfloat16
    nt = (((1,), (1,)), ((), ()))
    rows = HEADS_PER_TILE * BLOCK
    qi = lax.broadcasted_iota(jnp.int32, (rows, 2 * BLOCK), 0) % BLOCK
    kj = lax.broadcasted_iota(jnp.int32, (rows, 2 * BLOCK), 1)
    off = BLOCK + qi - kj
    band = (off >= 0) & (off <= BLOCK)
    own_block = kj >= BLOCK
    ss = []
    for qs, load_k, _, key_lanes, _ in units:
        lhs = jnp.concatenate([jnp.where(key_lanes[i], qs[i], 0.0) for i in range(HEADS_PER_TILE)], axis=0)
        ss.append(lax.dot_general(lhs.astype(bf16), load_k().astype(bf16), nt, preferred_element_type=f32) * SCALE)
    ss = [jnp.where(band & (own_block | jnp.logical_not(unit[4])), s, NEG_INF) for s, unit in zip(ss, units)]
    if sinks is not None:
        assert HEADS_PER_TILE == 2
        sink_slot = kj == (qi + BLOCK + 1) % (2 * BLOCK)
        head0 = lax.broadcasted_iota(jnp.int32, (rows, 2 * BLOCK), 0) < BLOCK
        sink_logit = jnp.where(head0, sinks[0], sinks[1])
        ss = [jnp.where(sink_slot, sink_logit, s) for s in ss]
    ms = [jnp.max(s, axis=-1, keepdims=True) for s in ss]
    es = [jnp.exp(s - m) for s, m in zip(ss, ms)]
    denoms = [jnp.sum(e, axis=-1, keepdims=True) for e in es]
    results = []
    for e, d, m, unit in zip(es, denoms, ms, units):
        p = e / d
        if sinks is not None:
            p = jnp.where(sink_slot, 0.0, p)
        o = jnp.dot(p.astype(bf16), unit[2]().astype(bf16), preferred_element_type=f32)
        lse = m + jnp.log(d)
        results.append([(o[i * BLOCK:(i + 1) * BLOCK, :], lse[i * BLOCK:(i + 1) * BLOCK, :])
                        for i in range(HEADS_PER_TILE)])
    return results


def _prompt_attention_body(qa_ref, ka_ref, va_ref, qb_ref, kb_ref, vb_ref, sink_ref, oa_ref, ob_ref,
                           o_scr, lse_scr):
    f32 = jnp.float32
    seq = qa_ref.shape[1]
    lane = lax.broadcasted_iota(jnp.int32, (1, LANES), 1)
    low = lane < HEAD_DIM
    own = [low, jnp.logical_not(low)]

    def window(ref, prev, cur):
        return lambda: jnp.concatenate([ref[0, prev, :], ref[0, cur, :]], axis=0)

    for c, (reach, dil) in enumerate(DILATED_CONFIGS):
        assert reach // dil == BLOCK
        blocks = seq // (dil * BLOCK)

        def step(it, carry, c=c, dil=dil, blocks=blocks):
            units, curs = [], []
            for k in range(ATTN_STREAMS):
                u = it * ATTN_STREAMS + k
                res, blk = u // blocks, u % blocks
                cur = pl.ds(res + dil * BLOCK * blk, BLOCK, stride=dil)
                prev = pl.ds(res + dil * BLOCK * jnp.maximum(blk - 1, 0), BLOCK, stride=dil)
                q = qa_ref[0, cur, :]
                units.append(([q, q], window(ka_ref, prev, cur), window(va_ref, prev, cur), own, blk == 0))
                curs.append(cur)
            for cur, ((o0, l0), (o1, l1)) in zip(curs, _band_units(units, None)):
                o_scr[c, cur, :] = jnp.where(low, o0, o1)
                lse_scr[c, cur, :] = jnp.where(low, l0, l1)
            return carry

        lax.fori_loop(0, dil * blocks // ATTN_STREAMS, step, 0)

    def merge(t, carry):
        rows = pl.ds(pl.multiple_of(t * BLOCK, BLOCK), BLOCK)
        ls = [lse_scr[c, rows, :] for c in range(len(DILATED_CONFIGS))]
        top = jnp.maximum(jnp.maximum(ls[0], ls[1]), ls[2])
        ws = [jnp.exp(l - top) for l in ls]
        num = ws[0] * o_scr[0, rows, :] + ws[1] * o_scr[1, rows, :] + ws[2] * o_scr[2, rows, :]
        oa_ref[0, rows, :] = num / (ws[0] + ws[1] + ws[2])
        return carry

    lax.fori_loop(0, seq // BLOCK, merge, 0)

    slab = pl.program_id(1)
    sinks = [sink_ref[slab * HEADS_PER_TILE + i] for i in range(HEADS_PER_TILE)]

    def step_b(it, carry):
        units, curs = [], []
        for k in range(ATTN_STREAMS):
            blk = it * ATTN_STREAMS + k
            cur = pl.ds(pl.multiple_of(blk * BLOCK, BLOCK), BLOCK)
            prev = pl.ds(pl.multiple_of(jnp.maximum(blk - 1, 0) * BLOCK, BLOCK), BLOCK)
            q = qb_ref[0, cur, :]
            units.append(([q, q], window(kb_ref, prev, cur), window(vb_ref, prev, cur), own, blk == 0))
            curs.append(cur)
        for cur, ((o0, _), (o1, _)) in zip(curs, _band_units(units, sinks)):
            ob_ref[0, cur, :] = jnp.where(low, o0, o1)
        return carry

    lax.fori_loop(0, seq // BLOCK // ATTN_STREAMS, step_b, 0)


def prompt_attention(qa, ka, va, qb, kb, vb, sink):
    n, seq, _ = qa.shape
    assert kb.shape[-1] == KV_HEADS_B * LANES and seq % (BLOCK * max(d for _, d in DILATED_CONFIGS)) == 0
    slab = pl.BlockSpec((1, seq, LANES), lambda b, p: (b, 0, p))
    whole = pl.BlockSpec((1, seq, LANES), lambda b, p: (b, 0, p * HEADS_PER_TILE // GROUP_B))
    f32 = jnp.float32
    return pl.pallas_call(
        _prompt_attention_body,
        grid=(n, WIDTH_A // LANES),
        in_specs=[slab, slab, slab, slab, whole, whole, pl.BlockSpec(memory_space=pltpu.SMEM)],
        out_specs=[slab, slab],
        out_shape=[jax.ShapeDtypeStruct((n, seq, WIDTH_A), f32), jax.ShapeDtypeStruct((n, seq, WIDTH_B), f32)],
        scratch_shapes=[pltpu.VMEM((len(DILATED_CONFIGS), seq, LANES), f32),
                        pltpu.VMEM((len(DILATED_CONFIGS), seq, LANES), f32)],
        compiler_params=pltpu.CompilerParams(
            dimension_semantics=("parallel", "parallel"),
            vmem_limit_bytes=ATTN_VMEM_LIMIT),
        name="prompt_attention",
    )(qa, ka, va, qb, kb, vb, sink.reshape(HEADS_B))


PROJ_VMEM_LIMIT = 48 * 1024 * 1024
PROJ_TOKEN_BLOCK = 512
QKV_WIDTHS = IN_WIDTHS[:6]
QKV_WIDTH = sum(QKV_WIDTHS)
QKV_ROTATED = (True, True, False, True, True, False)


def _rms_normed(x, g):
    return x * lax.rsqrt(jnp.mean(x * x, axis=-1, keepdims=True) + NORM_EPS) * g


def _rope_slab(x, cos, sin_signed):
    lane = lax.broadcasted_iota(jnp.int32, (1, LANES), 1)
    half = HEAD_DIM // 2
    partner = jnp.where(lane % HEAD_DIM < half, pltpu.roll(x, LANES - half, axis=1), pltpu.roll(x, half, axis=1))
    return x * cos + partner * sin_signed


def _in_proj_body(x_ref, g_ref, w_ref, cos_ref, sin_ref, *out_refs, widths, channel_major):
    xn = _rms_normed(x_ref[...], g_ref[...]).astype(jnp.bfloat16)
    cos, sin = cos_ref[...], sin_ref[...]
    groups = []
    c0 = 0
    for width, rotated in zip(widths, QKV_ROTATED):
        z = jnp.dot(xn, w_ref[:, c0:c0 + width], preferred_element_type=jnp.float32)
        if rotated:
            z = jnp.concatenate([_rope_slab(z[:, c:c + LANES], cos, sin) for c in range(0, width, LANES)], axis=1)
        groups.append(z)
        c0 += width
    for ref, z in zip(out_refs[:6], groups):
        ref[...] = z
    if channel_major:
        kaT_ref, vaT_ref, kbT_ref, vbT_ref = out_refs[6:]
        kaT_ref[0] = groups[1].T
        vaT_ref[0] = groups[2].T
        tb = x_ref.shape[0]
        low = lax.broadcasted_iota(jnp.int32, (1, LANES), 1) < HEAD_DIM
        for ref, z in ((kbT_ref, groups[4]), (vbT_ref, groups[5])):
            tail = z[tb - WINDOW_B:, :]
            ref[0] = jnp.where(low, tail[:, :LANES], tail[:, LANES:]).T


def in_proj(x, g_mix, w_bf, cos, sin, tb, seq):
    t = x.shape[0]
    channel_major = seq is not None
    per_seq = seq // tb if channel_major else None
    widths = QKV_WIDTHS[:4] + ((2 * KV_WIDTH_B,) * 2 if channel_major else QKV_WIDTHS[4:])
    assert w_bf.shape[1] == sum(widths)
    f32 = jnp.float32
    tok = lambda w: pl.BlockSpec((tb, w), lambda i: (i, 0))
    out_specs = [tok(w) for w in widths]
    out_shape = [jax.ShapeDtypeStruct((t, w), f32) for w in widths]
    if channel_major:
        assert seq == WINDOW_A and tb >= WINDOW_B and KV_WIDTH_B == LANES
        n = t // seq
        out_specs += [pl.BlockSpec((1, WIDTH_A, tb), lambda i: (i // per_seq, 0, i % per_seq))] * 2
        out_specs += [pl.BlockSpec((1, KV_WIDTH_B, WINDOW_B), lambda i: (i // per_seq, 0, 0))] * 2
        out_shape += [jax.ShapeDtypeStruct((n, WIDTH_A, seq), f32)] * 2
        out_shape += [jax.ShapeDtypeStruct((n, KV_WIDTH_B, WINDOW_B), f32)] * 2
    return pl.pallas_call(
        functools.partial(_in_proj_body, widths=widths, channel_major=channel_major),
        grid=(t // tb,),
        in_specs=[tok(D_MODEL), pl.BlockSpec((1, D_MODEL), lambda i: (0, 0)),
                  pl.BlockSpec(w_bf.shape, lambda i: (0, 0)), tok(LANES), tok(LANES)],
        out_specs=out_specs,
        out_shape=out_shape,
        compiler_params=pltpu.CompilerParams(
            dimension_semantics=("arbitrary",),
            vmem_limit_bytes=PROJ_VMEM_LIMIT),
        name="in_proj",
    )(x, g_mix.reshape(1, D_MODEL), w_bf, cos, sin)


def duplicate_kv_columns(w_qkv):
    edges = np.cumsum((0,) + QKV_WIDTHS)
    cols = np.arange(KV_WIDTH_B).reshape(KV_HEADS_B, 1, HEAD_DIM)
    cols = np.broadcast_to(cols, (KV_HEADS_B, LANES // HEAD_DIM, HEAD_DIM)).reshape(-1)
    return jnp.concatenate([w_qkv[:, :edges[4]], w_qkv[:, edges[4] + cols], w_qkv[:, edges[5] + cols]], axis=1)


def _out_proj_body(x_ref, oa_ref, ob_ref, gmix_ref, wg_ref, wa_ref, wb_ref, wo_ref, gffn_ref, h_ref, hnT_ref):
    f32, bf16 = jnp.float32, jnp.bfloat16
    x = x_ref[...]
    xn = _rms_normed(x, gmix_ref[...]).astype(bf16)
    ya = jnp.dot(oa_ref[...].astype(bf16), wa_ref[...], preferred_element_type=f32)
    merged = jax.nn.sigmoid(jnp.dot(xn, wg_ref[:, :D_MODEL], preferred_element_type=f32)) * ya
    yb = jnp.dot(ob_ref[...].astype(bf16), wb_ref[...], preferred_element_type=f32)
    merged = merged + jax.nn.sigmoid(jnp.dot(xn, wg_ref[:, D_MODEL:], preferred_element_type=f32)) * yb
    h = x + jnp.dot(merged.astype(bf16), wo_ref[...], preferred_element_type=f32)
    h_ref[...] = h
    hnT_ref[...] = _rms_normed(h, gffn_ref[...]).T.astype(bf16)


def out_proj(x, oa, ob, g_mix, w_gate_bf, w_a_bf, w_b_bf, w_o_bf, g_ffn, tb):
    t = x.shape[0]
    tok = lambda w: pl.BlockSpec((tb, w), lambda i: (i, 0))
    full = lambda a: pl.BlockSpec(a.shape, lambda i: (0, 0))
    g_mix, g_ffn = g_mix.reshape(1, D_MODEL), g_ffn.reshape(1, D_MODEL)
    return pl.pallas_call(
        _out_proj_body,
        grid=(t // tb,),
        in_specs=[tok(D_MODEL), tok(WIDTH_A), tok(WIDTH_B), full(g_mix), full(w_gate_bf), full(w_a_bf),
                  full(w_b_bf), full(w_o_bf), full(g_ffn)],
        out_specs=[tok(D_MODEL), pl.BlockSpec((D_MODEL, tb), lambda i: (0, i))],
        out_shape=[jax.ShapeDtypeStruct((t, D_MODEL), jnp.float32),
                   jax.ShapeDtypeStruct((D_MODEL, t), jnp.bfloat16)],
        compiler_params=pltpu.CompilerParams(
            dimension_semantics=("parallel",),
            vmem_limit_bytes=PROJ_VMEM_LIMIT),
        name="out_proj",
    )(x, oa, ob, g_mix, w_gate_bf, w_a_bf, w_b_bf, w_o_bf, g_ffn)


def rotary_tables(pos):
    inv = ROPE_THETA ** (-jnp.arange(0, HEAD_DIM, 2, dtype=jnp.float32) / HEAD_DIM)
    ang = pos.astype(jnp.float32)[:, None] * inv[None, :]
    cos, sin = jnp.cos(ang), jnp.sin(ang)
    reps = LANES // HEAD_DIM
    return jnp.tile(jnp.concatenate([cos, cos], axis=1), (1, reps)), jnp.tile(jnp.concatenate([-sin, sin], axis=1), (1, reps))


def kernel(x_prompt, x_sample, cache_a_k, cache_a_v, cache_b_k, cache_b_v, norm_mix, w_in,
           w_branch_a, w_branch_b, w_out, sink_b, norm_ffn, w_peer_q, peer_sub_keys, peer_u,
           peer_v, norm_final):
    assert DEPTH == 1
    bf16 = jnp.bfloat16
    n, seq, _ = x_prompt.shape
    ns, dec = x_sample.shape[:2]
    assert dec == 1
    w_in_bf = w_in[0].astype(bf16)
    w_qkv, w_gate = w_in_bf[:, :QKV_WIDTH], w_in_bf[:, QKV_WIDTH:]
    proj_weights = (norm_mix[0], w_gate, w_branch_a[0].astype(bf16), w_branch_b[0].astype(bf16),
                    w_out[0].astype(bf16), norm_ffn[0])
    sink = sink_b[0].astype(jnp.float32)

    xp = x_prompt.reshape(n * seq, D_MODEL)
    cos, sin = rotary_tables(jnp.tile(jnp.arange(seq), n))
    qa, ka, va, qb, kb, vb, ka_t, va_t, kb_t, vb_t = in_proj(xp, norm_mix[0], duplicate_kv_columns(w_qkv), cos, sin,
                                                             PROJ_TOKEN_BLOCK, seq)
    per_seq = lambda a: a.reshape(n, seq, a.shape[-1])
    oa, ob = prompt_attention(per_seq(qa), per_seq(ka), per_seq(va), per_seq(qb), per_seq(kb), per_seq(vb), sink)
    hp, hnt_p = out_proj(xp, oa.reshape(n * seq, WIDTH_A), ob.reshape(n * seq, WIDTH_B), *proj_weights,
                         PROJ_TOKEN_BLOCK)
    windows = lambda a, heads: a.reshape(n, heads, HEAD_DIM, a.shape[-1]).transpose(0, 3, 1, 2)[None]
    state_p = (windows(ka_t, HEADS_A), windows(va_t, HEADS_A), windows(kb_t, KV_HEADS_B), windows(vb_t, KV_HEADS_B))

    xs = x_sample.reshape(ns, D_MODEL)
    cos, sin = rotary_tables(jnp.full((ns,), PAST_LEN))
    qa, ka, va, qb, kb, vb = in_proj(xs, norm_mix[0], w_qkv, cos, sin, ns, None)
    oa, ob, state_s = sample_mixers(qa, ka, va, qb, kb, vb, sink, cache_a_k[0], cache_a_v[0],
                                    cache_b_k[0], cache_b_v[0])
    hs, hnt_s = out_proj(xs, oa, ob, *proj_weights, ns)

    y_all = peer_block(jnp.concatenate([hp, hs], axis=0), jnp.concatenate([hnt_p, hnt_s], axis=1),
                       w_peer_q[0], peer_sub_keys[0], peer_u[0], peer_v[0], norm_final)
    y_prompt = y_all[:n * seq].reshape(x_prompt.shape)
    y_sample = y_all[n * seq:].reshape(x_sample.shape)
    return (y_prompt, y_sample, *state_p, *[a[None] for a in state_s])
```

```python
import functools
import jax, jax.numpy as jnp
from jax import lax
import numpy as np
from jax.experimental import pallas as pl
from jax.experimental.pallas import tpu as pltpu

D_MODEL = 1024
BATCH = 8
SEQ = 2048
DEPTH = 1
DEC_BATCH = 128
DEC_SEQ = 1
PAST_LEN = 8192

HEAD_DIM = 64
HEADS_A = 8
DILATED_CONFIGS = ((128, 1), (512, 4), (2048, 16))
WINDOW_A = 2048
HEADS_B = 8
KV_HEADS_B = 2
GROUP_B = HEADS_B // KV_HEADS_B
WINDOW_B = 128
BLOCK = 128
ROPE_THETA = 10000.0
NORM_EPS = 1e-6
NEG_INF = -1e30
SCALE = HEAD_DIM ** -0.5

WIDTH_A = HEADS_A * HEAD_DIM
WIDTH_B = HEADS_B * HEAD_DIM
KV_WIDTH_B = KV_HEADS_B * HEAD_DIM
IN_WIDTHS = (WIDTH_A, WIDTH_A, WIDTH_A, WIDTH_B, KV_WIDTH_B, KV_WIDTH_B, D_MODEL, D_MODEL)
D_IN = sum(IN_WIDTHS)
SPLIT_POINTS = tuple(int(v) for v in np.cumsum(IN_WIDTHS)[:-1])

N_KEYS = 128
N_EXPERTS = N_KEYS * N_KEYS
PEER_HEADS = 8
PEER_TOPK = 16
PEER_HALF = 128
PEER_QUERY_DIM = 2 * PEER_HALF
PEER_CHUNK = 128


LANES = 128
SAMPLE_VMEM_LIMIT = 48 * 1024 * 1024
KEYS_PER_BRANCH = 128


def _bf16_round(x):
    return x.astype(jnp.bfloat16).astype(jnp.float32)


def _decode_softmax(s, s_new, sink):
    m = jnp.maximum(jnp.max(s, axis=-1, keepdims=True), s_new)
    if sink is not None:
        m = jnp.maximum(m, sink)
    e = jnp.exp(s - m)
    e_new = jnp.exp(s_new - m)
    denom = jnp.sum(e, axis=-1, keepdims=True) + e_new
    if sink is not None:
        denom = denom + jnp.exp(sink - m)
    return e / denom, e_new / denom, m + jnp.log(denom)


def _sample_mixer_body(qa_ref, ka_ref, va_ref, qb_ref, kb_ref, vb_ref, sink_ref,
                       cak_ref, cav_ref, cbk_ref, cbv_ref,
                       oa_ref, ob_ref, nak_ref, nav_ref, nbk_ref, nbv_ref, s_scr, o_scr):
    f32, bf16 = jnp.float32, jnp.bfloat16
    nt = (((1,), (1,)), ((), ()))
    win_a = cak_ref.shape[2]
    lane_tiles = win_a // LANES

    def as_column(row):
        return jnp.broadcast_to(row, (LANES, row.shape[1])).T

    def shifted(old, col):
        rolled = pltpu.roll(old, old.shape[1] - 1, axis=1)
        pos = lax.broadcasted_iota(jnp.int32, old.shape, 1)
        return jnp.where(pos == old.shape[1] - 1, jnp.tile(col, (1, old.shape[1] // LANES)), rolled)

    qa, ka, va = qa_ref[0], ka_ref[0], va_ref[0]
    own = (lax.broadcasted_iota(jnp.int32, (HEADS_A, WIDTH_A), 1) // HEAD_DIM
           == lax.broadcasted_iota(jnp.int32, (HEADS_A, WIDTH_A), 0))
    s_new = jnp.sum(jnp.where(own, qa * ka, 0.0), axis=-1, keepdims=True) * SCALE
    q_col = as_column(qa)
    k_col = as_column(ka)
    v_col = as_column(va)
    for h in range(HEADS_A):
        rows = slice(h * HEAD_DIM, (h + 1) * HEAD_DIM)
        qh = q_col[rows, :]

        def logits(t, carry, rows=rows, qh=qh, h=h):
            lanes = pl.ds(pl.multiple_of(t * LANES, LANES), LANES)
            s_scr[pl.ds(h, 1), lanes] = jnp.sum(cak_ref[0, rows, lanes] * qh, axis=0, keepdims=True)
            return carry

        lax.fori_loop(0, lane_tiles, logits, 0)
        nak_ref[0, rows, :] = shifted(cak_ref[0, rows, :], k_col[rows, :])

    s = s_scr[...] * SCALE
    pos = lax.broadcasted_iota(jnp.int32, s.shape, 1)
    ps, p_news, lses = [], [], []
    for window, dil in DILATED_CONFIGS:
        reach = (pos >= win_a - window) & (pos % dil == 0)
        p, p_new, lse = _decode_softmax(jnp.where(reach, s, NEG_INF), s_new, None)
        ps.append(p)
        p_news.append(p_new)
        lses.append(lse)
    top = jnp.maximum(jnp.maximum(lses[0], lses[1]), lses[2])
    ws = [jnp.exp(l - top) for l in lses]
    inv = 1.0 / (ws[0] + ws[1] + ws[2])
    s_scr[...] = (ws[0] * ps[0] + ws[1] * ps[1] + ws[2] * ps[2]) * inv
    p_new = (ws[0] * p_news[0] + ws[1] * p_news[1] + ws[2] * p_news[2]) * inv

    for h in range(HEADS_A):
        rows = slice(h * HEAD_DIM, (h + 1) * HEAD_DIM)

        def weighted(t, acc, rows=rows, h=h):
            lanes = pl.ds(pl.multiple_of(t * LANES, LANES), LANES)
            return acc + cav_ref[0, rows, lanes] * s_scr[pl.ds(h, 1), lanes]

        acc = lax.fori_loop(0, lane_tiles, weighted, jnp.zeros((HEAD_DIM, LANES), f32))
        o_scr[rows, :] = jnp.broadcast_to(jnp.sum(acc, axis=1, keepdims=True), (HEAD_DIM, LANES))
        nav_ref[0, rows, :] = shifted(cav_ref[0, rows, :], v_col[rows, :])
    p_new_lanes = jnp.sum(jnp.where(own, p_new, 0.0), axis=0, keepdims=True)
    oa_ref[0] = o_scr[...].T[0:1, :] + p_new_lanes * va

    qb, kb, vb = qb_ref[0], kb_ref[0], vb_ref[0]
    lane = lax.broadcasted_iota(jnp.int32, (1, KV_WIDTH_B), 1)
    heads_per_chunk = KV_WIDTH_B // HEAD_DIM
    q_rows = []
    for h in range(HEADS_B):
        c = h // heads_per_chunk
        piece = qb[:, c * KV_WIDTH_B:(c + 1) * KV_WIDTH_B]
        if h % heads_per_chunk != h // GROUP_B:
            piece = pltpu.roll(piece, HEAD_DIM, axis=1)
        q_rows.append(jnp.where(lane // HEAD_DIM == h // GROUP_B, piece, 0.0))
    q_rows = jnp.concatenate(q_rows, axis=0)
    s = jnp.dot(q_rows.astype(bf16), cbk_ref[0].astype(bf16), preferred_element_type=f32) * SCALE
    s_new = jnp.sum(_bf16_round(q_rows) * _bf16_round(kb), axis=-1, keepdims=True) * SCALE
    p, p_new, _ = _decode_softmax(s, s_new, sink_ref[...])
    o = (lax.dot_general(p.astype(bf16), cbv_ref[0].astype(bf16), nt, preferred_element_type=f32)
         + _bf16_round(p_new) * _bf16_round(vb))
    for c in range(WIDTH_B // KV_WIDTH_B):
        halves = []
        for slot in range(heads_per_chunk):
            h = c * heads_per_chunk + slot
            r = o[h:h + 1, :]
            if h // GROUP_B != slot:
                r = pltpu.roll(r, HEAD_DIM, axis=1)
            halves.append(r)
        ob_ref[0, :, c * KV_WIDTH_B:(c + 1) * KV_WIDTH_B] = jnp.where(lane < HEAD_DIM, halves[0], halves[1])

    nbk_ref[0] = shifted(cbk_ref[0], as_column(kb))
    nbv_ref[0] = shifted(cbv_ref[0], as_column(vb))


def sample_mixers(qa, ka, va, qb, kb, vb, sink, ck_a, cv_a, ck_b, cv_b):
    n = qa.shape[0]
    assert ck_a.shape[1] == WINDOW_A and ck_b.shape[1] == WINDOW_B
    assert KV_WIDTH_B == 2 * HEAD_DIM == LANES
    row = lambda a, w: a.reshape(n, 1, w)
    win = lambda a, w: a.transpose(0, 2, 3, 1).reshape(n, w, a.shape[1])
    unwin = lambda a, like: a.reshape(n, like.shape[2], like.shape[3], like.shape[1]).transpose(0, 3, 1, 2)
    row_spec = lambda w: pl.BlockSpec((1, 1, w), lambda b: (b, 0, 0))
    win_spec = lambda r, w: pl.BlockSpec((1, w, r), lambda b: (b, 0, 0))
    f32 = jnp.float32
    oa, ob, nak, nav, nbk, nbv = pl.pallas_call(
        _sample_mixer_body,
        grid=(n,),
        in_specs=[row_spec(WIDTH_A), row_spec(WIDTH_A), row_spec(WIDTH_A),
                  row_spec(WIDTH_B), row_spec(KV_WIDTH_B), row_spec(KV_WIDTH_B),
                  pl.BlockSpec((HEADS_B, 1), lambda b: (0, 0)),
                  win_spec(WINDOW_A, WIDTH_A), win_spec(WINDOW_A, WIDTH_A),
                  win_spec(WINDOW_B, KV_WIDTH_B), win_spec(WINDOW_B, KV_WIDTH_B)],
        out_specs=[row_spec(WIDTH_A), row_spec(WIDTH_B),
                   win_spec(WINDOW_A, WIDTH_A), win_spec(WINDOW_A, WIDTH_A),
                   win_spec(WINDOW_B, KV_WIDTH_B), win_spec(WINDOW_B, KV_WIDTH_B)],
        out_shape=[jax.ShapeDtypeStruct((n, 1, WIDTH_A), f32), jax.ShapeDtypeStruct((n, 1, WIDTH_B), f32),
                   jax.ShapeDtypeStruct((n, WIDTH_A, WINDOW_A), f32), jax.ShapeDtypeStruct((n, WIDTH_A, WINDOW_A), f32),
                   jax.ShapeDtypeStruct((n, KV_WIDTH_B, WINDOW_B), f32),
                   jax.ShapeDtypeStruct((n, KV_WIDTH_B, WINDOW_B), f32)],
        scratch_shapes=[pltpu.VMEM((HEADS_A, WINDOW_A), f32), pltpu.VMEM((WIDTH_A, LANES), f32)],
        compiler_params=pltpu.CompilerParams(
            dimension_semantics=("parallel",),
            vmem_limit_bytes=SAMPLE_VMEM_LIMIT),
        name="sample_mixers",
    )(row(qa, WIDTH_A), row(ka, WIDTH_A), row(va, WIDTH_A), row(qb, WIDTH_B), row(kb, KV_WIDTH_B),
      row(vb, KV_WIDTH_B), sink.reshape(HEADS_B, 1),
      win(ck_a, WIDTH_A), win(cv_a, WIDTH_A), win(ck_b, KV_WIDTH_B), win(cv_b, KV_WIDTH_B))
    state = (unwin(nak, ck_a), unwin(nav, cv_a), unwin(nbk, ck_b), unwin(nbv, cv_b))
    return oa.reshape(n, WIDTH_A), ob.reshape(n, WIDTH_B), state


PEER_TOKEN_BLOCK = 512
PEER_EXPERT_BLOCK = 1024
PEER_ROWS = 16
PEER_PIPE_ROWS = 256
PEER_TABLE_PARTS = 4
ROUTE_STREAMS = 2
TOP_ROWS = 24
PEER_VMEM_LIMIT = 48 * 1024 * 1024
INV_SQRT2 = 0.7071067811865476


def _peer_route_body(hnT_ref, wqT_ref, keys_ref, a_ref, e_ref, q_scr, s_scr, top_scr, thr_scr, invz_scr):
    tb = PEER_TOKEN_BLOCK
    lane_tiles = tb // LANES
    q_scr[...] = jnp.dot(wqT_ref[...], hnT_ref[...], preferred_element_type=jnp.float32).astype(jnp.bfloat16)
    for hp in range(2 * PEER_HEADS):
        s_scr[hp] = jnp.dot(keys_ref[hp % 2], q_scr[hp * PEER_HALF:(hp + 1) * PEER_HALF, :],
                            preferred_element_type=jnp.float32)

    def take_max(x, iota, n):
        m = jnp.max(x, axis=0, keepdims=True)
        first = jnp.min(jnp.where(x == m, iota, float(n)), axis=0, keepdims=True)
        return m, jnp.where(iota == first, -jnp.inf, x)

    groups = lane_tiles // ROUTE_STREAMS

    def tile_lanes(u, k):
        return pl.ds(pl.multiple_of(((u % groups) * ROUTE_STREAMS + k) * LANES, LANES), LANES)

    def half_top(u, carry):
        hp = u // groups
        iota = lax.broadcasted_iota(jnp.int32, (N_KEYS, LANES), 0).astype(jnp.float32)
        lanes = [tile_lanes(u, k) for k in range(ROUTE_STREAMS)]
        xs = [s_scr[hp, :, ln] for ln in lanes]
        for ln in lanes:
            top_scr[hp, PEER_TOPK:, ln] = jnp.full((TOP_ROWS - PEER_TOPK, LANES), -jnp.inf, jnp.float32)
        for r in range(PEER_TOPK + 1):
            for k, ln in enumerate(lanes):
                m, xs[k] = take_max(xs[k], iota, N_KEYS)
                top_scr[hp, pl.ds(r, 1), ln] = m
        return carry

    lax.fori_loop(0, 2 * PEER_HEADS * groups, half_top, 0)

    def pair_top(u, carry):
        h = u // groups
        lanes = [tile_lanes(u, k) for k in range(ROUTE_STREAMS)]
        xs = []
        for ln in lanes:
            t1 = top_scr[2 * h, :, ln]
            t2 = top_scr[2 * h + 1, :, ln]
            xs.append(jnp.concatenate([t1[0:1, :] + t2] + [t1[k:k + 1, :] + t2[0:8, :] for k in range(1, 8)]
                                      + [t1[8:, :] + t2[0:1, :]], axis=0))
        n = xs[0].shape[0]
        iota = lax.broadcasted_iota(jnp.int32, (n, LANES), 0).astype(jnp.float32)
        best, v, z = [None] * ROUTE_STREAMS, [None] * ROUTE_STREAMS, [None] * ROUTE_STREAMS
        for r in range(PEER_TOPK):
            for k in range(ROUTE_STREAMS):
                v[k], xs[k] = take_max(xs[k], iota, n)
                if r == 0:
                    best[k], z[k] = v[k], jnp.ones_like(v[k])
                else:
                    z[k] = z[k] + jnp.exp(v[k] - best[k])
        for k, ln in enumerate(lanes):
            nxt, _ = take_max(xs[k], iota, n)
            thr_scr[h, :, ln] = 0.5 * (v[k] + nxt)
            invz_scr[h, :, ln] = 1.0 / z[k]
        return carry

    lax.fori_loop(0, PEER_HEADS * groups, pair_top, 0)

    def emit(u, carry):
        h = u // (N_KEYS // PEER_ROWS)
        rows = pl.ds(pl.multiple_of((u % (N_KEYS // PEER_ROWS)) * PEER_ROWS, PEER_ROWS), PEER_ROWS)
        s1 = s_scr[2 * h, rows, :]
        s2 = s_scr[2 * h + 1, rows, :]
        a_ref[2 * h, rows, :] = thr_scr[h] - s1
        a_ref[2 * h + 1, rows, :] = s2
        e_ref[2 * h, rows, :] = jnp.exp(s1 - top_scr[2 * h, pl.ds(0, 1), :]) * invz_scr[h]
        e_ref[2 * h + 1, rows, :] = jnp.exp(s2 - top_scr[2 * h + 1, pl.ds(0, 1), :])
        return carry

    lax.fori_loop(0, PEER_HEADS * (N_KEYS // PEER_ROWS), emit, 0)


def _peer_route(hnT, wqT, keys_bf):
    t_pad = hnT.shape[1]
    tb = PEER_TOKEN_BLOCK
    hp = 2 * PEER_HEADS
    tok3 = pl.BlockSpec((None, hp, N_KEYS, tb), lambda i: (i, 0, 0, 0))
    return pl.pallas_call(
        _peer_route_body,
        grid=(t_pad // tb,),
        in_specs=[
            pl.BlockSpec((D_MODEL, tb), lambda i: (0, i)),
            pl.BlockSpec((hp * PEER_HALF, D_MODEL), lambda i: (0, 0)),
            pl.BlockSpec((2, N_KEYS, PEER_HALF), lambda i: (0, 0, 0)),
        ],
        out_specs=[tok3, tok3],
        out_shape=[jax.ShapeDtypeStruct((t_pad // tb, hp, N_KEYS, tb), jnp.float32)] * 2,
        scratch_shapes=[
            pltpu.VMEM((hp * PEER_HALF, tb), jnp.bfloat16),
            pltpu.VMEM((hp, N_KEYS, tb), jnp.float32),
            pltpu.VMEM((hp, TOP_ROWS, tb), jnp.float32),
            pltpu.VMEM((PEER_HEADS, 1, tb), jnp.float32),
            pltpu.VMEM((PEER_HEADS, 1, tb), jnp.float32),
        ],
        compiler_params=pltpu.CompilerParams(
            dimension_semantics=("parallel",),
            vmem_limit_bytes=PEER_VMEM_LIMIT),
        name="peer_route",
    )(hnT, wqT, keys_bf)


def _peer_expert_body(hnT_ref, *refs):
    u_refs, vT_refs = refs[:PEER_TABLE_PARTS], refs[PEER_TABLE_PARTS:2 * PEER_TABLE_PARTS]
    a_ref, e_ref, res_ref, gfin_ref, o_ref, h0_scr, h1_scr, a0_scr, a1_scr, acc_scr = refs[2 * PEER_TABLE_PARTS:]
    part = PEER_EXPERT_BLOCK // PEER_TABLE_PARTS
    assert part == PEER_PIPE_ROWS
    j = pl.program_id(1)
    last = pl.num_programs(1) - 1
    f32 = jnp.float32
    slots = ((h0_scr, a0_scr), (h1_scr, a1_scr))

    @pl.when(j == 0)
    def _():
        acc_scr[...] = jnp.zeros_like(acc_scr)
        a1_scr[...] = jnp.zeros_like(a1_scr)
        for s, u_ref in enumerate(u_refs):
            h0_scr[s * part:(s + 1) * part, :] = jnp.dot(u_ref[...], hnT_ref[...], preferred_element_type=f32)

    def steady(h_cur, a_cur, h_prv, a_prv):
        tile = PEER_PIPE_ROWS

        def gating(row0):
            i1 = (j - 1) * (PEER_EXPERT_BLOCK // N_KEYS) + row0 // N_KEYS
            r = row0 % N_KEYS
            gate = jnp.zeros((PEER_ROWS, PEER_TOKEN_BLOCK), f32)
            for h in range(PEER_HEADS):
                need = a_ref[2 * h, pl.ds(i1, 1), :]
                e1row = e_ref[2 * h, pl.ds(i1, 1), :]
                val = e_ref[2 * h + 1, r:r + PEER_ROWS, :] * e1row
                gate = gate + jnp.where(a_ref[2 * h + 1, r:r + PEER_ROWS, :] >= need, val, 0.0)
            x = h_prv[row0:row0 + PEER_ROWS, :]
            act = 0.5 * x * (1.0 + lax.erf(x * INV_SQRT2))
            a_prv[row0:row0 + PEER_ROWS, :] = (act * gate).astype(jnp.bfloat16)

        def pre_activation(span, cols):
            h_cur[span, cols] = jnp.dot(u_refs[span.start // part][...], hnT_ref[:, cols],
                                        preferred_element_type=f32)

        def accumulate(span, out_rows):
            acc_scr[out_rows, :] += jnp.dot(vT_refs[span.start // part][out_rows, :], a_cur[span, :],
                                            preferred_element_type=f32)

        chunks_per_tile = tile // PEER_ROWS
        for s in range(PEER_EXPERT_BLOCK // tile):
            span = slice(s * tile, (s + 1) * tile)
            mxu_work = [functools.partial(pre_activation, span, slice(c * tile, (c + 1) * tile))
                        for c in range(PEER_TOKEN_BLOCK // tile)]
            mxu_work += [functools.partial(accumulate, span, slice(m * tile, (m + 1) * tile))
                         for m in range(D_MODEL // tile)]
            every = chunks_per_tile // len(mxu_work)
            for c in range(chunks_per_tile):
                if c % every == 0 and c // every < len(mxu_work):
                    mxu_work[c // every]()
                gating(s * tile + c * PEER_ROWS)

    for parity in (0, 1):
        pl.when((j > 0) & (j < last) & (j % 2 == parity))(
            functools.partial(steady, *slots[parity], *slots[1 - parity]))

    @pl.when(j == last)
    def _():
        acc = acc_scr[...]
        for s, vT_ref in enumerate(vT_refs):
            acc = acc + jnp.dot(vT_ref[...], a1_scr[s * part:(s + 1) * part, :], preferred_element_type=f32)
        y = res_ref[...] + acc.T
        y = y * lax.rsqrt(jnp.mean(y * y, axis=-1, keepdims=True) + NORM_EPS)
        o_ref[...] = y * gfin_ref[...]


def _peer_experts(hnT, a, e, res, g_final, u_bf, vT_bf):
    t_pad = hnT.shape[1]
    tb, eb = PEER_TOKEN_BLOCK, PEER_EXPERT_BLOCK
    n_blocks = N_EXPERTS // eb
    tok3 = pl.BlockSpec((None, 2 * PEER_HEADS, N_KEYS, tb), lambda i, j: (i, 0, 0, 0))
    parts = PEER_TABLE_PARTS
    u_specs = [pl.BlockSpec((eb // parts, D_MODEL), lambda i, j, q=q: (jnp.minimum(j, n_blocks - 1) * parts + q, 0))
               for q in range(parts)]
    vT_specs = [pl.BlockSpec((None, None, D_MODEL, eb // parts),
                             lambda i, j, q=q: (jnp.clip(j - 2, 0, n_blocks - 1), q, 0, 0)) for q in range(parts)]
    return pl.pallas_call(
        _peer_expert_body,
        grid=(t_pad // tb, n_blocks + 2),
        in_specs=[
            pl.BlockSpec((D_MODEL, tb), lambda i, j: (0, i)),
            *u_specs, *vT_specs,
            tok3, tok3,
            pl.BlockSpec((tb, D_MODEL), lambda i, j: (i, 0)),
            pl.BlockSpec((1, D_MODEL), lambda i, j: (0, 0)),
        ],
        out_specs=pl.BlockSpec((tb, D_MODEL), lambda i, j: (i, 0)),
        out_shape=jax.ShapeDtypeStruct((t_pad, D_MODEL), jnp.float32),
        scratch_shapes=[
            pltpu.VMEM((eb, tb), jnp.float32), pltpu.VMEM((eb, tb), jnp.float32),
            pltpu.VMEM((eb, tb), jnp.bfloat16), pltpu.VMEM((eb, tb), jnp.bfloat16),
            pltpu.VMEM((D_MODEL, tb), jnp.float32),
        ],
        compiler_params=pltpu.CompilerParams(
            dimension_semantics=("parallel", "arbitrary"),
            vmem_limit_bytes=PEER_VMEM_LIMIT),
        name="peer_experts",
    )(hnT, *[u_bf] * parts, *[vT_bf] * parts, a, e, res, g_final.reshape(1, D_MODEL))


def peer_block(h, hnT, w_q, sub_keys, u_tab, v_tab, g_final):
    t = h.shape[0]
    pad = -t % PEER_TOKEN_BLOCK
    hnT = jnp.pad(hnT, ((0, 0), (0, pad)))
    a, e = _peer_route(hnT, w_q.astype(jnp.bfloat16).T, sub_keys.astype(jnp.bfloat16))
    out = _peer_experts(hnT, a, e, jnp.pad(h, ((0, pad), (0, 0))), g_final,
                        u_tab.astype(jnp.bfloat16),
                        v_tab.astype(jnp.bfloat16).reshape(-1, PEER_TABLE_PARTS, PEER_EXPERT_BLOCK // PEER_TABLE_PARTS,
                                                           D_MODEL).transpose(0, 1, 3, 2))
    return out[:t]


ATTN_VMEM_LIMIT = 40 * 1024 * 1024
HEADS_PER_TILE = LANES // HEAD_DIM
ATTN_STREAMS = 4


def _band_units(units, sinks):
    f32, bf16 = jnp.float32, jnp.bfloat16
    nt = (((1,), (1,)), ((), ()))
    rows = HEADS_PER_TILE * BLOCK
    qi = lax.broadcasted_iota(jnp.int32, (rows, 2 * BLOCK), 0) % BLOCK
    kj = lax.broadcasted_iota(jnp.int32, (rows, 2 * BLOCK), 1)
    off = BLOCK + qi - kj
    band = (off >= 0) & (off <= BLOCK)
    own_block = kj >= BLOCK
    ss = []
    for qs, load_k, _, key_lanes, _ in units:
        lhs = jnp.concatenate([jnp.where(key_lanes[i], qs[i], 0.0) for i in range(HEADS_PER_TILE)], axis=0)
        ss.append(lax.dot_general(lhs.astype(bf16), load_k().astype(bf16), nt, preferred_element_type=f32) * SCALE)
    ss = [jnp.where(band & (own_block | jnp.logical_not(unit[4])), s, NEG_INF) for s, unit in zip(ss, units)]
    if sinks is not None:
        assert HEADS_PER_TILE == 2
        sink_slot = kj == (qi + BLOCK + 1) % (2 * BLOCK)
        head0 = lax.broadcasted_iota(jnp.int32, (rows, 2 * BLOCK), 0) < BLOCK
        sink_logit = jnp.where(head0, sinks[0], sinks[1])
        ss = [jnp.where(sink_slot, sink_logit, s) for s in ss]
    ms = [jnp.max(s, axis=-1, keepdims=True) for s in ss]
    es = [jnp.exp(s - m) for s, m in zip(ss, ms)]
    denoms = [jnp.sum(e, axis=-1, keepdims=True) for e in es]
    results = []
    for e, d, m, unit in zip(es, denoms, ms, units):
        p = e / d
        if sinks is not None:
            p = jnp.where(sink_slot, 0.0, p)
        o = jnp.dot(p.astype(bf16), unit[2]().astype(bf16), preferred_element_type=f32)
        lse = m + jnp.log(d)
        results.append([(o[i * BLOCK:(i + 1) * BLOCK, :], lse[i * BLOCK:(i + 1) * BLOCK, :])
                        for i in range(HEADS_PER_TILE)])
    return results


def _prompt_attention_body(qa_ref, ka_ref, va_ref, qb_ref, kb_ref, vb_ref, sink_ref, oa_ref, ob_ref,
                           o_scr, lse_scr):
    f32 = jnp.float32
    seq = qa_ref.shape[1]
    lane = lax.broadcasted_iota(jnp.int32, (1, LANES), 1)
    low = lane < HEAD_DIM
    own = [low, jnp.logical_not(low)]

    def window(ref, prev, cur):
        return lambda: jnp.concatenate([ref[0, prev, :], ref[0, cur, :]], axis=0)

    for c, (reach, dil) in enumerate(DILATED_CONFIGS):
        assert reach // dil == BLOCK
        blocks = seq // (dil * BLOCK)

        def step(it, carry, c=c, dil=dil, blocks=blocks):
            units, curs = [], []
            for k in range(ATTN_STREAMS):
                u = it * ATTN_STREAMS + k
                res, blk = u // blocks, u % blocks
                cur = pl.ds(res + dil * BLOCK * blk, BLOCK, stride=dil)
                prev = pl.ds(res + dil * BLOCK * jnp.maximum(blk - 1, 0), BLOCK, stride=dil)
                q = qa_ref[0, cur, :]
                units.append(([q, q], window(ka_ref, prev, cur), window(va_ref, prev, cur), own, blk == 0))
                curs.append(cur)
            for cur, ((o0, l0), (o1, l1)) in zip(curs, _band_units(units, None)):
                o_scr[c, cur, :] = jnp.where(low, o0, o1)
                lse_scr[c, cur, :] = jnp.where(low, l0, l1)
            return carry

        lax.fori_loop(0, dil * blocks // ATTN_STREAMS, step, 0)

    def merge(t, carry):
        rows = pl.ds(pl.multiple_of(t * BLOCK, BLOCK), BLOCK)
        ls = [lse_scr[c, rows, :] for c in range(len(DILATED_CONFIGS))]
        top = jnp.maximum(jnp.maximum(ls[0], ls[1]), ls[2])
        ws = [jnp.exp(l - top) for l in ls]
        num = ws[0] * o_scr[0, rows, :] + ws[1] * o_scr[1, rows, :] + ws[2] * o_scr[2, rows, :]
        oa_ref[0, rows, :] = num / (ws[0] + ws[1] + ws[2])
        return carry

    lax.fori_loop(0, seq // BLOCK, merge, 0)

    slab = pl.program_id(1)
    sinks = [sink_ref[slab * HEADS_PER_TILE + i] for i in range(HEADS_PER_TILE)]

    def step_b(it, carry):
        units, curs = [], []
        for k in range(ATTN_STREAMS):
            blk = it * ATTN_STREAMS + k
            cur = pl.ds(pl.multiple_of(blk * BLOCK, BLOCK), BLOCK)
            prev = pl.ds(pl.multiple_of(jnp.maximum(blk - 1, 0) * BLOCK, BLOCK), BLOCK)
            q = qb_ref[0, cur, :]
            units.append(([q, q], window(kb_ref, prev, cur), window(vb_ref, prev, cur), own, blk == 0))
            curs.append(cur)
        for cur, ((o0, _), (o1, _)) in zip(curs, _band_units(units, sinks)):
            ob_ref[0, cur, :] = jnp.where(low, o0, o1)
        return carry

    lax.fori_loop(0, seq // BLOCK // ATTN_STREAMS, step_b, 0)


def prompt_attention(qa, ka, va, qb, kb, vb, sink):
    n, seq, _ = qa.shape
    assert kb.shape[-1] == KV_HEADS_B * LANES and seq % (BLOCK * max(d for _, d in DILATED_CONFIGS)) == 0
    slab = pl.BlockSpec((1, seq, LANES), lambda b, p: (b, 0, p))
    whole = pl.BlockSpec((1, seq, LANES), lambda b, p: (b, 0, p * HEADS_PER_TILE // GROUP_B))
    f32 = jnp.float32
    return pl.pallas_call(
        _prompt_attention_body,
        grid=(n, WIDTH_A // LANES),
        in_specs=[slab, slab, slab, slab, whole, whole, pl.BlockSpec(memory_space=pltpu.SMEM)],
        out_specs=[slab, slab],
        out_shape=[jax.ShapeDtypeStruct((n, seq, WIDTH_A), f32), jax.ShapeDtypeStruct((n, seq, WIDTH_B), f32)],
        scratch_shapes=[pltpu.VMEM((len(DILATED_CONFIGS), seq, LANES), f32),
                        pltpu.VMEM((len(DILATED_CONFIGS), seq, LANES), f32)],
        compiler_params=pltpu.CompilerParams(
            dimension_semantics=("parallel", "parallel"),
            vmem_limit_bytes=ATTN_VMEM_LIMIT),
        name="prompt_attention",
    )(qa, ka, va, qb, kb, vb, sink.reshape(HEADS_B))


PROJ_VMEM_LIMIT = 48 * 1024 * 1024
PROJ_TOKEN_BLOCK = 512
QKV_WIDTHS = IN_WIDTHS[:6]
QKV_WIDTH = sum(QKV_WIDTHS)
QKV_ROTATED = (True, True, False, True, True, False)


def _rms_normed(x, g):
    return x * lax.rsqrt(jnp.mean(x * x, axis=-1, keepdims=True) + NORM_EPS) * g


def _rope_slab(x, cos, sin_signed):
    lane = lax.broadcasted_iota(jnp.int32, (1, LANES), 1)
    half = HEAD_DIM // 2
    partner = jnp.where(lane % HEAD_DIM < half, pltpu.roll(x, LANES - half, axis=1), pltpu.roll(x, half, axis=1))
    return x * cos + partner * sin_signed


def _in_proj_body(x_ref, g_ref, w_ref, cos_ref, sin_ref, *out_refs, widths, channel_major):
    xn = _rms_normed(x_ref[...], g_ref[...]).astype(jnp.bfloat16)
    cos, sin = cos_ref[...], sin_ref[...]
    groups = []
    c0 = 0
    for width, rotated in zip(widths, QKV_ROTATED):
        z = jnp.dot(xn, w_ref[:, c0:c0 + width], preferred_element_type=jnp.float32)
        if rotated:
            z = jnp.concatenate([_rope_slab(z[:, c:c + LANES], cos, sin) for c in range(0, width, LANES)], axis=1)
        groups.append(z)
        c0 += width
    for ref, z in zip(out_refs[:6], groups):
        ref[...] = z
    if channel_major:
        kaT_ref, vaT_ref, kbT_ref, vbT_ref = out_refs[6:]
        kaT_ref[0] = groups[1].T
        vaT_ref[0] = groups[2].T
        tb = x_ref.shape[0]
        low = lax.broadcasted_iota(jnp.int32, (1, LANES), 1) < HEAD_DIM
        for ref, z in ((kbT_ref, groups[4]), (vbT_ref, groups[5])):
            tail = z[tb - WINDOW_B:, :]
            ref[0] = jnp.where(low, tail[:, :LANES], tail[:, LANES:]).T


def in_proj(x, g_mix, w_bf, cos, sin, tb, seq):
    t = x.shape[0]
    channel_major = seq is not None
    per_seq = seq // tb if channel_major else None
    widths = QKV_WIDTHS[:4] + ((2 * KV_WIDTH_B,) * 2 if channel_major else QKV_WIDTHS[4:])
    assert w_bf.shape[1] == sum(widths)
    f32 = jnp.float32
    tok = lambda w: pl.BlockSpec((tb, w), lambda i: (i, 0))
    out_specs = [tok(w) for w in widths]
    out_shape = [jax.ShapeDtypeStruct((t, w), f32) for w in widths]
    if channel_major:
        assert seq == WINDOW_A and tb >= WINDOW_B and KV_WIDTH_B == LANES
        n = t // seq
        out_specs += [pl.BlockSpec((1, WIDTH_A, tb), lambda i: (i // per_seq, 0, i % per_seq))] * 2
        out_specs += [pl.BlockSpec((1, KV_WIDTH_B, WINDOW_B), lambda i: (i // per_seq, 0, 0))] * 2
        out_shape += [jax.ShapeDtypeStruct((n, WIDTH_A, seq), f32)] * 2
        out_shape += [jax.ShapeDtypeStruct((n, KV_WIDTH_B, WINDOW_B), f32)] * 2
    return pl.pallas_call(
        functools.partial(_in_proj_body, widths=widths, channel_major=channel_major),
        grid=(t // tb,),
        in_specs=[tok(D_MODEL), pl.BlockSpec((1, D_MODEL), lambda i: (0, 0)),
                  pl.BlockSpec(w_bf.shape, lambda i: (0, 0)), tok(LANES), tok(LANES)],
        out_specs=out_specs,
        out_shape=out_shape,
        compiler_params=pltpu.CompilerParams(
            dimension_semantics=("arbitrary",),
            vmem_limit_bytes=PROJ_VMEM_LIMIT),
        name="in_proj",
    )(x, g_mix.reshape(1, D_MODEL), w_bf, cos, sin)


def duplicate_kv_columns(w_qkv):
    edges = np.cumsum((0,) + QKV_WIDTHS)
    cols = np.arange(KV_WIDTH_B).reshape(KV_HEADS_B, 1, HEAD_DIM)
    cols = np.broadcast_to(cols, (KV_HEADS_B, LANES // HEAD_DIM, HEAD_DIM)).reshape(-1)
    return jnp.concatenate([w_qkv[:, :edges[4]], w_qkv[:, edges[4] + cols], w_qkv[:, edges[5] + cols]], axis=1)


def _out_proj_body(x_ref, oa_ref, ob_ref, gmix_ref, wg_ref, wa_ref, wb_ref, wo_ref, gffn_ref, h_ref, hnT_ref):
    f32, bf16 = jnp.float32, jnp.bfloat16
    x = x_ref[...]
    xn = _rms_normed(x, gmix_ref[...]).astype(bf16)
    ya = jnp.dot(oa_ref[...].astype(bf16), wa_ref[...], preferred_element_type=f32)
    merged = jax.nn.sigmoid(jnp.dot(xn, wg_ref[:, :D_MODEL], preferred_element_type=f32)) * ya
    yb = jnp.dot(ob_ref[...].astype(bf16), wb_ref[...], preferred_element_type=f32)
    merged = merged + jax.nn.sigmoid(jnp.dot(xn, wg_ref[:, D_MODEL:], preferred_element_type=f32)) * yb
    h = x + jnp.dot(merged.astype(bf16), wo_ref[...], preferred_element_type=f32)
    h_ref[...] = h
    hnT_ref[...] = _rms_normed(h, gffn_ref[...]).T.astype(bf16)


def out_proj(x, oa, ob, g_mix, w_gate_bf, w_a_bf, w_b_bf, w_o_bf, g_ffn, tb):
    t = x.shape[0]
    tok = lambda w: pl.BlockSpec((tb, w), lambda i: (i, 0))
    full = lambda a: pl.BlockSpec(a.shape, lambda i: (0, 0))
    g_mix, g_ffn = g_mix.reshape(1, D_MODEL), g_ffn.reshape(1, D_MODEL)
    return pl.pallas_call(
        _out_proj_body,
        grid=(t // tb,),
        in_specs=[tok(D_MODEL), tok(WIDTH_A), tok(WIDTH_B), full(g_mix), full(w_gate_bf), full(w_a_bf),
                  full(w_b_bf), full(w_o_bf), full(g_ffn)],
        out_specs=[tok(D_MODEL), pl.BlockSpec((D_MODEL, tb), lambda i: (0, i))],
        out_shape=[jax.ShapeDtypeStruct((t, D_MODEL), jnp.float32),
                   jax.ShapeDtypeStruct((D_MODEL, t), jnp.bfloat16)],
        compiler_params=pltpu.CompilerParams(
            dimension_semantics=("parallel",),
            vmem_limit_bytes=PROJ_VMEM_LIMIT),
        name="out_proj",
    )(x, oa, ob, g_mix, w_gate_bf, w_a_bf, w_b_bf, w_o_bf, g_ffn)


def rotary_tables(pos):
    inv = ROPE_THETA ** (-jnp.arange(0, HEAD_DIM, 2, dtype=jnp.float32) / HEAD_DIM)
    ang = pos.astype(jnp.float32)[:, None] * inv[None, :]
    cos, sin = jnp.cos(ang), jnp.sin(ang)
    reps = LANES // HEAD_DIM
    return jnp.tile(jnp.concatenate([cos, cos], axis=1), (1, reps)), jnp.tile(jnp.concatenate([-sin, sin], axis=1), (1, reps))


def kernel(x_prompt, x_sample, cache_a_k, cache_a_v, cache_b_k, cache_b_v, norm_mix, w_in,
           w_branch_a, w_branch_b, w_out, sink_b, norm_ffn, w_peer_q, peer_sub_keys, peer_u,
           peer_v, norm_final):
    assert DEPTH == 1
    bf16 = jnp.bfloat16
    n, seq, _ = x_prompt.shape
    ns, dec = x_sample.shape[:2]
    assert dec == 1
    w_in_bf = w_in[0].astype(bf16)
    w_qkv, w_gate = w_in_bf[:, :QKV_WIDTH], w_in_bf[:, QKV_WIDTH:]
    proj_weights = (norm_mix[0], w_gate, w_branch_a[0].astype(bf16), w_branch_b[0].astype(bf16),
                    w_out[0].astype(bf16), norm_ffn[0])
    sink = sink_b[0].astype(jnp.float32)

    xp = x_prompt.reshape(n * seq, D_MODEL)
    cos, sin = rotary_tables(jnp.tile(jnp.arange(seq), n))
    qa, ka, va, qb, kb, vb, ka_t, va_t, kb_t, vb_t = in_proj(xp, norm_mix[0], duplicate_kv_columns(w_qkv), cos, sin,
                                                             PROJ_TOKEN_BLOCK, seq)
    per_seq = lambda a: a.reshape(n, seq, a.shape[-1])
    oa, ob = prompt_attention(per_seq(qa), per_seq(ka), per_seq(va), per_seq(qb), per_seq(kb), per_seq(vb), sink)
    hp, hnt_p = out_proj(xp, oa.reshape(n * seq, WIDTH_A), ob.reshape(n * seq, WIDTH_B), *proj_weights,
                         PROJ_TOKEN_BLOCK)
    windows = lambda a, heads: a.reshape(n, heads, HEAD_DIM, a.shape[-1]).transpose(0, 3, 1, 2)[None]
    state_p = (windows(ka_t, HEADS_A), windows(va_t, HEADS_A), windows(kb_t, KV_HEADS_B), windows(vb_t, KV_HEADS_B))

    xs = x_sample.reshape(ns, D_MODEL)
    cos, sin = rotary_tables(jnp.full((ns,), PAST_LEN))
    qa, ka, va, qb, kb, vb = in_proj(xs, norm_mix[0], w_qkv, cos, sin, ns, None)
    oa, ob, state_s = sample_mixers(qa, ka, va, qb, kb, vb, sink, cache_a_k[0], cache_a_v[0],
                                    cache_b_k[0], cache_b_v[0])
    hs, hnt_s = out_proj(xs, oa, ob, *proj_weights, ns)

    y_all = peer_block(jnp.concatenate([hp, hs], axis=0), jnp.concatenate([hnt_p, hnt_s], axis=1),
                       w_peer_q[0], peer_sub_keys[0], peer_u[0], peer_v[0], norm_final)
    y_prompt = y_all[:n * seq].reshape(x_prompt.shape)
    y_sample = y_all[n * seq:].reshape(x_sample.shape)
    return (y_prompt, y_sample, *state_p, *[a[None] for a in state_s])
```

```python
import functools
import jax, jax.numpy as jnp
from jax import lax
import numpy as np
from jax.experimental import pallas as pl
from jax.experimental.pallas import tpu as pltpu

D_MODEL = 1024
BATCH = 8
SEQ = 2048
DEPTH = 1
DEC_BATCH = 128
DEC_SEQ = 1
PAST_LEN = 8192

HEAD_DIM = 64
HEADS_A = 8
DILATED_CONFIGS = ((128, 1), (512, 4), (2048, 16))
WINDOW_A = 2048
HEADS_B = 8
KV_HEADS_B = 2
GROUP_B = HEADS_B // KV_HEADS_B
WINDOW_B = 128
BLOCK = 128
ROPE_THETA = 10000.0
NORM_EPS = 1e-6
NEG_INF = -1e30
SCALE = HEAD_DIM ** -0.5

WIDTH_A = HEADS_A * HEAD_DIM
WIDTH_B = HEADS_B * HEAD_DIM
KV_WIDTH_B = KV_HEADS_B * HEAD_DIM
IN_WIDTHS = (WIDTH_A, WIDTH_A, WIDTH_A, WIDTH_B, KV_WIDTH_B, KV_WIDTH_B, D_MODEL, D_MODEL)
D_IN = sum(IN_WIDTHS)
SPLIT_POINTS = tuple(int(v) for v in np.cumsum(IN_WIDTHS)[:-1])

N_KEYS = 128
N_EXPERTS = N_KEYS * N_KEYS
PEER_HEADS = 8
PEER_TOPK = 16
PEER_HALF = 128
PEER_QUERY_DIM = 2 * PEER_HALF
PEER_CHUNK = 128


LANES = 128
SAMPLE_VMEM_LIMIT = 48 * 1024 * 1024
SAMPLE_PARTS = 2
KEYS_PER_BRANCH = 128


def _bf16_round(x):
    return x.astype(jnp.bfloat16).astype(jnp.float32)


def _decode_softmax(s, s_new, sink):
    m = jnp.maximum(jnp.max(s, axis=-1, keepdims=True), s_new)
    if sink is not None:
        m = jnp.maximum(m, sink)
    e = jnp.exp(s - m)
    e_new = jnp.exp(s_new - m)
    denom = jnp.sum(e, axis=-1, keepdims=True) + e_new
    if sink is not None:
        denom = denom + jnp.exp(sink - m)
    return e / denom, e_new / denom, m + jnp.log(denom)


def _sample_mixer_body(qa_ref, ka_ref, va_ref, qb_ref, kb_ref, vb_ref, sink_ref, *refs):
    f32, bf16 = jnp.float32, jnp.bfloat16
    nt = (((1,), (1,)), ((), ()))
    sp = SAMPLE_PARTS
    cak_refs, cav_refs = refs[:sp], refs[sp:2 * sp]
    cbk_ref, cbv_ref, oa_ref, ob_ref, nak_ref, nav_ref, nbk_ref, nbv_ref, s_scr, o_scr = refs[2 * sp:]
    heads_per_part = HEADS_A // sp
    win_a = cak_refs[0].shape[2]
    lane_tiles = win_a // LANES

    def as_column(row):
        return jnp.broadcast_to(row, (LANES, row.shape[1])).T

    def shifted(old, col):
        rolled = pltpu.roll(old, old.shape[1] - 1, axis=1)
        pos = lax.broadcasted_iota(jnp.int32, old.shape, 1)
        return jnp.where(pos == old.shape[1] - 1, jnp.tile(col, (1, old.shape[1] // LANES)), rolled)

    qa, ka, va = qa_ref[0], ka_ref[0], va_ref[0]
    own = (lax.broadcasted_iota(jnp.int32, (HEADS_A, WIDTH_A), 1) // HEAD_DIM
           == lax.broadcasted_iota(jnp.int32, (HEADS_A, WIDTH_A), 0))
    s_new = jnp.sum(jnp.where(own, qa * ka, 0.0), axis=-1, keepdims=True) * SCALE
    q_col = as_column(qa)
    k_col = as_column(ka)
    v_col = as_column(va)
    for h in range(HEADS_A):
        rows = slice(h * HEAD_DIM, (h + 1) * HEAD_DIM)
        local = slice((h % heads_per_part) * HEAD_DIM, (h % heads_per_part + 1) * HEAD_DIM)
        cak_ref = cak_refs[h // heads_per_part]
        qh = q_col[rows, :]

        def logits(t, carry, local=local, cak_ref=cak_ref, qh=qh, h=h):
            lanes = pl.ds(pl.multiple_of(t * LANES, LANES), LANES)
            s_scr[pl.ds(h, 1), lanes] = jnp.sum(cak_ref[0, local, lanes] * qh, axis=0, keepdims=True)
            return carry

        lax.fori_loop(0, lane_tiles, logits, 0)
        nak_ref[0, rows, :] = shifted(cak_ref[0, local, :], k_col[rows, :])

    s = s_scr[...] * SCALE
    pos = lax.broadcasted_iota(jnp.int32, s.shape, 1)
    ps, p_news, lses = [], [], []
    for window, dil in DILATED_CONFIGS:
        reach = (pos >= win_a - window) & (pos % dil == 0)
        p, p_new, lse = _decode_softmax(jnp.where(reach, s, NEG_INF), s_new, None)
        ps.append(p)
        p_news.append(p_new)
        lses.append(lse)
    top = jnp.maximum(jnp.maximum(lses[0], lses[1]), lses[2])
    ws = [jnp.exp(l - top) for l in lses]
    inv = 1.0 / (ws[0] + ws[1] + ws[2])
    s_scr[...] = (ws[0] * ps[0] + ws[1] * ps[1] + ws[2] * ps[2]) * inv
    p_new = (ws[0] * p_news[0] + ws[1] * p_news[1] + ws[2] * p_news[2]) * inv

    for h in range(HEADS_A):
        rows = slice(h * HEAD_DIM, (h + 1) * HEAD_DIM)
        local = slice((h % heads_per_part) * HEAD_DIM, (h % heads_per_part + 1) * HEAD_DIM)
        cav_ref = cav_refs[h // heads_per_part]

        def weighted(t, acc, local=local, cav_ref=cav_ref, h=h):
            lanes = pl.ds(pl.multiple_of(t * LANES, LANES), LANES)
            return acc + cav_ref[0, local, lanes] * s_scr[pl.ds(h, 1), lanes]

        acc = lax.fori_loop(0, lane_tiles, weighted, jnp.zeros((HEAD_DIM, LANES), f32))
        o_scr[rows, :] = jnp.broadcast_to(jnp.sum(acc, axis=1, keepdims=True), (HEAD_DIM, LANES))
        nav_ref[0, rows, :] = shifted(cav_ref[0, local, :], v_col[rows, :])
    p_new_lanes = jnp.sum(jnp.where(own, p_new, 0.0), axis=0, keepdims=True)
    oa_ref[0] = o_scr[...].T[0:1, :] + p_new_lanes * va

    qb, kb, vb = qb_ref[0], kb_ref[0], vb_ref[0]
    lane = lax.broadcasted_iota(jnp.int32, (1, KV_WIDTH_B), 1)
    heads_per_chunk = KV_WIDTH_B // HEAD_DIM
    q_rows = []
    for h in range(HEADS_B):
        c = h // heads_per_chunk
        piece = qb[:, c * KV_WIDTH_B:(c + 1) * KV_WIDTH_B]
        if h % heads_per_chunk != h // GROUP_B:
            piece = pltpu.roll(piece, HEAD_DIM, axis=1)
        q_rows.append(jnp.where(lane // HEAD_DIM == h // GROUP_B, piece, 0.0))
    q_rows = jnp.concatenate(q_rows, axis=0)
    s = jnp.dot(q_rows.astype(bf16), cbk_ref[0].astype(bf16), preferred_element_type=f32) * SCALE
    s_new = jnp.sum(_bf16_round(q_rows) * _bf16_round(kb), axis=-1, keepdims=True) * SCALE
    p, p_new, _ = _decode_softmax(s, s_new, sink_ref[...])
    o = (lax.dot_general(p.astype(bf16), cbv_ref[0].astype(bf16), nt, preferred_element_type=f32)
         + _bf16_round(p_new) * _bf16_round(vb))
    for c in range(WIDTH_B // KV_WIDTH_B):
        halves = []
        for slot in range(heads_per_chunk):
            h = c * heads_per_chunk + slot
            r = o[h:h + 1, :]
            if h // GROUP_B != slot:
                r = pltpu.roll(r, HEAD_DIM, axis=1)
            halves.append(r)
        ob_ref[0, :, c * KV_WIDTH_B:(c + 1) * KV_WIDTH_B] = jnp.where(lane < HEAD_DIM, halves[0], halves[1])

    nbk_ref[0] = shifted(cbk_ref[0], as_column(kb))
    nbv_ref[0] = shifted(cbv_ref[0], as_column(vb))


def sample_mixers(qa, ka, va, qb, kb, vb, sink, ck_a, cv_a, ck_b, cv_b):
    n = qa.shape[0]
    assert ck_a.shape[1] == WINDOW_A and ck_b.shape[1] == WINDOW_B
    assert KV_WIDTH_B == 2 * HEAD_DIM == LANES
    row = lambda a, w: a.reshape(n, 1, w)
    win = lambda a, w: a.transpose(0, 2, 3, 1).reshape(n, w, a.shape[1])
    unwin = lambda a, like: a.reshape(n, like.shape[2], like.shape[3], like.shape[1]).transpose(0, 3, 1, 2)
    row_spec = lambda w: pl.BlockSpec((1, 1, w), lambda b: (b, 0, 0))
    win_spec = lambda r, w: pl.BlockSpec((1, w, r), lambda b: (b, 0, 0))
    part_specs = [pl.BlockSpec((1, WIDTH_A // SAMPLE_PARTS, WINDOW_A), lambda b, q=q: (b, q, 0))
                  for q in range(SAMPLE_PARTS)]
    f32 = jnp.float32
    oa, ob, nak, nav, nbk, nbv = pl.pallas_call(
        _sample_mixer_body,
        grid=(n,),
        in_specs=[row_spec(WIDTH_A), row_spec(WIDTH_A), row_spec(WIDTH_A),
                  row_spec(WIDTH_B), row_spec(KV_WIDTH_B), row_spec(KV_WIDTH_B),
                  pl.BlockSpec((HEADS_B, 1), lambda b: (0, 0)),
                  *part_specs, *part_specs,
                  win_spec(WINDOW_B, KV_WIDTH_B), win_spec(WINDOW_B, KV_WIDTH_B)],
        out_specs=[row_spec(WIDTH_A), row_spec(WIDTH_B),
                   win_spec(WINDOW_A, WIDTH_A), win_spec(WINDOW_A, WIDTH_A),
                   win_spec(WINDOW_B, KV_WIDTH_B), win_spec(WINDOW_B, KV_WIDTH_B)],
        out_shape=[jax.ShapeDtypeStruct((n, 1, WIDTH_A), f32), jax.ShapeDtypeStruct((n, 1, WIDTH_B), f32),
                   jax.ShapeDtypeStruct((n, WIDTH_A, WINDOW_A), f32), jax.ShapeDtypeStruct((n, WIDTH_A, WINDOW_A), f32),
                   jax.ShapeDtypeStruct((n, KV_WIDTH_B, WINDOW_B), f32),
                   jax.ShapeDtypeStruct((n, KV_WIDTH_B, WINDOW_B), f32)],
        scratch_shapes=[pltpu.VMEM((HEADS_A, WINDOW_A), f32), pltpu.VMEM((WIDTH_A, LANES), f32)],
        compiler_params=pltpu.CompilerParams(
            dimension_semantics=("parallel",),
            vmem_limit_bytes=SAMPLE_VMEM_LIMIT),
        name="sample_mixers",
    )(row(qa, WIDTH_A), row(ka, WIDTH_A), row(va, WIDTH_A), row(qb, WIDTH_B), row(kb, KV_WIDTH_B),
      row(vb, KV_WIDTH_B), sink.reshape(HEADS_B, 1),
      *[win(ck_a, WIDTH_A)] * SAMPLE_PARTS, *[win(cv_a, WIDTH_A)] * SAMPLE_PARTS,
      win(ck_b, KV_WIDTH_B), win(cv_b, KV_WIDTH_B))
    state = (unwin(nak, ck_a), unwin(nav, cv_a), unwin(nbk, ck_b), unwin(nbv, cv_b))
    return oa.reshape(n, WIDTH_A), ob.reshape(n, WIDTH_B), state


PEER_TOKEN_BLOCK = 512
PEER_EXPERT_BLOCK = 1024
PEER_ROWS = 16
PEER_PIPE_ROWS = 256
PEER_TABLE_PARTS = 4
ROUTE_STREAMS = 2
TOP_ROWS = 24
PEER_VMEM_LIMIT = 48 * 1024 * 1024
INV_SQRT2 = 0.7071067811865476


def _peer_route_body(hnT_ref, wqT_ref, keys_ref, a_ref, e_ref, q_scr, s_scr, top_scr, thr_scr, invz_scr):
    tb = PEER_TOKEN_BLOCK
    lane_tiles = tb // LANES
    q_scr[...] = jnp.dot(wqT_ref[...], hnT_ref[...], preferred_element_type=jnp.float32).astype(jnp.bfloat16)
    for hp in range(2 * PEER_HEADS):
        s_scr[hp] = jnp.dot(keys_ref[hp % 2], q_scr[hp * PEER_HALF:(hp + 1) * PEER_HALF, :],
                            preferred_element_type=jnp.float32)

    def take_max(x, iota, n):
        m = jnp.max(x, axis=0, keepdims=True)
        first = jnp.min(jnp.where(x == m, iota, float(n)), axis=0, keepdims=True)
        return m, jnp.where(iota == first, -jnp.inf, x)

    groups = lane_tiles // ROUTE_STREAMS

    def tile_lanes(u, k):
        return pl.ds(pl.multiple_of(((u % groups) * ROUTE_STREAMS + k) * LANES, LANES), LANES)

    def half_top(u, carry):
        hp = u // groups
        iota = lax.broadcasted_iota(jnp.int32, (N_KEYS, LANES), 0).astype(jnp.float32)
        lanes = [tile_lanes(u, k) for k in range(ROUTE_STREAMS)]
        xs = [s_scr[hp, :, ln] for ln in lanes]
        for ln in lanes:
            top_scr[hp, PEER_TOPK:, ln] = jnp.full((TOP_ROWS - PEER_TOPK, LANES), -jnp.inf, jnp.float32)
        for r in range(PEER_TOPK + 1):
            for k, ln in enumerate(lanes):
                m, xs[k] = take_max(xs[k], iota, N_KEYS)
                top_scr[hp, pl.ds(r, 1), ln] = m
        return carry

    lax.fori_loop(0, 2 * PEER_HEADS * groups, half_top, 0)

    def pair_top(u, carry):
        h = u // groups
        lanes = [tile_lanes(u, k) for k in range(ROUTE_STREAMS)]
        xs = []
        for ln in lanes:
            t1 = top_scr[2 * h, :, ln]
            t2 = top_scr[2 * h + 1, :, ln]
            xs.append(jnp.concatenate([t1[0:1, :] + t2] + [t1[k:k + 1, :] + t2[0:8, :] for k in range(1, 8)]
                                      + [t1[8:, :] + t2[0:1, :]], axis=0))
        n = xs[0].shape[0]
        iota = lax.broadcasted_iota(jnp.int32, (n, LANES), 0).astype(jnp.float32)
        best, v, z = [None] * ROUTE_STREAMS, [None] * ROUTE_STREAMS, [None] * ROUTE_STREAMS
        for r in range(PEER_TOPK):
            for k in range(ROUTE_STREAMS):
                v[k], xs[k] = take_max(xs[k], iota, n)
                if r == 0:
                    best[k], z[k] = v[k], jnp.ones_like(v[k])
                else:
                    z[k] = z[k] + jnp.exp(v[k] - best[k])
        for k, ln in enumerate(lanes):
            nxt, _ = take_max(xs[k], iota, n)
            thr_scr[h, :, ln] = 0.5 * (v[k] + nxt)
            invz_scr[h, :, ln] = 1.0 / z[k]
        return carry

    lax.fori_loop(0, PEER_HEADS * groups, pair_top, 0)

    def emit(u, carry):
        h = u // (N_KEYS // PEER_ROWS)
        rows = pl.ds(pl.multiple_of((u % (N_KEYS // PEER_ROWS)) * PEER_ROWS, PEER_ROWS), PEER_ROWS)
        s1 = s_scr[2 * h, rows, :]
        s2 = s_scr[2 * h + 1, rows, :]
        a_ref[2 * h, rows, :] = thr_scr[h] - s1
        a_ref[2 * h + 1, rows, :] = s2
        e_ref[2 * h, rows, :] = jnp.exp(s1 - top_scr[2 * h, pl.ds(0, 1), :]) * invz_scr[h]
        e_ref[2 * h + 1, rows, :] = jnp.exp(s2 - top_scr[2 * h + 1, pl.ds(0, 1), :])
        return carry

    lax.fori_loop(0, PEER_HEADS * (N_KEYS // PEER_ROWS), emit, 0)


def _peer_route(hnT, wqT, keys_bf):
    t_pad = hnT.shape[1]
    tb = PEER_TOKEN_BLOCK
    hp = 2 * PEER_HEADS
    tok3 = pl.BlockSpec((None, hp, N_KEYS, tb), lambda i: (i, 0, 0, 0))
    return pl.pallas_call(
        _peer_route_body,
        grid=(t_pad // tb,),
        in_specs=[
            pl.BlockSpec((D_MODEL, tb), lambda i: (0, i)),
            pl.BlockSpec((hp * PEER_HALF, D_MODEL), lambda i: (0, 0)),
            pl.BlockSpec((2, N_KEYS, PEER_HALF), lambda i: (0, 0, 0)),
        ],
        out_specs=[tok3, tok3],
        out_shape=[jax.ShapeDtypeStruct((t_pad // tb, hp, N_KEYS, tb), jnp.float32)] * 2,
        scratch_shapes=[
            pltpu.VMEM((hp * PEER_HALF, tb), jnp.bfloat16),
            pltpu.VMEM((hp, N_KEYS, tb), jnp.float32),
            pltpu.VMEM((hp, TOP_ROWS, tb), jnp.float32),
            pltpu.VMEM((PEER_HEADS, 1, tb), jnp.float32),
            pltpu.VMEM((PEER_HEADS, 1, tb), jnp.float32),
        ],
        compiler_params=pltpu.CompilerParams(
            dimension_semantics=("parallel",),
            vmem_limit_bytes=PEER_VMEM_LIMIT),
        name="peer_route",
    )(hnT, wqT, keys_bf)


def _peer_expert_body(hnT_ref, *refs):
    parts = PEER_TABLE_PARTS
    u_refs, vT_refs, a_refs, e_refs = (refs[k * parts:(k + 1) * parts] for k in range(4))
    res_ref, gfin_ref, o_ref, h0_scr, h1_scr, a0_scr, a1_scr, acc_scr = refs[4 * parts:]
    heads_per_part = PEER_HEADS // parts
    part = PEER_EXPERT_BLOCK // PEER_TABLE_PARTS
    assert part == PEER_PIPE_ROWS
    j = pl.program_id(1)
    last = pl.num_programs(1) - 1
    f32 = jnp.float32
    slots = ((h0_scr, a0_scr), (h1_scr, a1_scr))

    @pl.when(j == 0)
    def _():
        acc_scr[...] = jnp.zeros_like(acc_scr)
        a1_scr[...] = jnp.zeros_like(a1_scr)
        for s, u_ref in enumerate(u_refs):
            h0_scr[s * part:(s + 1) * part, :] = jnp.dot(u_ref[...], hnT_ref[...], preferred_element_type=f32)

    def steady(h_cur, a_cur, h_prv, a_prv):
        tile = PEER_PIPE_ROWS

        def gating(row0):
            i1 = (j - 1) * (PEER_EXPERT_BLOCK // N_KEYS) + row0 // N_KEYS
            r = row0 % N_KEYS
            gate = jnp.zeros((PEER_ROWS, PEER_TOKEN_BLOCK), f32)
            for h in range(PEER_HEADS):
                a_ref, e_ref = a_refs[h // heads_per_part], e_refs[h // heads_per_part]
                hp = 2 * (h % heads_per_part)
                need = a_ref[hp, pl.ds(i1, 1), :]
                e1row = e_ref[hp, pl.ds(i1, 1), :]
                val = e_ref[hp + 1, r:r + PEER_ROWS, :] * e1row
                gate = gate + jnp.where(a_ref[hp + 1, r:r + PEER_ROWS, :] >= need, val, 0.0)
            x = h_prv[row0:row0 + PEER_ROWS, :]
            act = 0.5 * x * (1.0 + lax.erf(x * INV_SQRT2))
            a_prv[row0:row0 + PEER_ROWS, :] = (act * gate).astype(jnp.bfloat16)

        def pre_activation(span, cols):
            h_cur[span, cols] = jnp.dot(u_refs[span.start // part][...], hnT_ref[:, cols],
                                        preferred_element_type=f32)

        def accumulate(span, out_rows):
            acc_scr[out_rows, :] += jnp.dot(vT_refs[span.start // part][out_rows, :], a_cur[span, :],
                                            preferred_element_type=f32)

        chunks_per_tile = tile // PEER_ROWS
        for s in range(PEER_EXPERT_BLOCK // tile):
            span = slice(s * tile, (s + 1) * tile)
            mxu_work = [functools.partial(pre_activation, span, slice(c * tile, (c + 1) * tile))
                        for c in range(PEER_TOKEN_BLOCK // tile)]
            mxu_work += [functools.partial(accumulate, span, slice(m * tile, (m + 1) * tile))
                         for m in range(D_MODEL // tile)]
            every = chunks_per_tile // len(mxu_work)
            for c in range(chunks_per_tile):
                if c % every == 0 and c // every < len(mxu_work):
                    mxu_work[c // every]()
                gating(s * tile + c * PEER_ROWS)

    for parity in (0, 1):
        pl.when((j > 0) & (j < last) & (j % 2 == parity))(
            functools.partial(steady, *slots[parity], *slots[1 - parity]))

    @pl.when(j == last)
    def _():
        acc = acc_scr[...]
        for s, vT_ref in enumerate(vT_refs):
            acc = acc + jnp.dot(vT_ref[...], a1_scr[s * part:(s + 1) * part, :], preferred_element_type=f32)
        y = res_ref[...] + acc.T
        y = y * lax.rsqrt(jnp.mean(y * y, axis=-1, keepdims=True) + NORM_EPS)
        o_ref[...] = y * gfin_ref[...]


def _peer_experts(hnT, a, e, res, g_final, u_bf, vT_bf):
    t_pad = hnT.shape[1]
    tb, eb = PEER_TOKEN_BLOCK, PEER_EXPERT_BLOCK
    n_blocks = N_EXPERTS // eb
    parts = PEER_TABLE_PARTS
    tok_specs = [pl.BlockSpec((None, 2 * PEER_HEADS // parts, N_KEYS, tb), lambda i, j, q=q: (i, q, 0, 0))
                 for q in range(parts)]
    u_specs = [pl.BlockSpec((eb // parts, D_MODEL), lambda i, j, q=q: (jnp.minimum(j, n_blocks - 1) * parts + q, 0))
               for q in range(parts)]
    vT_specs = [pl.BlockSpec((None, None, D_MODEL, eb // parts),
                             lambda i, j, q=q: (jnp.clip(j - 2, 0, n_blocks - 1), q, 0, 0)) for q in range(parts)]
    return pl.pallas_call(
        _peer_expert_body,
        grid=(t_pad // tb, n_blocks + 2),
        in_specs=[
            pl.BlockSpec((D_MODEL, tb), lambda i, j: (0, i)),
            *u_specs, *vT_specs, *tok_specs, *tok_specs,
            pl.BlockSpec((tb, D_MODEL), lambda i, j: (i, 0)),
            pl.BlockSpec((1, D_MODEL), lambda i, j: (0, 0)),
        ],
        out_specs=pl.BlockSpec((tb, D_MODEL), lambda i, j: (i, 0)),
        out_shape=jax.ShapeDtypeStruct((t_pad, D_MODEL), jnp.float32),
        scratch_shapes=[
            pltpu.VMEM((eb, tb), jnp.float32), pltpu.VMEM((eb, tb), jnp.float32),
            pltpu.VMEM((eb, tb), jnp.bfloat16), pltpu.VMEM((eb, tb), jnp.bfloat16),
            pltpu.VMEM((D_MODEL, tb), jnp.float32),
        ],
        compiler_params=pltpu.CompilerParams(
            dimension_semantics=("parallel", "arbitrary"),
            vmem_limit_bytes=PEER_VMEM_LIMIT),
        name="peer_experts",
    )(hnT, *[u_bf] * parts, *[vT_bf] * parts, *[a] * parts, *[e] * parts, res, g_final.reshape(1, D_MODEL))


def peer_block(h, hnT, w_q, sub_keys, u_tab, v_tab, g_final):
    t = h.shape[0]
    pad = -t % PEER_TOKEN_BLOCK
    hnT = jnp.pad(hnT, ((0, 0), (0, pad)))
    a, e = _peer_route(hnT, w_q.astype(jnp.bfloat16).T, sub_keys.astype(jnp.bfloat16))
    out = _peer_experts(hnT, a, e, jnp.pad(h, ((0, pad), (0, 0))), g_final,
                        u_tab.astype(jnp.bfloat16),
                        v_tab.astype(jnp.bfloat16).reshape(-1, PEER_TABLE_PARTS, PEER_EXPERT_BLOCK // PEER_TABLE_PARTS,
                                                           D_MODEL).transpose(0, 1, 3, 2))
    return out[:t]


ATTN_VMEM_LIMIT = 40 * 1024 * 1024
HEADS_PER_TILE = LANES // HEAD_DIM
ATTN_STREAMS = 4


def _band_units(units, sinks):
    f32, bf16 = jnp.float32, jnp.bfloat16
    nt = (((1,), (1,)), ((), ()))
    rows = HEADS_PER_TILE * BLOCK
    qi = lax.broadcasted_iota(jnp.int32, (rows, 2 * BLOCK), 0) % BLOCK
    kj = lax.broadcasted_iota(jnp.int32, (rows, 2 * BLOCK), 1)
    off = BLOCK + qi - kj
    band = (off >= 0) & (off <= BLOCK)
    own_block = kj >= BLOCK
    ss = []
    for qs, load_k, _, key_lanes, _ in units:
        lhs = jnp.concatenate([jnp.where(key_lanes[i], qs[i], 0.0) for i in range(HEADS_PER_TILE)], axis=0)
        ss.append(lax.dot_general(lhs.astype(bf16), load_k().astype(bf16), nt, preferred_element_type=f32) * SCALE)
    ss = [jnp.where(band & (own_block | jnp.logical_not(unit[4])), s, NEG_INF) for s, unit in zip(ss, units)]
    if sinks is not None:
        assert HEADS_PER_TILE == 2
        sink_slot = kj == (qi + BLOCK + 1) % (2 * BLOCK)
        head0 = lax.broadcasted_iota(jnp.int32, (rows, 2 * BLOCK), 0) < BLOCK
        sink_logit = jnp.where(head0, sinks[0], sinks[1])
        ss = [jnp.where(sink_slot, sink_logit, s) for s in ss]
    ms = [jnp.max(s, axis=-1, keepdims=True) for s in ss]
    es = [jnp.exp(s - m) for s, m in zip(ss, ms)]
    denoms = [jnp.sum(e, axis=-1, keepdims=True) for e in es]
    results = []
    for e, d, m, unit in zip(es, denoms, ms, units):
        p = e / d
        if sinks is not None:
            p = jnp.where(sink_slot, 0.0, p)
        o = jnp.dot(p.astype(bf16), unit[2]().astype(bf16), preferred_element_type=f32)
        lse = m + jnp.log(d)
        results.append([(o[i * BLOCK:(i + 1) * BLOCK, :], lse[i * BLOCK:(i + 1) * BLOCK, :])
                        for i in range(HEADS_PER_TILE)])
    return results


def _prompt_attention_body(qa_ref, ka_ref, va_ref, qb_ref, kb_ref, vb_ref, sink_ref, oa_ref, ob_ref,
                           o_scr, lse_scr):
    f32 = jnp.float32
    seq = qa_ref.shape[1]
    lane = lax.broadcasted_iota(jnp.int32, (1, LANES), 1)
    low = lane < HEAD_DIM
    own = [low, jnp.logical_not(low)]

    def window(ref, prev, cur):
        return lambda: jnp.concatenate([ref[0, prev, :], ref[0, cur, :]], axis=0)

    for c, (reach, dil) in enumerate(DILATED_CONFIGS):
        assert reach // dil == BLOCK
        blocks = seq // (dil * BLOCK)

        def step(it, carry, c=c, dil=dil, blocks=blocks):
            units, curs = [], []
            for k in range(ATTN_STREAMS):
                u = it * ATTN_STREAMS + k
                res, blk = u // blocks, u % blocks
                cur = pl.ds(res + dil * BLOCK * blk, BLOCK, stride=dil)
                prev = pl.ds(res + dil * BLOCK * jnp.maximum(blk - 1, 0), BLOCK, stride=dil)
                q = qa_ref[0, cur, :]
                units.append(([q, q], window(ka_ref, prev, cur), window(va_ref, prev, cur), own, blk == 0))
                curs.append(cur)
            for cur, ((o0, l0), (o1, l1)) in zip(curs, _band_units(units, None)):
                o_scr[c, cur, :] = jnp.where(low, o0, o1)
                lse_scr[c, cur, :] = jnp.where(low, l0, l1)
            return carry

        lax.fori_loop(0, dil * blocks // ATTN_STREAMS, step, 0)

    def merge(t, carry):
        rows = pl.ds(pl.multiple_of(t * BLOCK, BLOCK), BLOCK)
        ls = [lse_scr[c, rows, :] for c in range(len(DILATED_CONFIGS))]
        top = jnp.maximum(jnp.maximum(ls[0], ls[1]), ls[2])
        ws = [jnp.exp(l - top) for l in ls]
        num = ws[0] * o_scr[0, rows, :] + ws[1] * o_scr[1, rows, :] + ws[2] * o_scr[2, rows, :]
        oa_ref[0, rows, :] = num / (ws[0] + ws[1] + ws[2])
        return carry

    lax.fori_loop(0, seq // BLOCK, merge, 0)

    slab = pl.program_id(1)
    sinks = [sink_ref[slab * HEADS_PER_TILE + i] for i in range(HEADS_PER_TILE)]

    def step_b(it, carry):
        units, curs = [], []
        for k in range(ATTN_STREAMS):
            blk = it * ATTN_STREAMS + k
            cur = pl.ds(pl.multiple_of(blk * BLOCK, BLOCK), BLOCK)
            prev = pl.ds(pl.multiple_of(jnp.maximum(blk - 1, 0) * BLOCK, BLOCK), BLOCK)
            q = qb_ref[0, cur, :]
            units.append(([q, q], window(kb_ref, prev, cur), window(vb_ref, prev, cur), own, blk == 0))
            curs.append(cur)
        for cur, ((o0, _), (o1, _)) in zip(curs, _band_units(units, sinks)):
            ob_ref[0, cur, :] = jnp.where(low, o0, o1)
        return carry

    lax.fori_loop(0, seq // BLOCK // ATTN_STREAMS, step_b, 0)


def prompt_attention(qa, ka, va, qb, kb, vb, sink):
    n, seq, _ = qa.shape
    assert kb.shape[-1] == KV_HEADS_B * LANES and seq % (BLOCK * max(d for _, d in DILATED_CONFIGS)) == 0
    slab = pl.BlockSpec((1, seq, LANES), lambda b, p: (b, 0, p))
    whole = pl.BlockSpec((1, seq, LANES), lambda b, p: (b, 0, p * HEADS_PER_TILE // GROUP_B))
    f32 = jnp.float32
    return pl.pallas_call(
        _prompt_attention_body,
        grid=(n, WIDTH_A // LANES),
        in_specs=[slab, slab, slab, slab, whole, whole, pl.BlockSpec(memory_space=pltpu.SMEM)],
        out_specs=[slab, slab],
        out_shape=[jax.ShapeDtypeStruct((n, seq, WIDTH_A), f32), jax.ShapeDtypeStruct((n, seq, WIDTH_B), f32)],
        scratch_shapes=[pltpu.VMEM((len(DILATED_CONFIGS), seq, LANES), f32),
                        pltpu.VMEM((len(DILATED_CONFIGS), seq, LANES), f32)],
        compiler_params=pltpu.CompilerParams(
            dimension_semantics=("parallel", "parallel"),
            vmem_limit_bytes=ATTN_VMEM_LIMIT),
        name="prompt_attention",
    )(qa, ka, va, qb, kb, vb, sink.reshape(HEADS_B))


PROJ_VMEM_LIMIT = 48 * 1024 * 1024
PROJ_TOKEN_BLOCK = 512
QKV_WIDTHS = IN_WIDTHS[:6]
QKV_WIDTH = sum(QKV_WIDTHS)
QKV_ROTATED = (True, True, False, True, True, False)


def _rms_normed(x, g):
    return x * lax.rsqrt(jnp.mean(x * x, axis=-1, keepdims=True) + NORM_EPS) * g


def _rope_slab(x, cos, sin_signed):
    lane = lax.broadcasted_iota(jnp.int32, (1, LANES), 1)
    half = HEAD_DIM // 2
    partner = jnp.where(lane % HEAD_DIM < half, pltpu.roll(x, LANES - half, axis=1), pltpu.roll(x, half, axis=1))
    return x * cos + partner * sin_signed


def _in_proj_body(x_ref, g_ref, w_ref, cos_ref, sin_ref, *out_refs, widths, channel_major):
    xn = _rms_normed(x_ref[...], g_ref[...]).astype(jnp.bfloat16)
    cos, sin = cos_ref[...], sin_ref[...]
    groups = []
    c0 = 0
    for width, rotated in zip(widths, QKV_ROTATED):
        z = jnp.dot(xn, w_ref[:, c0:c0 + width], preferred_element_type=jnp.float32)
        if rotated:
            z = jnp.concatenate([_rope_slab(z[:, c:c + LANES], cos, sin) for c in range(0, width, LANES)], axis=1)
        groups.append(z)
        c0 += width
    for ref, z in zip(out_refs[:6], groups):
        ref[...] = z
    if channel_major:
        kaT_ref, vaT_ref, kbT_ref, vbT_ref = out_refs[6:]
        kaT_ref[0] = groups[1].T
        vaT_ref[0] = groups[2].T
        tb = x_ref.shape[0]
        low = lax.broadcasted_iota(jnp.int32, (1, LANES), 1) < HEAD_DIM
        for ref, z in ((kbT_ref, groups[4]), (vbT_ref, groups[5])):
            tail = z[tb - WINDOW_B:, :]
            ref[0] = jnp.where(low, tail[:, :LANES], tail[:, LANES:]).T


def in_proj(x, g_mix, w_bf, cos, sin, tb, seq):
    t = x.shape[0]
    channel_major = seq is not None
    per_seq = seq // tb if channel_major else None
    widths = QKV_WIDTHS[:4] + ((2 * KV_WIDTH_B,) * 2 if channel_major else QKV_WIDTHS[4:])
    assert w_bf.shape[1] == sum(widths)
    f32 = jnp.float32
    tok = lambda w: pl.BlockSpec((tb, w), lambda i: (i, 0))
    out_specs = [tok(w) for w in widths]
    out_shape = [jax.ShapeDtypeStruct((t, w), f32) for w in widths]
    if channel_major:
        assert seq == WINDOW_A and tb >= WINDOW_B and KV_WIDTH_B == LANES
        n = t // seq
        out_specs += [pl.BlockSpec((1, WIDTH_A, tb), lambda i: (i // per_seq, 0, i % per_seq))] * 2
        out_specs += [pl.BlockSpec((1, KV_WIDTH_B, WINDOW_B), lambda i: (i // per_seq, 0, 0))] * 2
        out_shape += [jax.ShapeDtypeStruct((n, WIDTH_A, seq), f32)] * 2
        out_shape += [jax.ShapeDtypeStruct((n, KV_WIDTH_B, WINDOW_B), f32)] * 2
    return pl.pallas_call(
        functools.partial(_in_proj_body, widths=widths, channel_major=channel_major),
        grid=(t // tb,),
        in_specs=[tok(D_MODEL), pl.BlockSpec((1, D_MODEL), lambda i: (0, 0)),
                  pl.BlockSpec(w_bf.shape, lambda i: (0, 0)), tok(LANES), tok(LANES)],
        out_specs=out_specs,
        out_shape=out_shape,
        compiler_params=pltpu.CompilerParams(
            dimension_semantics=("arbitrary",),
            vmem_limit_bytes=PROJ_VMEM_LIMIT),
        name="in_proj",
    )(x, g_mix.reshape(1, D_MODEL), w_bf, cos, sin)


def duplicate_kv_columns(w_qkv):
    edges = np.cumsum((0,) + QKV_WIDTHS)
    cols = np.arange(KV_WIDTH_B).reshape(KV_HEADS_B, 1, HEAD_DIM)
    cols = np.broadcast_to(cols, (KV_HEADS_B, LANES // HEAD_DIM, HEAD_DIM)).reshape(-1)
    return jnp.concatenate([w_qkv[:, :edges[4]], w_qkv[:, edges[4] + cols], w_qkv[:, edges[5] + cols]], axis=1)


def _out_proj_body(x_ref, oa_ref, ob_ref, gmix_ref, wg_ref, wa_ref, wb_ref, wo_ref, gffn_ref, h_ref, hnT_ref):
    f32, bf16 = jnp.float32, jnp.bfloat16
    x = x_ref[...]
    xn = _rms_normed(x, gmix_ref[...]).astype(bf16)
    ya = jnp.dot(oa_ref[...].astype(bf16), wa_ref[...], preferred_element_type=f32)
    merged = jax.nn.sigmoid(jnp.dot(xn, wg_ref[:, :D_MODEL], preferred_element_type=f32)) * ya
    yb = jnp.dot(ob_ref[...].astype(bf16), wb_ref[...], preferred_element_type=f32)
    merged = merged + jax.nn.sigmoid(jnp.dot(xn, wg_ref[:, D_MODEL:], preferred_element_type=f32)) * yb
    h = x + jnp.dot(merged.astype(bf16), wo_ref[...], preferred_element_type=f32)
    h_ref[...] = h
    hnT_ref[...] = _rms_normed(h, gffn_ref[...]).T.astype(bf16)


def out_proj(x, oa, ob, g_mix, w_gate_bf, w_a_bf, w_b_bf, w_o_bf, g_ffn, tb):
    t = x.shape[0]
    tok = lambda w: pl.BlockSpec((tb, w), lambda i: (i, 0))
    full = lambda a: pl.BlockSpec(a.shape, lambda i: (0, 0))
    g_mix, g_ffn = g_mix.reshape(1, D_MODEL), g_ffn.reshape(1, D_MODEL)
    return pl.pallas_call(
        _out_proj_body,
        grid=(t // tb,),
        in_specs=[tok(D_MODEL), tok(WIDTH_A), tok(WIDTH_B), full(g_mix), full(w_gate_bf), full(w_a_bf),
                  full(w_b_bf), full(w_o_bf), full(g_ffn)],
        out_specs=[tok(D_MODEL), pl.BlockSpec((D_MODEL, tb), lambda i: (0, i))],
        out_shape=[jax.ShapeDtypeStruct((t, D_MODEL), jnp.float32),
                   jax.ShapeDtypeStruct((D_MODEL, t), jnp.bfloat16)],
        compiler_params=pltpu.CompilerParams(
            dimension_semantics=("parallel",),
            vmem_limit_bytes=PROJ_VMEM_LIMIT),
        name="out_proj",
    )(x, oa, ob, g_mix, w_gate_bf, w_a_bf, w_b_bf, w_o_bf, g_ffn)


def rotary_tables(pos):
    inv = ROPE_THETA ** (-jnp.arange(0, HEAD_DIM, 2, dtype=jnp.float32) / HEAD_DIM)
    ang = pos.astype(jnp.float32)[:, None] * inv[None, :]
    cos, sin = jnp.cos(ang), jnp.sin(ang)
    reps = LANES // HEAD_DIM
    return jnp.tile(jnp.concatenate([cos, cos], axis=1), (1, reps)), jnp.tile(jnp.concatenate([-sin, sin], axis=1), (1, reps))


def kernel(x_prompt, x_sample, cache_a_k, cache_a_v, cache_b_k, cache_b_v, norm_mix, w_in,
           w_branch_a, w_branch_b, w_out, sink_b, norm_ffn, w_peer_q, peer_sub_keys, peer_u,
           peer_v, norm_final):
    assert DEPTH == 1
    bf16 = jnp.bfloat16
    n, seq, _ = x_prompt.shape
    ns, dec = x_sample.shape[:2]
    assert dec == 1
    w_in_bf = w_in[0].astype(bf16)
    w_qkv, w_gate = w_in_bf[:, :QKV_WIDTH], w_in_bf[:, QKV_WIDTH:]
    proj_weights = (norm_mix[0], w_gate, w_branch_a[0].astype(bf16), w_branch_b[0].astype(bf16),
                    w_out[0].astype(bf16), norm_ffn[0])
    sink = sink_b[0].astype(jnp.float32)

    xp = x_prompt.reshape(n * seq, D_MODEL)
    cos, sin = rotary_tables(jnp.tile(jnp.arange(seq), n))
    qa, ka, va, qb, kb, vb, ka_t, va_t, kb_t, vb_t = in_proj(xp, norm_mix[0], duplicate_kv_columns(w_qkv), cos, sin,
                                                             PROJ_TOKEN_BLOCK, seq)
    per_seq = lambda a: a.reshape(n, seq, a.shape[-1])
    oa, ob = prompt_attention(per_seq(qa), per_seq(ka), per_seq(va), per_seq(qb), per_seq(kb), per_seq(vb), sink)
    hp, hnt_p = out_proj(xp, oa.reshape(n * seq, WIDTH_A), ob.reshape(n * seq, WIDTH_B), *proj_weights,
                         PROJ_TOKEN_BLOCK)
    windows = lambda a, heads: a.reshape(n, heads, HEAD_DIM, a.shape[-1]).transpose(0, 3, 1, 2)[None]
    state_p = (windows(ka_t, HEADS_A), windows(va_t, HEADS_A), windows(kb_t, KV_HEADS_B), windows(vb_t, KV_HEADS_B))

    xs = x_sample.reshape(ns, D_MODEL)
    cos, sin = rotary_tables(jnp.full((ns,), PAST_LEN))
    qa, ka, va, qb, kb, vb = in_proj(xs, norm_mix[0], w_qkv, cos, sin, ns, None)
    oa, ob, state_s = sample_mixers(qa, ka, va, qb, kb, vb, sink, cache_a_k[0], cache_a_v[0],
                                    cache_b_k[0], cache_b_v[0])
    hs, hnt_s = out_proj(xs, oa, ob, *proj_weights, ns)

    y_all = peer_block(jnp.concatenate([hp, hs], axis=0), jnp.concatenate([hnt_p, hnt_s], axis=1),
                       w_peer_q[0], peer_sub_keys[0], peer_u[0], peer_v[0], norm_final)
    y_prompt = y_all[:n * seq].reshape(x_prompt.shape)
    y_sample = y_all[n * seq:].reshape(x_sample.shape)
    return (y_prompt, y_sample, *state_p, *[a[None] for a in state_s])
```

```python
import functools
import jax, jax.numpy as jnp
from jax import lax
import numpy as np
from jax.experimental import pallas as pl
from jax.experimental.pallas import tpu as pltpu

D_MODEL = 1024
BATCH = 8
SEQ = 2048
DEPTH = 1
DEC_BATCH = 128
DEC_SEQ = 1
PAST_LEN = 8192

HEAD_DIM = 64
HEADS_A = 8
DILATED_CONFIGS = ((128, 1), (512, 4), (2048, 16))
WINDOW_A = 2048
HEADS_B = 8
KV_HEADS_B = 2
GROUP_B = HEADS_B // KV_HEADS_B
WINDOW_B = 128
BLOCK = 128
ROPE_THETA = 10000.0
NORM_EPS = 1e-6
NEG_INF = -1e30
SCALE = HEAD_DIM ** -0.5

WIDTH_A = HEADS_A * HEAD_DIM
WIDTH_B = HEADS_B * HEAD_DIM
KV_WIDTH_B = KV_HEADS_B * HEAD_DIM
IN_WIDTHS = (WIDTH_A, WIDTH_A, WIDTH_A, WIDTH_B, KV_WIDTH_B, KV_WIDTH_B, D_MODEL, D_MODEL)
D_IN = sum(IN_WIDTHS)
SPLIT_POINTS = tuple(int(v) for v in np.cumsum(IN_WIDTHS)[:-1])

N_KEYS = 128
N_EXPERTS = N_KEYS * N_KEYS
PEER_HEADS = 8
PEER_TOPK = 16
PEER_HALF = 128
PEER_QUERY_DIM = 2 * PEER_HALF
PEER_CHUNK = 128


LANES = 128
SAMPLE_VMEM_LIMIT = 48 * 1024 * 1024
SAMPLE_PARTS = 2
KEYS_PER_BRANCH = 128


def _bf16_round(x):
    return x.astype(jnp.bfloat16).astype(jnp.float32)


def _decode_softmax(s, s_new, sink):
    m = jnp.maximum(jnp.max(s, axis=-1, keepdims=True), s_new)
    if sink is not None:
        m = jnp.maximum(m, sink)
    e = jnp.exp(s - m)
    e_new = jnp.exp(s_new - m)
    denom = jnp.sum(e, axis=-1, keepdims=True) + e_new
    if sink is not None:
        denom = denom + jnp.exp(sink - m)
    return e / denom, e_new / denom, m + jnp.log(denom)


def _sample_mixer_body(qa_ref, ka_ref, va_ref, qb_ref, kb_ref, vb_ref, sink_ref, *refs):
    f32, bf16 = jnp.float32, jnp.bfloat16
    nt = (((1,), (1,)), ((), ()))
    sp = SAMPLE_PARTS
    cak_refs, cav_refs = refs[:sp], refs[sp:2 * sp]
    cbk_ref, cbv_ref, oa_ref, ob_ref, nak_ref, nav_ref, nbk_ref, nbv_ref, s_scr, o_scr = refs[2 * sp:]
    heads_per_part = HEADS_A // sp
    win_a = cak_refs[0].shape[2]
    lane_tiles = win_a // LANES

    def as_column(row):
        return jnp.broadcast_to(row, (LANES, row.shape[1])).T

    not_last_lane = lax.broadcasted_iota(jnp.int32, (1, LANES), 1) < LANES - 1

    def shifted(old, col):
        rolled = pltpu.roll(old, old.shape[1] - 1, axis=1)
        pos = lax.broadcasted_iota(jnp.int32, old.shape, 1)
        return jnp.where(pos == old.shape[1] - 1, jnp.tile(col, (1, old.shape[1] // LANES)), rolled)

    qa, ka, va = qa_ref[0], ka_ref[0], va_ref[0]
    own = (lax.broadcasted_iota(jnp.int32, (HEADS_A, WIDTH_A), 1) // HEAD_DIM
           == lax.broadcasted_iota(jnp.int32, (HEADS_A, WIDTH_A), 0))
    s_new = jnp.sum(jnp.where(own, qa * ka, 0.0), axis=-1, keepdims=True) * SCALE
    q_col = as_column(qa)
    k_col = as_column(ka)
    v_col = as_column(va)
    for h in range(HEADS_A):
        rows = slice(h * HEAD_DIM, (h + 1) * HEAD_DIM)
        local = slice((h % heads_per_part) * HEAD_DIM, (h % heads_per_part + 1) * HEAD_DIM)
        cak_ref = cak_refs[h // heads_per_part]
        qh = q_col[rows, :]

        left = k_col[rows, :]
        for t in reversed(range(lane_tiles)):
            tile = slice(t * LANES, (t + 1) * LANES)
            cur = cak_ref[0, local, tile]
            s_scr[h:h + 1, tile] = jnp.sum(cur * qh, axis=0, keepdims=True)
            cur_left = pltpu.roll(cur, LANES - 1, axis=1)
            nak_ref[0, rows, tile] = jnp.where(not_last_lane, cur_left, left)
            left = cur_left

    s = s_scr[...] * SCALE
    pos = lax.broadcasted_iota(jnp.int32, s.shape, 1)
    ps, p_news, lses = [], [], []
    for window, dil in DILATED_CONFIGS:
        reach = (pos >= win_a - window) & (pos % dil == 0)
        p, p_new, lse = _decode_softmax(jnp.where(reach, s, NEG_INF), s_new, None)
        ps.append(p)
        p_news.append(p_new)
        lses.append(lse)
    top = jnp.maximum(jnp.maximum(lses[0], lses[1]), lses[2])
    ws = [jnp.exp(l - top) for l in lses]
    inv = 1.0 / (ws[0] + ws[1] + ws[2])
    s_scr[...] = (ws[0] * ps[0] + ws[1] * ps[1] + ws[2] * ps[2]) * inv
    p_new = (ws[0] * p_news[0] + ws[1] * p_news[1] + ws[2] * p_news[2]) * inv

    for h in range(HEADS_A):
        rows = slice(h * HEAD_DIM, (h + 1) * HEAD_DIM)
        local = slice((h % heads_per_part) * HEAD_DIM, (h % heads_per_part + 1) * HEAD_DIM)
        cav_ref = cav_refs[h // heads_per_part]

        left = v_col[rows, :]
        acc = jnp.zeros((HEAD_DIM, LANES), f32)
        for t in reversed(range(lane_tiles)):
            tile = slice(t * LANES, (t + 1) * LANES)
            cur = cav_ref[0, local, tile]
            acc = acc + cur * s_scr[h:h + 1, tile]
            cur_left = pltpu.roll(cur, LANES - 1, axis=1)
            nav_ref[0, rows, tile] = jnp.where(not_last_lane, cur_left, left)
            left = cur_left
        o_scr[rows, :] = jnp.broadcast_to(jnp.sum(acc, axis=1, keepdims=True), (HEAD_DIM, LANES))
    p_new_lanes = jnp.sum(jnp.where(own, p_new, 0.0), axis=0, keepdims=True)
    oa_ref[0] = o_scr[...].T[0:1, :] + p_new_lanes * va

    qb, kb, vb = qb_ref[0], kb_ref[0], vb_ref[0]
    lane = lax.broadcasted_iota(jnp.int32, (1, KV_WIDTH_B), 1)
    heads_per_chunk = KV_WIDTH_B // HEAD_DIM
    q_rows = []
    for h in range(HEADS_B):
        c = h // heads_per_chunk
        piece = qb[:, c * KV_WIDTH_B:(c + 1) * KV_WIDTH_B]
        if h % heads_per_chunk != h // GROUP_B:
            piece = pltpu.roll(piece, HEAD_DIM, axis=1)
        q_rows.append(jnp.where(lane // HEAD_DIM == h // GROUP_B, piece, 0.0))
    q_rows = jnp.concatenate(q_rows, axis=0)
    s = jnp.dot(q_rows.astype(bf16), cbk_ref[0].astype(bf16), preferred_element_type=f32) * SCALE
    s_new = jnp.sum(_bf16_round(q_rows) * _bf16_round(kb), axis=-1, keepdims=True) * SCALE
    p, p_new, _ = _decode_softmax(s, s_new, sink_ref[...])
    o = (lax.dot_general(p.astype(bf16), cbv_ref[0].astype(bf16), nt, preferred_element_type=f32)
         + _bf16_round(p_new) * _bf16_round(vb))
    for c in range(WIDTH_B // KV_WIDTH_B):
        halves = []
        for slot in range(heads_per_chunk):
            h = c * heads_per_chunk + slot
            r = o[h:h + 1, :]
            if h // GROUP_B != slot:
                r = pltpu.roll(r, HEAD_DIM, axis=1)
            halves.append(r)
        ob_ref[0, :, c * KV_WIDTH_B:(c + 1) * KV_WIDTH_B] = jnp.where(lane < HEAD_DIM, halves[0], halves[1])

    nbk_ref[0] = shifted(cbk_ref[0], as_column(kb))
    nbv_ref[0] = shifted(cbv_ref[0], as_column(vb))


def sample_mixers(qa, ka, va, qb, kb, vb, sink, ck_a, cv_a, ck_b, cv_b):
    n = qa.shape[0]
    assert ck_a.shape[1] == WINDOW_A and ck_b.shape[1] == WINDOW_B
    assert KV_WIDTH_B == 2 * HEAD_DIM == LANES
    row = lambda a, w: a.reshape(n, 1, w)
    win = lambda a, w: a.transpose(0, 2, 3, 1).reshape(n, w, a.shape[1])
    unwin = lambda a, like: a.reshape(n, like.shape[2], like.shape[3], like.shape[1]).transpose(0, 3, 1, 2)
    row_spec = lambda w: pl.BlockSpec((1, 1, w), lambda b: (b, 0, 0))
    win_spec = lambda r, w: pl.BlockSpec((1, w, r), lambda b: (b, 0, 0))
    part_specs = [pl.BlockSpec((1, WIDTH_A // SAMPLE_PARTS, WINDOW_A), lambda b, q=q: (b, q, 0))
                  for q in range(SAMPLE_PARTS)]
    f32 = jnp.float32
    oa, ob, nak, nav, nbk, nbv = pl.pallas_call(
        _sample_mixer_body,
        grid=(n,),
        in_specs=[row_spec(WIDTH_A), row_spec(WIDTH_A), row_spec(WIDTH_A),
                  row_spec(WIDTH_B), row_spec(KV_WIDTH_B), row_spec(KV_WIDTH_B),
                  pl.BlockSpec((HEADS_B, 1), lambda b: (0, 0)),
                  *part_specs, *part_specs,
                  win_spec(WINDOW_B, KV_WIDTH_B), win_spec(WINDOW_B, KV_WIDTH_B)],
        out_specs=[row_spec(WIDTH_A), row_spec(WIDTH_B),
                   win_spec(WINDOW_A, WIDTH_A), win_spec(WINDOW_A, WIDTH_A),
                   win_spec(WINDOW_B, KV_WIDTH_B), win_spec(WINDOW_B, KV_WIDTH_B)],
        out_shape=[jax.ShapeDtypeStruct((n, 1, WIDTH_A), f32), jax.ShapeDtypeStruct((n, 1, WIDTH_B), f32),
                   jax.ShapeDtypeStruct((n, WIDTH_A, WINDOW_A), f32), jax.ShapeDtypeStruct((n, WIDTH_A, WINDOW_A), f32),
                   jax.ShapeDtypeStruct((n, KV_WIDTH_B, WINDOW_B), f32),
                   jax.ShapeDtypeStruct((n, KV_WIDTH_B, WINDOW_B), f32)],
        scratch_shapes=[pltpu.VMEM((HEADS_A, WINDOW_A), f32), pltpu.VMEM((WIDTH_A, LANES), f32)],
        compiler_params=pltpu.CompilerParams(
            dimension_semantics=("parallel",),
            vmem_limit_bytes=SAMPLE_VMEM_LIMIT),
        name="sample_mixers",
    )(row(qa, WIDTH_A), row(ka, WIDTH_A), row(va, WIDTH_A), row(qb, WIDTH_B), row(kb, KV_WIDTH_B),
      row(vb, KV_WIDTH_B), sink.reshape(HEADS_B, 1),
      *[win(ck_a, WIDTH_A)] * SAMPLE_PARTS, *[win(cv_a, WIDTH_A)] * SAMPLE_PARTS,
      win(ck_b, KV_WIDTH_B), win(cv_b, KV_WIDTH_B))
    state = (unwin(nak, ck_a), unwin(nav, cv_a), unwin(nbk, ck_b), unwin(nbv, cv_b))
    return oa.reshape(n, WIDTH_A), ob.reshape(n, WIDTH_B), state


PEER_TOKEN_BLOCK = 512
PEER_EXPERT_BLOCK = 1024
PEER_ROWS = 16
PEER_PIPE_ROWS = 256
PEER_TABLE_PARTS = 4
ROUTE_STREAMS = 2
TOP_ROWS = 24
PEER_VMEM_LIMIT = 48 * 1024 * 1024
INV_SQRT2 = 0.7071067811865476


def _peer_route_body(hnT_ref, wqT_ref, keys_ref, a_ref, e_ref, q_scr, s_scr, top_scr, thr_scr, invz_scr):
    tb = PEER_TOKEN_BLOCK
    lane_tiles = tb // LANES
    q_scr[...] = jnp.dot(wqT_ref[...], hnT_ref[...], preferred_element_type=jnp.float32).astype(jnp.bfloat16)
    for hp in range(2 * PEER_HEADS):
        s_scr[hp] = jnp.dot(keys_ref[hp % 2], q_scr[hp * PEER_HALF:(hp + 1) * PEER_HALF, :],
                            preferred_element_type=jnp.float32)

    def take_max(x, iota, n):
        m = jnp.max(x, axis=0, keepdims=True)
        first = jnp.min(jnp.where(x == m, iota, float(n)), axis=0, keepdims=True)
        return m, jnp.where(iota == first, -jnp.inf, x)

    groups = lane_tiles // ROUTE_STREAMS

    def tile_lanes(u, k):
        return pl.ds(pl.multiple_of(((u % groups) * ROUTE_STREAMS + k) * LANES, LANES), LANES)

    def half_top(u, carry):
        hp = u // groups
        iota = lax.broadcasted_iota(jnp.int32, (N_KEYS, LANES), 0).astype(jnp.float32)
        lanes = [tile_lanes(u, k) for k in range(ROUTE_STREAMS)]
        xs = [s_scr[hp, :, ln] for ln in lanes]
        for ln in lanes:
            top_scr[hp, PEER_TOPK:, ln] = jnp.full((TOP_ROWS - PEER_TOPK, LANES), -jnp.inf, jnp.float32)
        for r in range(PEER_TOPK + 1):
            for k, ln in enumerate(lanes):
                m, xs[k] = take_max(xs[k], iota, N_KEYS)
                top_scr[hp, pl.ds(r, 1), ln] = m
        return carry

    lax.fori_loop(0, 2 * PEER_HEADS * groups, half_top, 0)

    def pair_top(u, carry):
        h = u // groups
        lanes = [tile_lanes(u, k) for k in range(ROUTE_STREAMS)]
        xs = []
        for ln in lanes:
            t1 = top_scr[2 * h, :, ln]
            t2 = top_scr[2 * h + 1, :, ln]
            xs.append(jnp.concatenate([t1[0:1, :] + t2] + [t1[k:k + 1, :] + t2[0:8, :] for k in range(1, 8)]
                                      + [t1[8:, :] + t2[0:1, :]], axis=0))
        n = xs[0].shape[0]
        iota = lax.broadcasted_iota(jnp.int32, (n, LANES), 0).astype(jnp.float32)
        best, v, z = [None] * ROUTE_STREAMS, [None] * ROUTE_STREAMS, [None] * ROUTE_STREAMS
        for r in range(PEER_TOPK):
            for k in range(ROUTE_STREAMS):
                v[k], xs[k] = take_max(xs[k], iota, n)
                if r == 0:
                    best[k], z[k] = v[k], jnp.ones_like(v[k])
                else:
                    z[k] = z[k] + jnp.exp(v[k] - best[k])
        for k, ln in enumerate(lanes):
            nxt, _ = take_max(xs[k], iota, n)
            thr_scr[h, :, ln] = 0.5 * (v[k] + nxt)
            invz_scr[h, :, ln] = 1.0 / z[k]
        return carry

    lax.fori_loop(0, PEER_HEADS * groups, pair_top, 0)

    def emit(u, carry):
        h = u // (N_KEYS // PEER_ROWS)
        rows = pl.ds(pl.multiple_of((u % (N_KEYS // PEER_ROWS)) * PEER_ROWS, PEER_ROWS), PEER_ROWS)
        s1 = s_scr[2 * h, rows, :]
        s2 = s_scr[2 * h + 1, rows, :]
        a_ref[2 * h, rows, :] = thr_scr[h] - s1
        a_ref[2 * h + 1, rows, :] = s2
        e_ref[2 * h, rows, :] = jnp.exp(s1 - top_scr[2 * h, pl.ds(0, 1), :]) * invz_scr[h]
        e_ref[2 * h + 1, rows, :] = jnp.exp(s2 - top_scr[2 * h + 1, pl.ds(0, 1), :])
        return carry

    lax.fori_loop(0, PEER_HEADS * (N_KEYS // PEER_ROWS), emit, 0)


def _peer_route(hnT, wqT, keys_bf):
    t_pad = hnT.shape[1]
    tb = PEER_TOKEN_BLOCK
    hp = 2 * PEER_HEADS
    tok3 = pl.BlockSpec((None, hp, N_KEYS, tb), lambda i: (i, 0, 0, 0))
    return pl.pallas_call(
        _peer_route_body,
        grid=(t_pad // tb,),
        in_specs=[
            pl.BlockSpec((D_MODEL, tb), lambda i: (0, i)),
            pl.BlockSpec((hp * PEER_HALF, D_MODEL), lambda i: (0, 0)),
            pl.BlockSpec((2, N_KEYS, PEER_HALF), lambda i: (0, 0, 0)),
        ],
        out_specs=[tok3, tok3],
        out_shape=[jax.ShapeDtypeStruct((t_pad // tb, hp, N_KEYS, tb), jnp.float32)] * 2,
        scratch_shapes=[
            pltpu.VMEM((hp * PEER_HALF, tb), jnp.bfloat16),
            pltpu.VMEM((hp, N_KEYS, tb), jnp.float32),
            pltpu.VMEM((hp, TOP_ROWS, tb), jnp.float32),
            pltpu.VMEM((PEER_HEADS, 1, tb), jnp.float32),
            pltpu.VMEM((PEER_HEADS, 1, tb), jnp.float32),
        ],
        compiler_params=pltpu.CompilerParams(
            dimension_semantics=("parallel",),
            vmem_limit_bytes=PEER_VMEM_LIMIT),
        name="peer_route",
    )(hnT, wqT, keys_bf)


def _peer_expert_body(hnT_ref, *refs):
    parts = PEER_TABLE_PARTS
    u_refs, vT_refs, a_refs, e_refs = (refs[k * parts:(k + 1) * parts] for k in range(4))
    res_ref, res_tail_ref, gfin_ref, o_ref, o_tail_ref, h0_scr, h1_scr, a0_scr, a1_scr, acc_scr = refs[4 * parts:]
    heads_per_part = PEER_HEADS // parts
    part = PEER_EXPERT_BLOCK // PEER_TABLE_PARTS
    assert part == PEER_PIPE_ROWS
    j = pl.program_id(1)
    last = pl.num_programs(1) - 1
    f32 = jnp.float32
    slots = ((h0_scr, a0_scr), (h1_scr, a1_scr))

    @pl.when(j == 0)
    def _():
        acc_scr[...] = jnp.zeros_like(acc_scr)
        a1_scr[...] = jnp.zeros_like(a1_scr)
        for s, u_ref in enumerate(u_refs):
            h0_scr[s * part:(s + 1) * part, :] = jnp.dot(u_ref[...], hnT_ref[...], preferred_element_type=f32)

    def steady(h_cur, a_cur, h_prv, a_prv):
        tile = PEER_PIPE_ROWS

        def gating(row0):
            i1 = (j - 1) * (PEER_EXPERT_BLOCK // N_KEYS) + row0 // N_KEYS
            r = row0 % N_KEYS
            gate = jnp.zeros((PEER_ROWS, PEER_TOKEN_BLOCK), f32)
            for h in range(PEER_HEADS):
                a_ref, e_ref = a_refs[h // heads_per_part], e_refs[h // heads_per_part]
                hp = 2 * (h % heads_per_part)
                need = a_ref[hp, pl.ds(i1, 1), :]
                e1row = e_ref[hp, pl.ds(i1, 1), :]
                val = e_ref[hp + 1, r:r + PEER_ROWS, :] * e1row
                gate = gate + jnp.where(a_ref[hp + 1, r:r + PEER_ROWS, :] >= need, val, 0.0)
            x = h_prv[row0:row0 + PEER_ROWS, :]
            act = 0.5 * x * (1.0 + lax.erf(x * INV_SQRT2))
            a_prv[row0:row0 + PEER_ROWS, :] = (act * gate).astype(jnp.bfloat16)

        def pre_activation(span, cols):
            h_cur[span, cols] = jnp.dot(u_refs[span.start // part][...], hnT_ref[:, cols],
                                        preferred_element_type=f32)

        def accumulate(span, out_rows):
            acc_scr[out_rows, :] += jnp.dot(vT_refs[span.start // part][out_rows, :], a_cur[span, :],
                                            preferred_element_type=f32)

        chunks_per_tile = tile // PEER_ROWS
        for s in range(PEER_EXPERT_BLOCK // tile):
            span = slice(s * tile, (s + 1) * tile)
            mxu_work = [functools.partial(pre_activation, span, slice(c * tile, (c + 1) * tile))
                        for c in range(PEER_TOKEN_BLOCK // tile)]
            mxu_work += [functools.partial(accumulate, span, slice(m * tile, (m + 1) * tile))
                         for m in range(D_MODEL // tile)]
            every = chunks_per_tile // len(mxu_work)
            for c in range(chunks_per_tile):
                if c % every == 0 and c // every < len(mxu_work):
                    mxu_work[c // every]()
                gating(s * tile + c * PEER_ROWS)

    for parity in (0, 1):
        pl.when((j > 0) & (j < last) & (j % 2 == parity))(
            functools.partial(steady, *slots[parity], *slots[1 - parity]))

    @pl.when(j == last)
    def _():
        acc = acc_scr[...]
        for s, vT_ref in enumerate(vT_refs):
            acc = acc + jnp.dot(vT_ref[...], a1_scr[s * part:(s + 1) * part, :], preferred_element_type=f32)
        is_tail = pl.program_id(0) == pl.num_programs(0) - 1
        y = jnp.where(is_tail, res_tail_ref[...], res_ref[...]) + acc.T
        y = y * lax.rsqrt(jnp.mean(y * y, axis=-1, keepdims=True) + NORM_EPS) * gfin_ref[...]

        @pl.when(is_tail)
        def _():
            o_tail_ref[...] = y

        @pl.when(jnp.logical_not(is_tail))
        def _():
            o_ref[...] = y


def _peer_experts(hnT, a, e, res, res_tail, g_final, u_bf, vT_bf):
    t_pad = hnT.shape[1]
    tb, eb = PEER_TOKEN_BLOCK, PEER_EXPERT_BLOCK
    n_blocks = N_EXPERTS // eb
    main_blocks = t_pad // tb - 1
    assert res.shape[0] == main_blocks * tb and res_tail.shape[0] == tb
    main_spec = pl.BlockSpec((tb, D_MODEL), lambda i, j: (jnp.minimum(i, main_blocks - 1), 0))
    tail_spec = pl.BlockSpec((tb, D_MODEL), lambda i, j: (0, 0))
    parts = PEER_TABLE_PARTS
    tok_specs = [pl.BlockSpec((None, 2 * PEER_HEADS // parts, N_KEYS, tb), lambda i, j, q=q: (i, q, 0, 0))
                 for q in range(parts)]
    u_specs = [pl.BlockSpec((eb // parts, D_MODEL), lambda i, j, q=q: (jnp.minimum(j, n_blocks - 1) * parts + q, 0))
               for q in range(parts)]
    vT_specs = [pl.BlockSpec((None, None, D_MODEL, eb // parts),
                             lambda i, j, q=q: (jnp.clip(j - 2, 0, n_blocks - 1), q, 0, 0)) for q in range(parts)]
    return pl.pallas_call(
        _peer_expert_body,
        grid=(t_pad // tb, n_blocks + 2),
        in_specs=[
            pl.BlockSpec((D_MODEL, tb), lambda i, j: (0, i)),
            *u_specs, *vT_specs, *tok_specs, *tok_specs,
            main_spec, tail_spec,
            pl.BlockSpec((1, D_MODEL), lambda i, j: (0, 0)),
        ],
        out_specs=[main_spec, tail_spec],
        out_shape=[jax.ShapeDtypeStruct(res.shape, jnp.float32), jax.ShapeDtypeStruct(res_tail.shape, jnp.float32)],
        scratch_shapes=[
            pltpu.VMEM((eb, tb), jnp.float32), pltpu.VMEM((eb, tb), jnp.float32),
            pltpu.VMEM((eb, tb), jnp.bfloat16), pltpu.VMEM((eb, tb), jnp.bfloat16),
            pltpu.VMEM((D_MODEL, tb), jnp.float32),
        ],
        compiler_params=pltpu.CompilerParams(
            dimension_semantics=("arbitrary", "arbitrary"),
            vmem_limit_bytes=PEER_VMEM_LIMIT),
        name="peer_experts",
    )(hnT, *[u_bf] * parts, *[vT_bf] * parts, *[a] * parts, *[e] * parts, res, res_tail,
      g_final.reshape(1, D_MODEL))


def peer_block(h, hnT, h_tail, hnT_tail, w_q, sub_keys, u_tab, v_tab, g_final):
    tb = PEER_TOKEN_BLOCK
    t_tail = h_tail.shape[0]
    assert h.shape[0] % tb == 0 and t_tail <= tb
    hnT_all = jnp.concatenate([hnT, jnp.pad(hnT_tail, ((0, 0), (0, tb - t_tail)))], axis=1)
    a, e = _peer_route(hnT_all, w_q.astype(jnp.bfloat16).T, sub_keys.astype(jnp.bfloat16))
    out, out_tail = _peer_experts(
        hnT_all, a, e, h, jnp.pad(h_tail, ((0, tb - t_tail), (0, 0))), g_final, u_tab.astype(jnp.bfloat16),
        v_tab.astype(jnp.bfloat16).reshape(-1, PEER_TABLE_PARTS, PEER_EXPERT_BLOCK // PEER_TABLE_PARTS,
                                           D_MODEL).transpose(0, 1, 3, 2))
    return out, out_tail[:t_tail]


ATTN_VMEM_LIMIT = 40 * 1024 * 1024
HEADS_PER_TILE = LANES // HEAD_DIM
ATTN_STREAMS = 4


def _band_units(units, sinks):
    f32, bf16 = jnp.float32, jnp.bfloat16
    nt = (((1,), (1,)), ((), ()))
    rows = HEADS_PER_TILE * BLOCK
    qi = lax.broadcasted_iota(jnp.int32, (rows, 2 * BLOCK), 0) % BLOCK
    kj = lax.broadcasted_iota(jnp.int32, (rows, 2 * BLOCK), 1)
    off = BLOCK + qi - kj
    band = (off >= 0) & (off <= BLOCK)
    own_block = kj >= BLOCK
    ss = []
    for qs, load_k, _, key_lanes, _ in units:
        lhs = jnp.concatenate([jnp.where(key_lanes[i], qs[i], 0.0) for i in range(HEADS_PER_TILE)], axis=0)
        ss.append(lax.dot_general(lhs.astype(bf16), load_k().astype(bf16), nt, preferred_element_type=f32) * SCALE)
    ss = [jnp.where(band & (own_block | jnp.logical_not(unit[4])), s, NEG_INF) for s, unit in zip(ss, units)]
    if sinks is not None:
        assert HEADS_PER_TILE == 2
        sink_slot = kj == (qi + BLOCK + 1) % (2 * BLOCK)
        head0 = lax.broadcasted_iota(jnp.int32, (rows, 2 * BLOCK), 0) < BLOCK
        sink_logit = jnp.where(head0, sinks[0], sinks[1])
        ss = [jnp.where(sink_slot, sink_logit, s) for s in ss]
    ms = [jnp.max(s, axis=-1, keepdims=True) for s in ss]
    es = [jnp.exp(s - m) for s, m in zip(ss, ms)]
    denoms = [jnp.sum(e, axis=-1, keepdims=True) for e in es]
    results = []
    for e, d, m, unit in zip(es, denoms, ms, units):
        p = e / d
        if sinks is not None:
            p = jnp.where(sink_slot, 0.0, p)
        o = jnp.dot(p.astype(bf16), unit[2]().astype(bf16), preferred_element_type=f32)
        lse = m + jnp.log(d)
        results.append([(o[i * BLOCK:(i + 1) * BLOCK, :], lse[i * BLOCK:(i + 1) * BLOCK, :])
                        for i in range(HEADS_PER_TILE)])
    return results


def _prompt_attention_body(qa_ref, ka_ref, va_ref, qb_ref, kb_ref, vb_ref, sink_ref, oa_ref, ob_ref,
                           o_scr, lse_scr):
    f32 = jnp.float32
    seq = qa_ref.shape[1]
    lane = lax.broadcasted_iota(jnp.int32, (1, LANES), 1)
    low = lane < HEAD_DIM
    own = [low, jnp.logical_not(low)]

    def window(ref, prev, cur):
        return lambda: jnp.concatenate([ref[0, prev, :], ref[0, cur, :]], axis=0)

    for c, (reach, dil) in enumerate(DILATED_CONFIGS):
        assert reach // dil == BLOCK
        blocks = seq // (dil * BLOCK)

        def step(it, carry, c=c, dil=dil, blocks=blocks):
            units, curs = [], []
            for k in range(ATTN_STREAMS):
                u = it * ATTN_STREAMS + k
                res, blk = u // blocks, u % blocks
                cur = pl.ds(res + dil * BLOCK * blk, BLOCK, stride=dil)
                prev = pl.ds(res + dil * BLOCK * jnp.maximum(blk - 1, 0), BLOCK, stride=dil)
                q = qa_ref[0, cur, :]
                units.append(([q, q], window(ka_ref, prev, cur), window(va_ref, prev, cur), own, blk == 0))
                curs.append(cur)
            for cur, ((o0, l0), (o1, l1)) in zip(curs, _band_units(units, None)):
                o_scr[c, cur, :] = jnp.where(low, o0, o1)
                lse_scr[c, cur, :] = jnp.where(low, l0, l1)
            return carry

        lax.fori_loop(0, dil * blocks // ATTN_STREAMS, step, 0)

    def merge(t, carry):
        rows = pl.ds(pl.multiple_of(t * BLOCK, BLOCK), BLOCK)
        ls = [lse_scr[c, rows, :] for c in range(len(DILATED_CONFIGS))]
        top = jnp.maximum(jnp.maximum(ls[0], ls[1]), ls[2])
        ws = [jnp.exp(l - top) for l in ls]
        num = ws[0] * o_scr[0, rows, :] + ws[1] * o_scr[1, rows, :] + ws[2] * o_scr[2, rows, :]
        oa_ref[0, rows, :] = num / (ws[0] + ws[1] + ws[2])
        return carry

    lax.fori_loop(0, seq // BLOCK, merge, 0)

    slab = pl.program_id(1)
    sinks = [sink_ref[slab * HEADS_PER_TILE + i] for i in range(HEADS_PER_TILE)]

    def step_b(it, carry):
        units, curs = [], []
        for k in range(ATTN_STREAMS):
            blk = it * ATTN_STREAMS + k
            cur = pl.ds(pl.multiple_of(blk * BLOCK, BLOCK), BLOCK)
            prev = pl.ds(pl.multiple_of(jnp.maximum(blk - 1, 0) * BLOCK, BLOCK), BLOCK)
            q = qb_ref[0, cur, :]
            units.append(([q, q], window(kb_ref, prev, cur), window(vb_ref, prev, cur), own, blk == 0))
            curs.append(cur)
        for cur, ((o0, _), (o1, _)) in zip(curs, _band_units(units, sinks)):
            ob_ref[0, cur, :] = jnp.where(low, o0, o1)
        return carry

    lax.fori_loop(0, seq // BLOCK // ATTN_STREAMS, step_b, 0)


def prompt_attention(qa, ka, va, qb, kb, vb, sink):
    n, seq, _ = qa.shape
    assert kb.shape[-1] == KV_HEADS_B * LANES and seq % (BLOCK * max(d for _, d in DILATED_CONFIGS)) == 0
    slab = pl.BlockSpec((1, seq, LANES), lambda b, p: (b, 0, p))
    whole = pl.BlockSpec((1, seq, LANES), lambda b, p: (b, 0, p * HEADS_PER_TILE // GROUP_B))
    f32 = jnp.float32
    return pl.pallas_call(
        _prompt_attention_body,
        grid=(n, WIDTH_A // LANES),
        in_specs=[slab, slab, slab, slab, whole, whole, pl.BlockSpec(memory_space=pltpu.SMEM)],
        out_specs=[slab, slab],
        out_shape=[jax.ShapeDtypeStruct((n, seq, WIDTH_A), f32), jax.ShapeDtypeStruct((n, seq, WIDTH_B), f32)],
        scratch_shapes=[pltpu.VMEM((len(DILATED_CONFIGS), seq, LANES), f32),
                        pltpu.VMEM((len(DILATED_CONFIGS), seq, LANES), f32)],
        compiler_params=pltpu.CompilerParams(
            dimension_semantics=("parallel", "parallel"),
            vmem_limit_bytes=ATTN_VMEM_LIMIT),
        name="prompt_attention",
    )(qa, ka, va, qb, kb, vb, sink.reshape(HEADS_B))


PROJ_VMEM_LIMIT = 48 * 1024 * 1024
PROJ_TOKEN_BLOCK = 512
QKV_WIDTHS = IN_WIDTHS[:6]
QKV_WIDTH = sum(QKV_WIDTHS)
QKV_ROTATED = (True, True, False, True, True, False)


def _rms_normed(x, g):
    return x * lax.rsqrt(jnp.mean(x * x, axis=-1, keepdims=True) + NORM_EPS) * g


def _rope_slab(x, cos, sin_signed):
    lane = lax.broadcasted_iota(jnp.int32, (1, LANES), 1)
    half = HEAD_DIM // 2
    partner = jnp.where(lane % HEAD_DIM < half, pltpu.roll(x, LANES - half, axis=1), pltpu.roll(x, half, axis=1))
    return x * cos + partner * sin_signed


def _in_proj_body(x_ref, g_ref, w_ref, cos_ref, sin_ref, *out_refs, widths, channel_major):
    xn = _rms_normed(x_ref[...], g_ref[...]).astype(jnp.bfloat16)
    cos, sin = cos_ref[...], sin_ref[...]
    groups = []
    c0 = 0
    for width, rotated in zip(widths, QKV_ROTATED):
        z = jnp.dot(xn, w_ref[:, c0:c0 + width], preferred_element_type=jnp.float32)
        if rotated:
            z = jnp.concatenate([_rope_slab(z[:, c:c + LANES], cos, sin) for c in range(0, width, LANES)], axis=1)
        groups.append(z)
        c0 += width
    for ref, z in zip(out_refs[:6], groups):
        ref[...] = z
    if channel_major:
        kaT_ref, vaT_ref, kbT_ref, vbT_ref = out_refs[6:]
        kaT_ref[0] = groups[1].T
        vaT_ref[0] = groups[2].T
        tb = x_ref.shape[0]
        low = lax.broadcasted_iota(jnp.int32, (1, LANES), 1) < HEAD_DIM
        for ref, z in ((kbT_ref, groups[4]), (vbT_ref, groups[5])):
            tail = z[tb - WINDOW_B:, :]
            ref[0] = jnp.where(low, tail[:, :LANES], tail[:, LANES:]).T


def in_proj(x, g_mix, w_bf, cos, sin, tb, seq):
    t = x.shape[0]
    channel_major = seq is not None
    per_seq = seq // tb if channel_major else None
    widths = QKV_WIDTHS[:4] + ((2 * KV_WIDTH_B,) * 2 if channel_major else QKV_WIDTHS[4:])
    assert w_bf.shape[1] == sum(widths)
    f32 = jnp.float32
    tok = lambda w: pl.BlockSpec((tb, w), lambda i: (i, 0))
    out_specs = [tok(w) for w in widths]
    out_shape = [jax.ShapeDtypeStruct((t, w), f32) for w in widths]
    if channel_major:
        assert seq == WINDOW_A and tb >= WINDOW_B and KV_WIDTH_B == LANES
        n = t // seq
        out_specs += [pl.BlockSpec((1, WIDTH_A, tb), lambda i: (i // per_seq, 0, i % per_seq))] * 2
        out_specs += [pl.BlockSpec((1, KV_WIDTH_B, WINDOW_B), lambda i: (i // per_seq, 0, 0))] * 2
        out_shape += [jax.ShapeDtypeStruct((n, WIDTH_A, seq), f32)] * 2
        out_shape += [jax.ShapeDtypeStruct((n, KV_WIDTH_B, WINDOW_B), f32)] * 2
    return pl.pallas_call(
        functools.partial(_in_proj_body, widths=widths, channel_major=channel_major),
        grid=(t // tb,),
        in_specs=[tok(D_MODEL), pl.BlockSpec((1, D_MODEL), lambda i: (0, 0)),
                  pl.BlockSpec(w_bf.shape, lambda i: (0, 0)), tok(LANES), tok(LANES)],
        out_specs=out_specs,
        out_shape=out_shape,
        compiler_params=pltpu.CompilerParams(
            dimension_semantics=("arbitrary",),
            vmem_limit_bytes=PROJ_VMEM_LIMIT),
        name="in_proj",
    )(x, g_mix.reshape(1, D_MODEL), w_bf, cos, sin)


def duplicate_kv_columns(w_qkv):
    edges = np.cumsum((0,) + QKV_WIDTHS)
    cols = np.arange(KV_WIDTH_B).reshape(KV_HEADS_B, 1, HEAD_DIM)
    cols = np.broadcast_to(cols, (KV_HEADS_B, LANES // HEAD_DIM, HEAD_DIM)).reshape(-1)
    return jnp.concatenate([w_qkv[:, :edges[4]], w_qkv[:, edges[4] + cols], w_qkv[:, edges[5] + cols]], axis=1)


def _out_proj_body(x_ref, oa_ref, ob_ref, gmix_ref, wg_ref, wa_ref, wb_ref, wo_ref, gffn_ref, h_ref, hnT_ref):
    f32, bf16 = jnp.float32, jnp.bfloat16
    x = x_ref[...]
    xn = _rms_normed(x, gmix_ref[...]).astype(bf16)
    ya = jnp.dot(oa_ref[...].astype(bf16), wa_ref[...], preferred_element_type=f32)
    merged = jax.nn.sigmoid(jnp.dot(xn, wg_ref[:, :D_MODEL], preferred_element_type=f32)) * ya
    yb = jnp.dot(ob_ref[...].astype(bf16), wb_ref[...], preferred_element_type=f32)
    merged = merged + jax.nn.sigmoid(jnp.dot(xn, wg_ref[:, D_MODEL:], preferred_element_type=f32)) * yb
    h = x + jnp.dot(merged.astype(bf16), wo_ref[...], preferred_element_type=f32)
    h_ref[...] = h
    hnT_ref[...] = _rms_normed(h, gffn_ref[...]).T.astype(bf16)


def out_proj(x, oa, ob, g_mix, w_gate_bf, w_a_bf, w_b_bf, w_o_bf, g_ffn, tb):
    t = x.shape[0]
    tok = lambda w: pl.BlockSpec((tb, w), lambda i: (i, 0))
    full = lambda a: pl.BlockSpec(a.shape, lambda i: (0, 0))
    g_mix, g_ffn = g_mix.reshape(1, D_MODEL), g_ffn.reshape(1, D_MODEL)
    return pl.pallas_call(
        _out_proj_body,
        grid=(t // tb,),
        in_specs=[tok(D_MODEL), tok(WIDTH_A), tok(WIDTH_B), full(g_mix), full(w_gate_bf), full(w_a_bf),
                  full(w_b_bf), full(w_o_bf), full(g_ffn)],
        out_specs=[tok(D_MODEL), pl.BlockSpec((D_MODEL, tb), lambda i: (0, i))],
        out_shape=[jax.ShapeDtypeStruct((t, D_MODEL), jnp.float32),
                   jax.ShapeDtypeStruct((D_MODEL, t), jnp.bfloat16)],
        compiler_params=pltpu.CompilerParams(
            dimension_semantics=("parallel",),
            vmem_limit_bytes=PROJ_VMEM_LIMIT),
        name="out_proj",
    )(x, oa, ob, g_mix, w_gate_bf, w_a_bf, w_b_bf, w_o_bf, g_ffn)


def rotary_tables(pos):
    inv = ROPE_THETA ** (-jnp.arange(0, HEAD_DIM, 2, dtype=jnp.float32) / HEAD_DIM)
    ang = pos.astype(jnp.float32)[:, None] * inv[None, :]
    cos, sin = jnp.cos(ang), jnp.sin(ang)
    reps = LANES // HEAD_DIM
    return jnp.tile(jnp.concatenate([cos, cos], axis=1), (1, reps)), jnp.tile(jnp.concatenate([-sin, sin], axis=1), (1, reps))


def kernel(x_prompt, x_sample, cache_a_k, cache_a_v, cache_b_k, cache_b_v, norm_mix, w_in,
           w_branch_a, w_branch_b, w_out, sink_b, norm_ffn, w_peer_q, peer_sub_keys, peer_u,
           peer_v, norm_final):
    assert DEPTH == 1
    bf16 = jnp.bfloat16
    n, seq, _ = x_prompt.shape
    ns, dec = x_sample.shape[:2]
    assert dec == 1
    w_in_bf = w_in[0].astype(bf16)
    w_qkv, w_gate = w_in_bf[:, :QKV_WIDTH], w_in_bf[:, QKV_WIDTH:]
    proj_weights = (norm_mix[0], w_gate, w_branch_a[0].astype(bf16), w_branch_b[0].astype(bf16),
                    w_out[0].astype(bf16), norm_ffn[0])
    sink = sink_b[0].astype(jnp.float32)

    xp = x_prompt.reshape(n * seq, D_MODEL)
    cos, sin = rotary_tables(jnp.tile(jnp.arange(seq), n))
    qa, ka, va, qb, kb, vb, ka_t, va_t, kb_t, vb_t = in_proj(xp, norm_mix[0], duplicate_kv_columns(w_qkv), cos, sin,
                                                             PROJ_TOKEN_BLOCK, seq)
    per_seq = lambda a: a.reshape(n, seq, a.shape[-1])
    oa, ob = prompt_attention(per_seq(qa), per_seq(ka), per_seq(va), per_seq(qb), per_seq(kb), per_seq(vb), sink)
    hp, hnt_p = out_proj(xp, oa.reshape(n * seq, WIDTH_A), ob.reshape(n * seq, WIDTH_B), *proj_weights,
                         PROJ_TOKEN_BLOCK)
    windows = lambda a, heads: a.reshape(n, heads, HEAD_DIM, a.shape[-1]).transpose(0, 3, 1, 2)[None]
    state_p = (windows(ka_t, HEADS_A), windows(va_t, HEADS_A), windows(kb_t, KV_HEADS_B), windows(vb_t, KV_HEADS_B))

    xs = x_sample.reshape(ns, D_MODEL)
    cos, sin = rotary_tables(jnp.full((ns,), PAST_LEN))
    qa, ka, va, qb, kb, vb = in_proj(xs, norm_mix[0], w_qkv, cos, sin, ns, None)
    oa, ob, state_s = sample_mixers(qa, ka, va, qb, kb, vb, sink, cache_a_k[0], cache_a_v[0],
                                    cache_b_k[0], cache_b_v[0])
    hs, hnt_s = out_proj(xs, oa, ob, *proj_weights, ns)

    y_prompt, y_sample = peer_block(hp, hnt_p, hs, hnt_s, w_peer_q[0], peer_sub_keys[0], peer_u[0], peer_v[0],
                                    norm_final)
    return (y_prompt.reshape(x_prompt.shape), y_sample.reshape(x_sample.shape), *state_p,
            *[a[None] for a in state_s])
```

```python
import functools
import jax, jax.numpy as jnp
from jax import lax
import numpy as np
from jax.experimental import pallas as pl
from jax.experimental.pallas import tpu as pltpu

D_MODEL = 1024
DEPTH = 1
PAST_LEN = 8192

HEAD_DIM = 64
HEADS_A = 8
DILATED_CONFIGS = ((128, 1), (512, 4), (2048, 16))
WINDOW_A = 2048
HEADS_B = 8
KV_HEADS_B = 2
GROUP_B = HEADS_B // KV_HEADS_B
WINDOW_B = 128
BLOCK = 128
ROPE_THETA = 10000.0
NORM_EPS = 1e-6
NEG_INF = -1e30
SCALE = HEAD_DIM ** -0.5

WIDTH_A = HEADS_A * HEAD_DIM
WIDTH_B = HEADS_B * HEAD_DIM
KV_WIDTH_B = KV_HEADS_B * HEAD_DIM
IN_WIDTHS = (WIDTH_A, WIDTH_A, WIDTH_A, WIDTH_B, KV_WIDTH_B, KV_WIDTH_B, D_MODEL, D_MODEL)

N_KEYS = 128
N_EXPERTS = N_KEYS * N_KEYS
PEER_HEADS = 8
PEER_TOPK = 16
PEER_HALF = 128


LANES = 128
SAMPLE_VMEM_LIMIT = 48 * 1024 * 1024
SAMPLE_PARTS = 2


def _bf16_round(x):
    return x.astype(jnp.bfloat16).astype(jnp.float32)


def _decode_softmax(s, s_new, sink):
    m = jnp.maximum(jnp.max(s, axis=-1, keepdims=True), s_new)
    if sink is not None:
        m = jnp.maximum(m, sink)
    e = jnp.exp(s - m)
    e_new = jnp.exp(s_new - m)
    denom = jnp.sum(e, axis=-1, keepdims=True) + e_new
    if sink is not None:
        denom = denom + jnp.exp(sink - m)
    return e / denom, e_new / denom, m + jnp.log(denom)


def _sample_mixer_body(qa_ref, ka_ref, va_ref, qb_ref, kb_ref, vb_ref, sink_ref, *refs):
    f32, bf16 = jnp.float32, jnp.bfloat16
    nt = (((1,), (1,)), ((), ()))
    sp = SAMPLE_PARTS
    cak_refs, cav_refs = refs[:sp], refs[sp:2 * sp]
    cbk_ref, cbv_ref, oa_ref, ob_ref, nak_ref, nav_ref, nbk_ref, nbv_ref, s_scr, o_scr = refs[2 * sp:]
    heads_per_part = HEADS_A // sp
    win_a = cak_refs[0].shape[2]
    lane_tiles = win_a // LANES

    def as_column(row):
        return jnp.broadcast_to(row, (LANES, row.shape[1])).T

    not_last_lane = lax.broadcasted_iota(jnp.int32, (1, LANES), 1) < LANES - 1

    def shifted(old, col):
        rolled = pltpu.roll(old, old.shape[1] - 1, axis=1)
        pos = lax.broadcasted_iota(jnp.int32, old.shape, 1)
        return jnp.where(pos == old.shape[1] - 1, jnp.tile(col, (1, old.shape[1] // LANES)), rolled)

    qa, ka, va = qa_ref[0], ka_ref[0], va_ref[0]
    own = (lax.broadcasted_iota(jnp.int32, (HEADS_A, WIDTH_A), 1) // HEAD_DIM
           == lax.broadcasted_iota(jnp.int32, (HEADS_A, WIDTH_A), 0))
    s_new = jnp.sum(jnp.where(own, qa * ka, 0.0), axis=-1, keepdims=True) * SCALE
    q_col = as_column(qa)
    k_col = as_column(ka)
    v_col = as_column(va)
    for h in range(HEADS_A):
        rows = slice(h * HEAD_DIM, (h + 1) * HEAD_DIM)
        local = slice((h % heads_per_part) * HEAD_DIM, (h % heads_per_part + 1) * HEAD_DIM)
        cak_ref = cak_refs[h // heads_per_part]
        qh = q_col[rows, :]

        left = k_col[rows, :]
        for t in reversed(range(lane_tiles)):
            tile = slice(t * LANES, (t + 1) * LANES)
            cur = cak_ref[0, local, tile]
            s_scr[h:h + 1, tile] = jnp.sum(cur * qh, axis=0, keepdims=True)
            cur_left = pltpu.roll(cur, LANES - 1, axis=1)
            nak_ref[0, rows, tile] = jnp.where(not_last_lane, cur_left, left)
            left = cur_left

    s = s_scr[...] * SCALE
    pos = lax.broadcasted_iota(jnp.int32, s.shape, 1)
    ps, p_news, lses = [], [], []
    for window, dil in DILATED_CONFIGS:
        reach = (pos >= win_a - window) & (pos % dil == 0)
        p, p_new, lse = _decode_softmax(jnp.where(reach, s, NEG_INF), s_new, None)
        ps.append(p)
        p_news.append(p_new)
        lses.append(lse)
    top = jnp.maximum(jnp.maximum(lses[0], lses[1]), lses[2])
    ws = [jnp.exp(l - top) for l in lses]
    inv = 1.0 / (ws[0] + ws[1] + ws[2])
    s_scr[...] = (ws[0] * ps[0] + ws[1] * ps[1] + ws[2] * ps[2]) * inv
    p_new = (ws[0] * p_news[0] + ws[1] * p_news[1] + ws[2] * p_news[2]) * inv

    for h in range(HEADS_A):
        rows = slice(h * HEAD_DIM, (h + 1) * HEAD_DIM)
        local = slice((h % heads_per_part) * HEAD_DIM, (h % heads_per_part + 1) * HEAD_DIM)
        cav_ref = cav_refs[h // heads_per_part]

        left = v_col[rows, :]
        acc = jnp.zeros((HEAD_DIM, LANES), f32)
        for t in reversed(range(lane_tiles)):
            tile = slice(t * LANES, (t + 1) * LANES)
            cur = cav_ref[0, local, tile]
            acc = acc + cur * s_scr[h:h + 1, tile]
            cur_left = pltpu.roll(cur, LANES - 1, axis=1)
            nav_ref[0, rows, tile] = jnp.where(not_last_lane, cur_left, left)
            left = cur_left
        o_scr[rows, :] = jnp.broadcast_to(jnp.sum(acc, axis=1, keepdims=True), (HEAD_DIM, LANES))
    p_new_lanes = jnp.sum(jnp.where(own, p_new, 0.0), axis=0, keepdims=True)
    oa_ref[0] = o_scr[...].T[0:1, :] + p_new_lanes * va

    qb, kb, vb = qb_ref[0], kb_ref[0], vb_ref[0]
    lane = lax.broadcasted_iota(jnp.int32, (1, KV_WIDTH_B), 1)
    heads_per_chunk = KV_WIDTH_B // HEAD_DIM
    q_rows = []
    for h in range(HEADS_B):
        c = h // heads_per_chunk
        piece = qb[:, c * KV_WIDTH_B:(c + 1) * KV_WIDTH_B]
        if h % heads_per_chunk != h // GROUP_B:
            piece = pltpu.roll(piece, HEAD_DIM, axis=1)
        q_rows.append(jnp.where(lane // HEAD_DIM == h // GROUP_B, piece, 0.0))
    q_rows = jnp.concatenate(q_rows, axis=0)
    s = jnp.dot(q_rows.astype(bf16), cbk_ref[0].astype(bf16), preferred_element_type=f32) * SCALE
    s_new = jnp.sum(_bf16_round(q_rows) * _bf16_round(kb), axis=-1, keepdims=True) * SCALE
    p, p_new, _ = _decode_softmax(s, s_new, sink_ref[...])
    o = (lax.dot_general(p.astype(bf16), cbv_ref[0].astype(bf16), nt, preferred_element_type=f32)
         + _bf16_round(p_new) * _bf16_round(vb))
    for c in range(WIDTH_B // KV_WIDTH_B):
        halves = []
        for slot in range(heads_per_chunk):
            h = c * heads_per_chunk + slot
            r = o[h:h + 1, :]
            if h // GROUP_B != slot:
                r = pltpu.roll(r, HEAD_DIM, axis=1)
            halves.append(r)
        ob_ref[0, :, c * KV_WIDTH_B:(c + 1) * KV_WIDTH_B] = jnp.where(lane < HEAD_DIM, halves[0], halves[1])

    nbk_ref[0] = shifted(cbk_ref[0], as_column(kb))
    nbv_ref[0] = shifted(cbv_ref[0], as_column(vb))


def sample_mixers(qa, ka, va, qb, kb, vb, sink, ck_a, cv_a, ck_b, cv_b):
    n = qa.shape[0]
    assert ck_a.shape[1] == WINDOW_A and ck_b.shape[1] == WINDOW_B
    assert KV_WIDTH_B == 2 * HEAD_DIM == LANES
    row = lambda a, w: a.reshape(n, 1, w)
    win = lambda a, w: a.transpose(0, 2, 3, 1).reshape(n, w, a.shape[1])
    unwin = lambda a, like: a.reshape(n, like.shape[2], like.shape[3], like.shape[1]).transpose(0, 3, 1, 2)
    row_spec = lambda w: pl.BlockSpec((1, 1, w), lambda b: (b, 0, 0))
    win_spec = lambda r, w: pl.BlockSpec((1, w, r), lambda b: (b, 0, 0))
    part_specs = [pl.BlockSpec((1, WIDTH_A // SAMPLE_PARTS, WINDOW_A), lambda b, q=q: (b, q, 0))
                  for q in range(SAMPLE_PARTS)]
    f32 = jnp.float32
    oa, ob, nak, nav, nbk, nbv = pl.pallas_call(
        _sample_mixer_body,
        grid=(n,),
        in_specs=[row_spec(WIDTH_A), row_spec(WIDTH_A), row_spec(WIDTH_A),
                  row_spec(WIDTH_B), row_spec(KV_WIDTH_B), row_spec(KV_WIDTH_B),
                  pl.BlockSpec((HEADS_B, 1), lambda b: (0, 0)),
                  *part_specs, *part_specs,
                  win_spec(WINDOW_B, KV_WIDTH_B), win_spec(WINDOW_B, KV_WIDTH_B)],
        out_specs=[row_spec(WIDTH_A), row_spec(WIDTH_B),
                   win_spec(WINDOW_A, WIDTH_A), win_spec(WINDOW_A, WIDTH_A),
                   win_spec(WINDOW_B, KV_WIDTH_B), win_spec(WINDOW_B, KV_WIDTH_B)],
        out_shape=[jax.ShapeDtypeStruct((n, 1, WIDTH_A), f32), jax.ShapeDtypeStruct((n, 1, WIDTH_B), f32),
                   jax.ShapeDtypeStruct((n, WIDTH_A, WINDOW_A), f32), jax.ShapeDtypeStruct((n, WIDTH_A, WINDOW_A), f32),
                   jax.ShapeDtypeStruct((n, KV_WIDTH_B, WINDOW_B), f32),
                   jax.ShapeDtypeStruct((n, KV_WIDTH_B, WINDOW_B), f32)],
        scratch_shapes=[pltpu.VMEM((HEADS_A, WINDOW_A), f32), pltpu.VMEM((WIDTH_A, LANES), f32)],
        compiler_params=pltpu.CompilerParams(
            dimension_semantics=("parallel",),
            vmem_limit_bytes=SAMPLE_VMEM_LIMIT),
        name="sample_mixers",
    )(row(qa, WIDTH_A), row(ka, WIDTH_A), row(va, WIDTH_A), row(qb, WIDTH_B), row(kb, KV_WIDTH_B),
      row(vb, KV_WIDTH_B), sink.reshape(HEADS_B, 1),
      *[win(ck_a, WIDTH_A)] * SAMPLE_PARTS, *[win(cv_a, WIDTH_A)] * SAMPLE_PARTS,
      win(ck_b, KV_WIDTH_B), win(cv_b, KV_WIDTH_B))
    state = (unwin(nak, ck_a), unwin(nav, cv_a), unwin(nbk, ck_b), unwin(nbv, cv_b))
    return oa.reshape(n, WIDTH_A), ob.reshape(n, WIDTH_B), state


PEER_TOKEN_BLOCK = 512
PEER_EXPERT_BLOCK = 1024
PEER_ROWS = 16
PEER_PIPE_ROWS = 256
PEER_TABLE_PARTS = 4
ROUTE_STREAMS = 4
TOP_ROWS = 24
PEER_VMEM_LIMIT = 48 * 1024 * 1024
INV_SQRT2 = 0.7071067811865476


def _peer_route_body(hnT_ref, wqT_ref, keys_ref, a_ref, e_ref, q_scr, s_scr, top_scr, thr_scr, invz_scr):
    tb = PEER_TOKEN_BLOCK
    lane_tiles = tb // LANES
    q_scr[...] = jnp.dot(wqT_ref[...], hnT_ref[...], preferred_element_type=jnp.float32).astype(jnp.bfloat16)
    for hp in range(2 * PEER_HEADS):
        s_scr[hp] = jnp.dot(keys_ref[hp % 2], q_scr[hp * PEER_HALF:(hp + 1) * PEER_HALF, :],
                            preferred_element_type=jnp.float32)

    def take_max(x, iota, n):
        m = jnp.max(x, axis=0, keepdims=True)
        first = jnp.min(jnp.where(x == m, iota, float(n)), axis=0, keepdims=True)
        return m, jnp.where(iota == first, -jnp.inf, x)

    groups = lane_tiles // ROUTE_STREAMS

    def tile_lanes(u, k):
        return pl.ds(pl.multiple_of(((u % groups) * ROUTE_STREAMS + k) * LANES, LANES), LANES)

    def half_top(u, carry):
        hp = u // groups
        iota = lax.broadcasted_iota(jnp.int32, (N_KEYS, LANES), 0).astype(jnp.float32)
        lanes = [tile_lanes(u, k) for k in range(ROUTE_STREAMS)]
        xs = [s_scr[hp, :, ln] for ln in lanes]
        for ln in lanes:
            top_scr[hp, PEER_TOPK:, ln] = jnp.full((TOP_ROWS - PEER_TOPK, LANES), -jnp.inf, jnp.float32)
        for r in range(PEER_TOPK + 1):
            for k, ln in enumerate(lanes):
                m, xs[k] = take_max(xs[k], iota, N_KEYS)
                top_scr[hp, pl.ds(r, 1), ln] = m
        return carry

    lax.fori_loop(0, 2 * PEER_HEADS * groups, half_top, 0)

    def pair_top(u, carry):
        h = u // groups
        lanes = [tile_lanes(u, k) for k in range(ROUTE_STREAMS)]
        xs = []
        for ln in lanes:
            t1 = top_scr[2 * h, :, ln]
            t2 = top_scr[2 * h + 1, :, ln]
            xs.append(jnp.concatenate([t1[0:1, :] + t2] + [t1[k:k + 1, :] + t2[0:8, :] for k in range(1, 8)]
                                      + [t1[8:, :] + t2[0:1, :]], axis=0))
        n = xs[0].shape[0]
        iota = lax.broadcasted_iota(jnp.int32, (n, LANES), 0).astype(jnp.float32)
        best, v, z = [None] * ROUTE_STREAMS, [None] * ROUTE_STREAMS, [None] * ROUTE_STREAMS
        for r in range(PEER_TOPK):
            for k in range(ROUTE_STREAMS):
                v[k], xs[k] = take_max(xs[k], iota, n)
                if r == 0:
                    best[k], z[k] = v[k], jnp.ones_like(v[k])
                else:
                    z[k] = z[k] + jnp.exp(v[k] - best[k])
        for k, ln in enumerate(lanes):
            nxt, _ = take_max(xs[k], iota, n)
            thr_scr[h, :, ln] = 0.5 * (v[k] + nxt)
            invz_scr[h, :, ln] = 1.0 / z[k]
        return carry

    lax.fori_loop(0, PEER_HEADS * groups, pair_top, 0)

    def emit(u, carry):
        h = u // (N_KEYS // PEER_ROWS)
        rows = pl.ds(pl.multiple_of((u % (N_KEYS // PEER_ROWS)) * PEER_ROWS, PEER_ROWS), PEER_ROWS)
        s1 = s_scr[2 * h, rows, :]
        s2 = s_scr[2 * h + 1, rows, :]
        a_ref[2 * h, rows, :] = thr_scr[h] - s1
        a_ref[2 * h + 1, rows, :] = s2
        e_ref[2 * h, rows, :] = jnp.exp(s1 - top_scr[2 * h, pl.ds(0, 1), :]) * invz_scr[h]
        e_ref[2 * h + 1, rows, :] = jnp.exp(s2 - top_scr[2 * h + 1, pl.ds(0, 1), :])
        return carry

    lax.fori_loop(0, PEER_HEADS * (N_KEYS // PEER_ROWS), emit, 0)


def _peer_route(hnT, wqT, keys_bf):
    t_pad = hnT.shape[1]
    tb = PEER_TOKEN_BLOCK
    hp = 2 * PEER_HEADS
    tok3 = pl.BlockSpec((None, hp, N_KEYS, tb), lambda i: (i, 0, 0, 0))
    return pl.pallas_call(
        _peer_route_body,
        grid=(t_pad // tb,),
        in_specs=[
            pl.BlockSpec((D_MODEL, tb), lambda i: (0, i)),
            pl.BlockSpec((hp * PEER_HALF, D_MODEL), lambda i: (0, 0)),
            pl.BlockSpec((2, N_KEYS, PEER_HALF), lambda i: (0, 0, 0)),
        ],
        out_specs=[tok3, tok3],
        out_shape=[jax.ShapeDtypeStruct((t_pad // tb, hp, N_KEYS, tb), jnp.float32)] * 2,
        scratch_shapes=[
            pltpu.VMEM((hp * PEER_HALF, tb), jnp.bfloat16),
            pltpu.VMEM((hp, N_KEYS, tb), jnp.float32),
            pltpu.VMEM((hp, TOP_ROWS, tb), jnp.float32),
            pltpu.VMEM((PEER_HEADS, 1, tb), jnp.float32),
            pltpu.VMEM((PEER_HEADS, 1, tb), jnp.float32),
        ],
        compiler_params=pltpu.CompilerParams(
            dimension_semantics=("parallel",),
            vmem_limit_bytes=PEER_VMEM_LIMIT),
        name="peer_route",
    )(hnT, wqT, keys_bf)


def _peer_expert_body(hnT_ref, *refs):
    parts = PEER_TABLE_PARTS
    u_refs, vT_refs, a_refs, e_refs = (refs[k * parts:(k + 1) * parts] for k in range(4))
    res_ref, res_tail_ref, gfin_ref, o_ref, o_tail_ref, h0_scr, h1_scr, a0_scr, a1_scr, acc_scr = refs[4 * parts:]
    heads_per_part = PEER_HEADS // parts
    part = PEER_EXPERT_BLOCK // PEER_TABLE_PARTS
    assert part == PEER_PIPE_ROWS
    j = pl.program_id(1)
    last = pl.num_programs(1) - 1
    f32 = jnp.float32
    slots = ((h0_scr, a0_scr), (h1_scr, a1_scr))

    @pl.when(j == 0)
    def _():
        acc_scr[...] = jnp.zeros_like(acc_scr)
        a1_scr[...] = jnp.zeros_like(a1_scr)
        for s, u_ref in enumerate(u_refs):
            h0_scr[s * part:(s + 1) * part, :] = jnp.dot(u_ref[...], hnT_ref[...], preferred_element_type=f32)

    def steady(h_cur, a_cur, h_prv, a_prv):
        tile = PEER_PIPE_ROWS

        def gating(row0):
            i1 = (j - 1) * (PEER_EXPERT_BLOCK // N_KEYS) + row0 // N_KEYS
            r = row0 % N_KEYS
            gate = jnp.zeros((PEER_ROWS, PEER_TOKEN_BLOCK), f32)
            for h in range(PEER_HEADS):
                a_ref, e_ref = a_refs[h // heads_per_part], e_refs[h // heads_per_part]
                hp = 2 * (h % heads_per_part)
                need = a_ref[hp, pl.ds(i1, 1), :]
                e1row = e_ref[hp, pl.ds(i1, 1), :]
                val = e_ref[hp + 1, r:r + PEER_ROWS, :] * e1row
                gate = gate + jnp.where(a_ref[hp + 1, r:r + PEER_ROWS, :] >= need, val, 0.0)
            x = h_prv[row0:row0 + PEER_ROWS, :]
            act = 0.5 * x * (1.0 + lax.erf(x * INV_SQRT2))
            a_prv[row0:row0 + PEER_ROWS, :] = (act * gate).astype(jnp.bfloat16)

        def pre_activation(span, cols):
            h_cur[span, cols] = jnp.dot(u_refs[span.start // part][...], hnT_ref[:, cols],
                                        preferred_element_type=f32)

        def accumulate(span, out_rows):
            acc_scr[out_rows, :] += jnp.dot(vT_refs[span.start // part][out_rows, :], a_cur[span, :],
                                            preferred_element_type=f32)

        chunks_per_tile = tile // PEER_ROWS
        for s in range(PEER_EXPERT_BLOCK // tile):
            span = slice(s * tile, (s + 1) * tile)
            mxu_work = [functools.partial(pre_activation, span, slice(c * tile, (c + 1) * tile))
                        for c in range(PEER_TOKEN_BLOCK // tile)]
            mxu_work += [functools.partial(accumulate, span, slice(m * tile, (m + 1) * tile))
                         for m in range(D_MODEL // tile)]
            every = chunks_per_tile // len(mxu_work)
            for c in range(chunks_per_tile):
                if c % every == 0 and c // every < len(mxu_work):
                    mxu_work[c // every]()
                gating(s * tile + c * PEER_ROWS)

    for parity in (0, 1):
        pl.when((j > 0) & (j < last) & (j % 2 == parity))(
            functools.partial(steady, *slots[parity], *slots[1 - parity]))

    @pl.when(j == last)
    def _():
        acc = acc_scr[...]
        for s, vT_ref in enumerate(vT_refs):
            acc = acc + jnp.dot(vT_ref[...], a1_scr[s * part:(s + 1) * part, :], preferred_element_type=f32)
        is_tail = pl.program_id(0) == pl.num_programs(0) - 1
        y = jnp.where(is_tail, res_tail_ref[...], res_ref[...]) + acc.T
        y = y * lax.rsqrt(jnp.mean(y * y, axis=-1, keepdims=True) + NORM_EPS) * gfin_ref[...]

        @pl.when(is_tail)
        def _():
            o_tail_ref[...] = y

        @pl.when(jnp.logical_not(is_tail))
        def _():
            o_ref[...] = y


def _peer_experts(hnT, a, e, res, res_tail, g_final, u_bf, vT_bf):
    t_pad = hnT.shape[1]
    tb, eb = PEER_TOKEN_BLOCK, PEER_EXPERT_BLOCK
    n_blocks = N_EXPERTS // eb
    main_blocks = t_pad // tb - 1
    assert res.shape[0] == main_blocks * tb and res_tail.shape[0] == tb
    main_spec = pl.BlockSpec((tb, D_MODEL), lambda i, j: (jnp.minimum(i, main_blocks - 1), 0))
    tail_spec = pl.BlockSpec((tb, D_MODEL), lambda i, j: (0, 0))
    parts = PEER_TABLE_PARTS
    tok_specs = [pl.BlockSpec((None, 2 * PEER_HEADS // parts, N_KEYS, tb), lambda i, j, q=q: (i, q, 0, 0))
                 for q in range(parts)]
    u_specs = [pl.BlockSpec((eb // parts, D_MODEL), lambda i, j, q=q: (jnp.minimum(j, n_blocks - 1) * parts + q, 0))
               for q in range(parts)]
    vT_specs = [pl.BlockSpec((None, None, D_MODEL, eb // parts),
                             lambda i, j, q=q: (jnp.clip(j - 2, 0, n_blocks - 1), q, 0, 0)) for q in range(parts)]
    return pl.pallas_call(
        _peer_expert_body,
        grid=(t_pad // tb, n_blocks + 2),
        in_specs=[
            pl.BlockSpec((D_MODEL, tb), lambda i, j: (0, i)),
            *u_specs, *vT_specs, *tok_specs, *tok_specs,
            main_spec, tail_spec,
            pl.BlockSpec((1, D_MODEL), lambda i, j: (0, 0)),
        ],
        out_specs=[main_spec, tail_spec],
        out_shape=[jax.ShapeDtypeStruct(res.shape, jnp.float32), jax.ShapeDtypeStruct(res_tail.shape, jnp.float32)],
        scratch_shapes=[
            pltpu.VMEM((eb, tb), jnp.float32), pltpu.VMEM((eb, tb), jnp.float32),
            pltpu.VMEM((eb, tb), jnp.bfloat16), pltpu.VMEM((eb, tb), jnp.bfloat16),
            pltpu.VMEM((D_MODEL, tb), jnp.float32),
        ],
        compiler_params=pltpu.CompilerParams(
            dimension_semantics=("arbitrary", "arbitrary"),
            vmem_limit_bytes=PEER_VMEM_LIMIT),
        name="peer_experts",
    )(hnT, *[u_bf] * parts, *[vT_bf] * parts, *[a] * parts, *[e] * parts, res, res_tail,
      g_final.reshape(1, D_MODEL))


def peer_block(h, hnT, h_tail, hnT_tail, w_q, sub_keys, u_tab, v_tab, g_final):
    tb = PEER_TOKEN_BLOCK
    t_tail = h_tail.shape[0]
    assert h.shape[0] % tb == 0 and t_tail <= tb
    hnT_all = jnp.concatenate([hnT, jnp.pad(hnT_tail, ((0, 0), (0, tb - t_tail)))], axis=1)
    a, e = _peer_route(hnT_all, w_q.astype(jnp.bfloat16).T, sub_keys.astype(jnp.bfloat16))
    out, out_tail = _peer_experts(
        hnT_all, a, e, h, jnp.pad(h_tail, ((0, tb - t_tail), (0, 0))), g_final, u_tab.astype(jnp.bfloat16),
        v_tab.astype(jnp.bfloat16).reshape(-1, PEER_TABLE_PARTS, PEER_EXPERT_BLOCK // PEER_TABLE_PARTS,
                                           D_MODEL).transpose(0, 1, 3, 2))
    return out, out_tail[:t_tail]


ATTN_VMEM_LIMIT = 40 * 1024 * 1024
HEADS_PER_TILE = LANES // HEAD_DIM
ATTN_STREAMS = 4


def _band_units(units, sinks):
    f32, bf16 = jnp.float32, jnp.bfloat16
    nt = (((1,), (1,)), ((), ()))
    rows = HEADS_PER_TILE * BLOCK
    qi = lax.broadcasted_iota(jnp.int32, (rows, 2 * BLOCK), 0) % BLOCK
    kj = lax.broadcasted_iota(jnp.int32, (rows, 2 * BLOCK), 1)
    off = BLOCK + qi - kj
    band = (off >= 0) & (off <= BLOCK)
    own_block = kj >= BLOCK
    ss = []
    for qs, load_k, _, key_lanes, _ in units:
        lhs = jnp.concatenate([jnp.where(key_lanes[i], qs[i], 0.0) for i in range(HEADS_PER_TILE)], axis=0)
        ss.append(lax.dot_general(lhs.astype(bf16), load_k().astype(bf16), nt, preferred_element_type=f32) * SCALE)
    ss = [jnp.where(band & (own_block | jnp.logical_not(unit[4])), s, NEG_INF) for s, unit in zip(ss, units)]
    if sinks is not None:
        assert HEADS_PER_TILE == 2
        sink_slot = kj == (qi + BLOCK + 1) % (2 * BLOCK)
        head0 = lax.broadcasted_iota(jnp.int32, (rows, 2 * BLOCK), 0) < BLOCK
        sink_logit = jnp.where(head0, sinks[0], sinks[1])
        ss = [jnp.where(sink_slot, sink_logit, s) for s in ss]
    ms = [jnp.max(s, axis=-1, keepdims=True) for s in ss]
    es = [jnp.exp(s - m) for s, m in zip(ss, ms)]
    denoms = [jnp.sum(e, axis=-1, keepdims=True) for e in es]
    results = []
    for e, d, m, unit in zip(es, denoms, ms, units):
        p = e / d
        if sinks is not None:
            p = jnp.where(sink_slot, 0.0, p)
        o = jnp.dot(p.astype(bf16), unit[2]().astype(bf16), preferred_element_type=f32)
        lse = m + jnp.log(d)
        results.append([(o[i * BLOCK:(i + 1) * BLOCK, :], lse[i * BLOCK:(i + 1) * BLOCK, :])
                        for i in range(HEADS_PER_TILE)])
    return results


def _prompt_attention_body(qa_ref, ka_ref, va_ref, qb_ref, kb_ref, vb_ref, sink_ref, oa_ref, ob_ref,
                           o_scr, lse_scr):
    f32 = jnp.float32
    seq = qa_ref.shape[1]
    lane = lax.broadcasted_iota(jnp.int32, (1, LANES), 1)
    low = lane < HEAD_DIM
    own = [low, jnp.logical_not(low)]

    def window(ref, prev, cur):
        return lambda: jnp.concatenate([ref[0, prev, :], ref[0, cur, :]], axis=0)

    for c, (reach, dil) in enumerate(DILATED_CONFIGS):
        assert reach // dil == BLOCK
        blocks = seq // (dil * BLOCK)

        def step(it, carry, c=c, dil=dil, blocks=blocks):
            units, curs = [], []
            for k in range(ATTN_STREAMS):
                u = it * ATTN_STREAMS + k
                res, blk = u // blocks, u % blocks
                cur = pl.ds(res + dil * BLOCK * blk, BLOCK, stride=dil)
                prev = pl.ds(res + dil * BLOCK * jnp.maximum(blk - 1, 0), BLOCK, stride=dil)
                q = qa_ref[0, cur, :]
                units.append(([q, q], window(ka_ref, prev, cur), window(va_ref, prev, cur), own, blk == 0))
                curs.append(cur)
            for cur, ((o0, l0), (o1, l1)) in zip(curs, _band_units(units, None)):
                o_scr[c, cur, :] = jnp.where(low, o0, o1)
                lse_scr[c, cur, :] = jnp.where(low, l0, l1)
            return carry

        lax.fori_loop(0, dil * blocks // ATTN_STREAMS, step, 0)

    def merge(t, carry):
        rows = pl.ds(pl.multiple_of(t * BLOCK, BLOCK), BLOCK)
        ls = [lse_scr[c, rows, :] for c in range(len(DILATED_CONFIGS))]
        top = jnp.maximum(jnp.maximum(ls[0], ls[1]), ls[2])
        ws = [jnp.exp(l - top) for l in ls]
        num = ws[0] * o_scr[0, rows, :] + ws[1] * o_scr[1, rows, :] + ws[2] * o_scr[2, rows, :]
        oa_ref[0, rows, :] = num / (ws[0] + ws[1] + ws[2])
        return carry

    lax.fori_loop(0, seq // BLOCK, merge, 0)

    slab = pl.program_id(1)
    sinks = [sink_ref[slab * HEADS_PER_TILE + i] for i in range(HEADS_PER_TILE)]

    def step_b(it, carry):
        units, curs = [], []
        for k in range(ATTN_STREAMS):
            blk = it * ATTN_STREAMS + k
            cur = pl.ds(pl.multiple_of(blk * BLOCK, BLOCK), BLOCK)
            prev = pl.ds(pl.multiple_of(jnp.maximum(blk - 1, 0) * BLOCK, BLOCK), BLOCK)
            q = qb_ref[0, cur, :]
            units.append(([q, q], window(kb_ref, prev, cur), window(vb_ref, prev, cur), own, blk == 0))
            curs.append(cur)
        for cur, ((o0, _), (o1, _)) in zip(curs, _band_units(units, sinks)):
            ob_ref[0, cur, :] = jnp.where(low, o0, o1)
        return carry

    lax.fori_loop(0, seq // BLOCK // ATTN_STREAMS, step_b, 0)


def prompt_attention(qa, ka, va, qb, kb, vb, sink):
    n, seq, _ = qa.shape
    assert kb.shape[-1] == KV_HEADS_B * LANES and seq % (BLOCK * max(d for _, d in DILATED_CONFIGS)) == 0
    slab = pl.BlockSpec((1, seq, LANES), lambda b, p: (b, 0, p))
    whole = pl.BlockSpec((1, seq, LANES), lambda b, p: (b, 0, p * HEADS_PER_TILE // GROUP_B))
    f32 = jnp.float32
    return pl.pallas_call(
        _prompt_attention_body,
        grid=(n, WIDTH_A // LANES),
        in_specs=[slab, slab, slab, slab, whole, whole, pl.BlockSpec(memory_space=pltpu.SMEM)],
        out_specs=[slab, slab],
        out_shape=[jax.ShapeDtypeStruct((n, seq, WIDTH_A), f32), jax.ShapeDtypeStruct((n, seq, WIDTH_B), f32)],
        scratch_shapes=[pltpu.VMEM((len(DILATED_CONFIGS), seq, LANES), f32),
                        pltpu.VMEM((len(DILATED_CONFIGS), seq, LANES), f32)],
        compiler_params=pltpu.CompilerParams(
            dimension_semantics=("parallel", "parallel"),
            vmem_limit_bytes=ATTN_VMEM_LIMIT),
        name="prompt_attention",
    )(qa, ka, va, qb, kb, vb, sink.reshape(HEADS_B))


PROJ_VMEM_LIMIT = 48 * 1024 * 1024
PROJ_TOKEN_BLOCK = 512
QKV_WIDTHS = IN_WIDTHS[:6]
QKV_WIDTH = sum(QKV_WIDTHS)
QKV_ROTATED = (True, True, False, True, True, False)


def _rms_normed(x, g):
    return x * lax.rsqrt(jnp.mean(x * x, axis=-1, keepdims=True) + NORM_EPS) * g


def _rope_slab(x, cos, sin_signed):
    lane = lax.broadcasted_iota(jnp.int32, (1, LANES), 1)
    half = HEAD_DIM // 2
    partner = jnp.where(lane % HEAD_DIM < half, pltpu.roll(x, LANES - half, axis=1), pltpu.roll(x, half, axis=1))
    return x * cos + partner * sin_signed


def _in_proj_body(x_ref, g_ref, w_ref, cos_ref, sin_ref, *out_refs, widths, channel_major):
    xn = _rms_normed(x_ref[...], g_ref[...]).astype(jnp.bfloat16)
    cos, sin = cos_ref[...], sin_ref[...]
    groups = []
    c0 = 0
    for width, rotated in zip(widths, QKV_ROTATED):
        z = jnp.dot(xn, w_ref[:, c0:c0 + width], preferred_element_type=jnp.float32)
        if rotated:
            z = jnp.concatenate([_rope_slab(z[:, c:c + LANES], cos, sin) for c in range(0, width, LANES)], axis=1)
        groups.append(z)
        c0 += width
    for ref, z in zip(out_refs[:6], groups):
        ref[...] = z
    if channel_major:
        kaT_ref, vaT_ref, kbT_ref, vbT_ref = out_refs[6:]
        kaT_ref[0] = groups[1].T
        vaT_ref[0] = groups[2].T
        tb = x_ref.shape[0]
        low = lax.broadcasted_iota(jnp.int32, (1, LANES), 1) < HEAD_DIM
        for ref, z in ((kbT_ref, groups[4]), (vbT_ref, groups[5])):
            tail = z[tb - WINDOW_B:, :]
            ref[0] = jnp.where(low, tail[:, :LANES], tail[:, LANES:]).T


def in_proj(x, g_mix, w_bf, cos, sin, tb, seq):
    t = x.shape[0]
    channel_major = seq is not None
    per_seq = seq // tb if channel_major else None
    widths = QKV_WIDTHS[:4] + ((2 * KV_WIDTH_B,) * 2 if channel_major else QKV_WIDTHS[4:])
    assert w_bf.shape[1] == sum(widths)
    f32 = jnp.float32
    tok = lambda w: pl.BlockSpec((tb, w), lambda i: (i, 0))
    out_specs = [tok(w) for w in widths]
    out_shape = [jax.ShapeDtypeStruct((t, w), f32) for w in widths]
    if channel_major:
        assert seq == WINDOW_A and tb >= WINDOW_B and KV_WIDTH_B == LANES
        n = t // seq
        out_specs += [pl.BlockSpec((1, WIDTH_A, tb), lambda i: (i // per_seq, 0, i % per_seq))] * 2
        out_specs += [pl.BlockSpec((1, KV_WIDTH_B, WINDOW_B), lambda i: (i // per_seq, 0, 0))] * 2
        out_shape += [jax.ShapeDtypeStruct((n, WIDTH_A, seq), f32)] * 2
        out_shape += [jax.ShapeDtypeStruct((n, KV_WIDTH_B, WINDOW_B), f32)] * 2
    return pl.pallas_call(
        functools.partial(_in_proj_body, widths=widths, channel_major=channel_major),
        grid=(t // tb,),
        in_specs=[tok(D_MODEL), pl.BlockSpec((1, D_MODEL), lambda i: (0, 0)),
                  pl.BlockSpec(w_bf.shape, lambda i: (0, 0)), tok(LANES), tok(LANES)],
        out_specs=out_specs,
        out_shape=out_shape,
        compiler_params=pltpu.CompilerParams(
            dimension_semantics=("arbitrary",),
            vmem_limit_bytes=PROJ_VMEM_LIMIT),
        name="in_proj",
    )(x, g_mix.reshape(1, D_MODEL), w_bf, cos, sin)


def duplicate_kv_columns(w_qkv):
    edges = np.cumsum((0,) + QKV_WIDTHS)
    cols = np.arange(KV_WIDTH_B).reshape(KV_HEADS_B, 1, HEAD_DIM)
    cols = np.broadcast_to(cols, (KV_HEADS_B, LANES // HEAD_DIM, HEAD_DIM)).reshape(-1)
    return jnp.concatenate([w_qkv[:, :edges[4]], w_qkv[:, edges[4] + cols], w_qkv[:, edges[5] + cols]], axis=1)


def _out_proj_body(x_ref, oa_ref, ob_ref, gmix_ref, wg_ref, wa_ref, wb_ref, wo_ref, gffn_ref, h_ref, hnT_ref):
    f32, bf16 = jnp.float32, jnp.bfloat16
    x = x_ref[...]
    xn = _rms_normed(x, gmix_ref[...]).astype(bf16)
    ya = jnp.dot(oa_ref[...].astype(bf16), wa_ref[...], preferred_element_type=f32)
    merged = jax.nn.sigmoid(jnp.dot(xn, wg_ref[:, :D_MODEL], preferred_element_type=f32)) * ya
    yb = jnp.dot(ob_ref[...].astype(bf16), wb_ref[...], preferred_element_type=f32)
    merged = merged + jax.nn.sigmoid(jnp.dot(xn, wg_ref[:, D_MODEL:], preferred_element_type=f32)) * yb
    h = x + jnp.dot(merged.astype(bf16), wo_ref[...], preferred_element_type=f32)
    h_ref[...] = h
    hnT_ref[...] = _rms_normed(h, gffn_ref[...]).T.astype(bf16)


def out_proj(x, oa, ob, g_mix, w_gate_bf, w_a_bf, w_b_bf, w_o_bf, g_ffn, tb):
    t = x.shape[0]
    tok = lambda w: pl.BlockSpec((tb, w), lambda i: (i, 0))
    full = lambda a: pl.BlockSpec(a.shape, lambda i: (0, 0))
    g_mix, g_ffn = g_mix.reshape(1, D_MODEL), g_ffn.reshape(1, D_MODEL)
    return pl.pallas_call(
        _out_proj_body,
        grid=(t // tb,),
        in_specs=[tok(D_MODEL), tok(WIDTH_A), tok(WIDTH_B), full(g_mix), full(w_gate_bf), full(w_a_bf),
                  full(w_b_bf), full(w_o_bf), full(g_ffn)],
        out_specs=[tok(D_MODEL), pl.BlockSpec((D_MODEL, tb), lambda i: (0, i))],
        out_shape=[jax.ShapeDtypeStruct((t, D_MODEL), jnp.float32),
                   jax.ShapeDtypeStruct((D_MODEL, t), jnp.bfloat16)],
        compiler_params=pltpu.CompilerParams(
            dimension_semantics=("parallel",),
            vmem_limit_bytes=PROJ_VMEM_LIMIT),
        name="out_proj",
    )(x, oa, ob, g_mix, w_gate_bf, w_a_bf, w_b_bf, w_o_bf, g_ffn)


def rotary_tables(pos):
    inv = ROPE_THETA ** (-jnp.arange(0, HEAD_DIM, 2, dtype=jnp.float32) / HEAD_DIM)
    ang = pos.astype(jnp.float32)[:, None] * inv[None, :]
    cos, sin = jnp.cos(ang), jnp.sin(ang)
    reps = LANES // HEAD_DIM
    return jnp.tile(jnp.concatenate([cos, cos], axis=1), (1, reps)), jnp.tile(jnp.concatenate([-sin, sin], axis=1), (1, reps))


def kernel(x_prompt, x_sample, cache_a_k, cache_a_v, cache_b_k, cache_b_v, norm_mix, w_in,
           w_branch_a, w_branch_b, w_out, sink_b, norm_ffn, w_peer_q, peer_sub_keys, peer_u,
           peer_v, norm_final):
    assert DEPTH == 1
    bf16 = jnp.bfloat16
    n, seq, _ = x_prompt.shape
    ns, dec = x_sample.shape[:2]
    assert dec == 1
    w_in_bf = w_in[0].astype(bf16)
    w_qkv, w_gate = w_in_bf[:, :QKV_WIDTH], w_in_bf[:, QKV_WIDTH:]
    proj_weights = (norm_mix[0], w_gate, w_branch_a[0].astype(bf16), w_branch_b[0].astype(bf16),
                    w_out[0].astype(bf16), norm_ffn[0])
    sink = sink_b[0].astype(jnp.float32)

    xp = x_prompt.reshape(n * seq, D_MODEL)
    cos, sin = rotary_tables(jnp.tile(jnp.arange(seq), n))
    qa, ka, va, qb, kb, vb, ka_t, va_t, kb_t, vb_t = in_proj(xp, norm_mix[0], duplicate_kv_columns(w_qkv), cos, sin,
                                                             PROJ_TOKEN_BLOCK, seq)
    per_seq = lambda a: a.reshape(n, seq, a.shape[-1])
    oa, ob = prompt_attention(per_seq(qa), per_seq(ka), per_seq(va), per_seq(qb), per_seq(kb), per_seq(vb), sink)
    hp, hnt_p = out_proj(xp, oa.reshape(n * seq, WIDTH_A), ob.reshape(n * seq, WIDTH_B), *proj_weights,
                         PROJ_TOKEN_BLOCK)
    windows = lambda a, heads: a.reshape(n, heads, HEAD_DIM, a.shape[-1]).transpose(0, 3, 1, 2)[None]
    state_p = (windows(ka_t, HEADS_A), windows(va_t, HEADS_A), windows(kb_t, KV_HEADS_B), windows(vb_t, KV_HEADS_B))

    xs = x_sample.reshape(ns, D_MODEL)
    cos, sin = rotary_tables(jnp.full((ns,), PAST_LEN))
    qa, ka, va, qb, kb, vb = in_proj(xs, norm_mix[0], w_qkv, cos, sin, ns, None)
    oa, ob, state_s = sample_mixers(qa, ka, va, qb, kb, vb, sink, cache_a_k[0], cache_a_v[0],
                                    cache_b_k[0], cache_b_v[0])
    hs, hnt_s = out_proj(xs, oa, ob, *proj_weights, ns)

    y_prompt, y_sample = peer_block(hp, hnt_p, hs, hnt_s, w_peer_q[0], peer_sub_keys[0], peer_u[0], peer_v[0],
                                    norm_final)
    return (y_prompt.reshape(x_prompt.shape), y_sample.reshape(x_sample.shape), *state_p,
            *[a[None] for a in state_s])
```

```python
import functools
import math
import jax, jax.numpy as jnp
from jax import lax
import numpy as np
from jax.experimental import pallas as pl
from jax.experimental.pallas import tpu as pltpu

D_MODEL = 1024
DEPTH = 1
PAST_LEN = 8192

HEAD_DIM = 64
HEADS_A = 8
DILATED_CONFIGS = ((128, 1), (512, 4), (2048, 16))
WINDOW_A = 2048
HEADS_B = 8
KV_HEADS_B = 2
GROUP_B = HEADS_B // KV_HEADS_B
WINDOW_B = 128
BLOCK = 128
ROPE_THETA = 10000.0
NORM_EPS = 1e-6
NEG_INF = -1e30
SCALE = HEAD_DIM ** -0.5
assert math.frexp(SCALE)[0] == 0.5

WIDTH_A = HEADS_A * HEAD_DIM
WIDTH_B = HEADS_B * HEAD_DIM
KV_WIDTH_B = KV_HEADS_B * HEAD_DIM
IN_WIDTHS = (WIDTH_A, WIDTH_A, WIDTH_A, WIDTH_B, KV_WIDTH_B, KV_WIDTH_B, D_MODEL, D_MODEL)

N_KEYS = 128
N_EXPERTS = N_KEYS * N_KEYS
PEER_HEADS = 8
PEER_TOPK = 16
PEER_HALF = 128


LANES = 128
SAMPLE_VMEM_LIMIT = 48 * 1024 * 1024
SAMPLE_PARTS = 2


def _bf16_round(x):
    return x.astype(jnp.bfloat16).astype(jnp.float32)


def _decode_softmax(s, s_new, sink):
    m = jnp.maximum(jnp.max(s, axis=-1, keepdims=True), s_new)
    if sink is not None:
        m = jnp.maximum(m, sink)
    e = jnp.exp(s - m)
    e_new = jnp.exp(s_new - m)
    denom = jnp.sum(e, axis=-1, keepdims=True) + e_new
    if sink is not None:
        denom = denom + jnp.exp(sink - m)
    return e / denom, e_new / denom, m + jnp.log(denom)


def _sample_mixer_body(qa_ref, ka_ref, va_ref, qb_ref, kb_ref, vb_ref, sink_ref, *refs):
    f32, bf16 = jnp.float32, jnp.bfloat16
    nt = (((1,), (1,)), ((), ()))
    sp = SAMPLE_PARTS
    cak_refs, cav_refs = refs[:sp], refs[sp:2 * sp]
    cbk_ref, cbv_ref, oa_ref, ob_ref, nak_ref, nav_ref, nbk_ref, nbv_ref, s_scr, o_scr = refs[2 * sp:]
    heads_per_part = HEADS_A // sp
    win_a = cak_refs[0].shape[2]
    lane_tiles = win_a // LANES

    def as_column(row):
        return jnp.broadcast_to(row, (LANES, row.shape[1])).T

    not_last_lane = lax.broadcasted_iota(jnp.int32, (1, LANES), 1) < LANES - 1

    def shifted(old, col):
        rolled = pltpu.roll(old, old.shape[1] - 1, axis=1)
        pos = lax.broadcasted_iota(jnp.int32, old.shape, 1)
        return jnp.where(pos == old.shape[1] - 1, jnp.tile(col, (1, old.shape[1] // LANES)), rolled)

    qa, ka, va = qa_ref[0], ka_ref[0], va_ref[0]
    own = (lax.broadcasted_iota(jnp.int32, (HEADS_A, WIDTH_A), 1) // HEAD_DIM
           == lax.broadcasted_iota(jnp.int32, (HEADS_A, WIDTH_A), 0))
    s_new = jnp.sum(jnp.where(own, qa * ka, 0.0), axis=-1, keepdims=True) * SCALE
    q_col = as_column(qa)
    k_col = as_column(ka)
    v_col = as_column(va)
    for h in range(HEADS_A):
        rows = slice(h * HEAD_DIM, (h + 1) * HEAD_DIM)
        local = slice((h % heads_per_part) * HEAD_DIM, (h % heads_per_part + 1) * HEAD_DIM)
        cak_ref = cak_refs[h // heads_per_part]
        qh = q_col[rows, :]

        left = k_col[rows, :]
        for t in reversed(range(lane_tiles)):
            tile = slice(t * LANES, (t + 1) * LANES)
            cur = cak_ref[0, local, tile]
            s_scr[h:h + 1, tile] = jnp.sum(cur * qh, axis=0, keepdims=True)
            cur_left = pltpu.roll(cur, LANES - 1, axis=1)
            nak_ref[0, rows, tile] = jnp.where(not_last_lane, cur_left, left)
            left = cur_left

    s = s_scr[...] * SCALE
    pos = lax.broadcasted_iota(jnp.int32, s.shape, 1)
    ps, p_news, lses = [], [], []
    for window, dil in DILATED_CONFIGS:
        reach = (pos >= win_a - window) & (pos % dil == 0)
        p, p_new, lse = _decode_softmax(jnp.where(reach, s, NEG_INF), s_new, None)
        ps.append(p)
        p_news.append(p_new)
        lses.append(lse)
    top = jnp.maximum(jnp.maximum(lses[0], lses[1]), lses[2])
    ws = [jnp.exp(l - top) for l in lses]
    inv = 1.0 / (ws[0] + ws[1] + ws[2])
    s_scr[...] = (ws[0] * ps[0] + ws[1] * ps[1] + ws[2] * ps[2]) * inv
    p_new = (ws[0] * p_news[0] + ws[1] * p_news[1] + ws[2] * p_news[2]) * inv

    for h in range(HEADS_A):
        rows = slice(h * HEAD_DIM, (h + 1) * HEAD_DIM)
        local = slice((h % heads_per_part) * HEAD_DIM, (h % heads_per_part + 1) * HEAD_DIM)
        cav_ref = cav_refs[h // heads_per_part]

        left = v_col[rows, :]
        acc = jnp.zeros((HEAD_DIM, LANES), f32)
        for t in reversed(range(lane_tiles)):
            tile = slice(t * LANES, (t + 1) * LANES)
            cur = cav_ref[0, local, tile]
            acc = acc + cur * s_scr[h:h + 1, tile]
            cur_left = pltpu.roll(cur, LANES - 1, axis=1)
            nav_ref[0, rows, tile] = jnp.where(not_last_lane, cur_left, left)
            left = cur_left
        o_scr[rows, :] = jnp.broadcast_to(jnp.sum(acc, axis=1, keepdims=True), (HEAD_DIM, LANES))
    p_new_lanes = jnp.sum(jnp.where(own, p_new, 0.0), axis=0, keepdims=True)
    oa_ref[0] = o_scr[...].T[0:1, :] + p_new_lanes * va

    qb, kb, vb = qb_ref[0], kb_ref[0], vb_ref[0]
    lane = lax.broadcasted_iota(jnp.int32, (1, KV_WIDTH_B), 1)
    heads_per_chunk = KV_WIDTH_B // HEAD_DIM
    q_rows = []
    for h in range(HEADS_B):
        c = h // heads_per_chunk
        piece = qb[:, c * KV_WIDTH_B:(c + 1) * KV_WIDTH_B]
        if h % heads_per_chunk != h // GROUP_B:
            piece = pltpu.roll(piece, HEAD_DIM, axis=1)
        q_rows.append(jnp.where(lane // HEAD_DIM == h // GROUP_B, piece, 0.0))
    q_rows = jnp.concatenate(q_rows, axis=0)
    s = jnp.dot(q_rows.astype(bf16), cbk_ref[0].astype(bf16), preferred_element_type=f32) * SCALE
    s_new = jnp.sum(_bf16_round(q_rows) * _bf16_round(kb), axis=-1, keepdims=True) * SCALE
    p, p_new, _ = _decode_softmax(s, s_new, sink_ref[...])
    o = (lax.dot_general(p.astype(bf16), cbv_ref[0].astype(bf16), nt, preferred_element_type=f32)
         + _bf16_round(p_new) * _bf16_round(vb))
    for c in range(WIDTH_B // KV_WIDTH_B):
        halves = []
        for slot in range(heads_per_chunk):
            h = c * heads_per_chunk + slot
            r = o[h:h + 1, :]
            if h // GROUP_B != slot:
                r = pltpu.roll(r, HEAD_DIM, axis=1)
            halves.append(r)
        ob_ref[0, :, c * KV_WIDTH_B:(c + 1) * KV_WIDTH_B] = jnp.where(lane < HEAD_DIM, halves[0], halves[1])

    nbk_ref[0] = shifted(cbk_ref[0], as_column(kb))
    nbv_ref[0] = shifted(cbv_ref[0], as_column(vb))


def sample_mixers(qa, ka, va, qb, kb, vb, sink, ck_a, cv_a, ck_b, cv_b):
    n = qa.shape[0]
    assert ck_a.shape[1] == WINDOW_A and ck_b.shape[1] == WINDOW_B
    assert KV_WIDTH_B == 2 * HEAD_DIM == LANES
    row = lambda a, w: a.reshape(n, 1, w)
    win = lambda a, w: a.transpose(0, 2, 3, 1).reshape(n, w, a.shape[1])
    unwin = lambda a, like: a.reshape(n, like.shape[2], like.shape[3], like.shape[1]).transpose(0, 3, 1, 2)
    row_spec = lambda w: pl.BlockSpec((1, 1, w), lambda b: (b, 0, 0))
    win_spec = lambda r, w: pl.BlockSpec((1, w, r), lambda b: (b, 0, 0))
    part_specs = [pl.BlockSpec((1, WIDTH_A // SAMPLE_PARTS, WINDOW_A), lambda b, q=q: (b, q, 0))
                  for q in range(SAMPLE_PARTS)]
    f32 = jnp.float32
    oa, ob, nak, nav, nbk, nbv = pl.pallas_call(
        _sample_mixer_body,
        grid=(n,),
        in_specs=[row_spec(WIDTH_A), row_spec(WIDTH_A), row_spec(WIDTH_A),
                  row_spec(WIDTH_B), row_spec(KV_WIDTH_B), row_spec(KV_WIDTH_B),
                  pl.BlockSpec((HEADS_B, 1), lambda b: (0, 0)),
                  *part_specs, *part_specs,
                  win_spec(WINDOW_B, KV_WIDTH_B), win_spec(WINDOW_B, KV_WIDTH_B)],
        out_specs=[row_spec(WIDTH_A), row_spec(WIDTH_B),
                   win_spec(WINDOW_A, WIDTH_A), win_spec(WINDOW_A, WIDTH_A),
                   win_spec(WINDOW_B, KV_WIDTH_B), win_spec(WINDOW_B, KV_WIDTH_B)],
        out_shape=[jax.ShapeDtypeStruct((n, 1, WIDTH_A), f32), jax.ShapeDtypeStruct((n, 1, WIDTH_B), f32),
                   jax.ShapeDtypeStruct((n, WIDTH_A, WINDOW_A), f32), jax.ShapeDtypeStruct((n, WIDTH_A, WINDOW_A), f32),
                   jax.ShapeDtypeStruct((n, KV_WIDTH_B, WINDOW_B), f32),
                   jax.ShapeDtypeStruct((n, KV_WIDTH_B, WINDOW_B), f32)],
        scratch_shapes=[pltpu.VMEM((HEADS_A, WINDOW_A), f32), pltpu.VMEM((WIDTH_A, LANES), f32)],
        compiler_params=pltpu.CompilerParams(
            dimension_semantics=("parallel",),
            vmem_limit_bytes=SAMPLE_VMEM_LIMIT),
        name="sample_mixers",
    )(row(qa, WIDTH_A), row(ka, WIDTH_A), row(va, WIDTH_A), row(qb, WIDTH_B), row(kb, KV_WIDTH_B),
      row(vb, KV_WIDTH_B), sink.reshape(HEADS_B, 1),
      *[win(ck_a, WIDTH_A)] * SAMPLE_PARTS, *[win(cv_a, WIDTH_A)] * SAMPLE_PARTS,
      win(ck_b, KV_WIDTH_B), win(cv_b, KV_WIDTH_B))
    state = (unwin(nak, ck_a), unwin(nav, cv_a), unwin(nbk, ck_b), unwin(nbv, cv_b))
    return oa.reshape(n, WIDTH_A), ob.reshape(n, WIDTH_B), state


PEER_TOKEN_BLOCK = 512
PEER_EXPERT_BLOCK = 1024
PEER_ROWS = 16
PEER_PIPE_ROWS = 256
PEER_GATE_ROWS = 64
PEER_TABLE_PARTS = 4
ROUTE_STREAMS = 4
TOP_ROWS = 24
PEER_VMEM_LIMIT = 48 * 1024 * 1024
INV_SQRT2 = 0.7071067811865476


def _peer_route_body(hnT_ref, wqT_ref, keys_ref, a_ref, e_ref, q_scr, s_scr, top_scr, thr_scr, invz_scr):
    tb = PEER_TOKEN_BLOCK
    lane_tiles = tb // LANES
    q_scr[...] = jnp.dot(wqT_ref[...], hnT_ref[...], preferred_element_type=jnp.float32).astype(jnp.bfloat16)
    for hp in range(2 * PEER_HEADS):
        s_scr[hp] = jnp.dot(keys_ref[hp % 2], q_scr[hp * PEER_HALF:(hp + 1) * PEER_HALF, :],
                            preferred_element_type=jnp.float32)

    def take_max(x, iota, n):
        m = jnp.max(x, axis=0, keepdims=True)
        first = jnp.min(jnp.where(x == m, iota, float(n)), axis=0, keepdims=True)
        return m, jnp.where(iota == first, -jnp.inf, x)

    groups = lane_tiles // ROUTE_STREAMS

    def tile_lanes(u, k):
        return pl.ds(pl.multiple_of(((u % groups) * ROUTE_STREAMS + k) * LANES, LANES), LANES)

    def half_top(u, carry):
        hp = u // groups
        iota = lax.broadcasted_iota(jnp.int32, (N_KEYS, LANES), 0).astype(jnp.float32)
        lanes = [tile_lanes(u, k) for k in range(ROUTE_STREAMS)]
        xs = [s_scr[hp, :, ln] for ln in lanes]
        for ln in lanes:
            top_scr[hp, PEER_TOPK:, ln] = jnp.full((TOP_ROWS - PEER_TOPK, LANES), -jnp.inf, jnp.float32)
        for r in range(PEER_TOPK + 1):
            for k, ln in enumerate(lanes):
                m, xs[k] = take_max(xs[k], iota, N_KEYS)
                top_scr[hp, pl.ds(r, 1), ln] = m
        return carry

    lax.fori_loop(0, 2 * PEER_HEADS * groups, half_top, 0)

    def pair_top(u, carry):
        h = u // groups
        lanes = [tile_lanes(u, k) for k in range(ROUTE_STREAMS)]
        xs = []
        for ln in lanes:
            t1 = top_scr[2 * h, :, ln]
            t2 = top_scr[2 * h + 1, :, ln]
            xs.append(jnp.concatenate([t1[0:1, :] + t2] + [t1[k:k + 1, :] + t2[0:8, :] for k in range(1, 8)]
                                      + [t1[8:, :] + t2[0:1, :]], axis=0))
        n = xs[0].shape[0]
        iota = lax.broadcasted_iota(jnp.int32, (n, LANES), 0).astype(jnp.float32)
        best, v, z = [None] * ROUTE_STREAMS, [None] * ROUTE_STREAMS, [None] * ROUTE_STREAMS
        for r in range(PEER_TOPK):
            for k in range(ROUTE_STREAMS):
                v[k], xs[k] = take_max(xs[k], iota, n)
                if r == 0:
                    best[k], z[k] = v[k], jnp.ones_like(v[k])
                else:
                    z[k] = z[k] + jnp.exp(v[k] - best[k])
        for k, ln in enumerate(lanes):
            nxt, _ = take_max(xs[k], iota, n)
            thr_scr[h, :, ln] = 0.5 * (v[k] + nxt)
            invz_scr[h, :, ln] = 1.0 / z[k]
        return carry

    lax.fori_loop(0, PEER_HEADS * groups, pair_top, 0)

    def emit(u, carry):
        h = u // (N_KEYS // PEER_ROWS)
        rows = pl.ds(pl.multiple_of((u % (N_KEYS // PEER_ROWS)) * PEER_ROWS, PEER_ROWS), PEER_ROWS)
        s1 = s_scr[2 * h, rows, :]
        s2 = s_scr[2 * h + 1, rows, :]
        outs = ((a_ref, 2 * h, thr_scr[h] - s1), (a_ref, 2 * h + 1, s2),
                (e_ref, 2 * h, jnp.exp(s1 - top_scr[2 * h, pl.ds(0, 1), :]) * invz_scr[h]),
                (e_ref, 2 * h + 1, jnp.exp(s2 - top_scr[2 * h + 1, pl.ds(0, 1), :])))
        for ref, hp, value in outs:
            for lt in range(lane_tiles):
                ref[hp, lt, rows, :] = value[:, lt * LANES:(lt + 1) * LANES]
        return carry

    lax.fori_loop(0, PEER_HEADS * (N_KEYS // PEER_ROWS), emit, 0)


def _peer_route(hnT, wqT, keys_bf):
    t_pad = hnT.shape[1]
    tb = PEER_TOKEN_BLOCK
    hp = 2 * PEER_HEADS
    tok3 = pl.BlockSpec((None, hp, tb // LANES, N_KEYS, LANES), lambda i: (i, 0, 0, 0, 0))
    return pl.pallas_call(
        _peer_route_body,
        grid=(t_pad // tb,),
        in_specs=[
            pl.BlockSpec((D_MODEL, tb), lambda i: (0, i)),
            pl.BlockSpec((hp * PEER_HALF, D_MODEL), lambda i: (0, 0)),
            pl.BlockSpec((2, N_KEYS, PEER_HALF), lambda i: (0, 0, 0)),
        ],
        out_specs=[tok3, tok3],
        out_shape=[jax.ShapeDtypeStruct((t_pad // tb, hp, tb // LANES, N_KEYS, LANES), jnp.float32)] * 2,
        scratch_shapes=[
            pltpu.VMEM((hp * PEER_HALF, tb), jnp.bfloat16),
            pltpu.VMEM((hp, N_KEYS, tb), jnp.float32),
            pltpu.VMEM((hp, TOP_ROWS, tb), jnp.float32),
            pltpu.VMEM((PEER_HEADS, 1, tb), jnp.float32),
            pltpu.VMEM((PEER_HEADS, 1, tb), jnp.float32),
        ],
        compiler_params=pltpu.CompilerParams(
            dimension_semantics=("parallel",),
            vmem_limit_bytes=PEER_VMEM_LIMIT),
        name="peer_route",
    )(hnT, wqT, keys_bf)


def _peer_expert_body(hnT_ref, *refs):
    parts = PEER_TABLE_PARTS
    u_refs, vT_refs, a_refs, e_refs = (refs[k * parts:(k + 1) * parts] for k in range(4))
    res_ref, res_tail_ref, gfin_ref, o_ref, o_tail_ref, h0_scr, h1_scr, a0_scr, a1_scr, acc_scr = refs[4 * parts:]
    heads_per_part = PEER_HEADS // parts
    part = PEER_EXPERT_BLOCK // PEER_TABLE_PARTS
    assert part == PEER_PIPE_ROWS
    j = pl.program_id(1)
    last = pl.num_programs(1) - 1
    f32 = jnp.float32
    slots = ((h0_scr, a0_scr), (h1_scr, a1_scr))

    @pl.when(j == 0)
    def _():
        acc_scr[...] = jnp.zeros_like(acc_scr)
        a1_scr[...] = jnp.zeros_like(a1_scr)
        for s, u_ref in enumerate(u_refs):
            h0_scr[s * part:(s + 1) * part, :] = jnp.dot(u_ref[...], hnT_ref[...], preferred_element_type=f32)

    def steady(h_cur, a_cur, h_prv, a_prv):
        tile = PEER_PIPE_ROWS

        def gating(key, lt):
            i1 = (j - 1) * (PEER_EXPERT_BLOCK // N_KEYS) + key
            lanes = slice(lt * LANES, (lt + 1) * LANES)
            shape = (PEER_GATE_ROWS, LANES)
            need, e1 = [], []
            for h in range(PEER_HEADS):
                a_ref, e_ref = a_refs[h // heads_per_part], e_refs[h // heads_per_part]
                hp = 2 * (h % heads_per_part)
                need.append(jnp.broadcast_to(a_ref[hp, lt, pl.ds(i1, 1), :], shape))
                e1.append(jnp.broadcast_to(e_ref[hp, lt, pl.ds(i1, 1), :], shape))
            for r in range(0, N_KEYS, PEER_GATE_ROWS):
                gate = jnp.zeros(shape, f32)
                for h in range(PEER_HEADS):
                    a_ref, e_ref = a_refs[h // heads_per_part], e_refs[h // heads_per_part]
                    hp = 2 * (h % heads_per_part)
                    val = e_ref[hp + 1, lt, r:r + PEER_GATE_ROWS, :] * e1[h]
                    gate = gate + jnp.where(a_ref[hp + 1, lt, r:r + PEER_GATE_ROWS, :] >= need[h], val, 0.0)
                rows = slice(key * N_KEYS + r, key * N_KEYS + r + PEER_GATE_ROWS)
                x = h_prv[rows, lanes]
                act = 0.5 * x * (1.0 + lax.erf(x * INV_SQRT2))
                a_prv[rows, lanes] = (act * gate).astype(jnp.bfloat16)

        def pre_activation(span, cols):
            h_cur[span, cols] = jnp.dot(u_refs[span.start // part][...], hnT_ref[:, cols],
                                        preferred_element_type=f32)

        def accumulate(span, out_rows):
            acc_scr[out_rows, :] += jnp.dot(vT_refs[span.start // part][out_rows, :], a_cur[span, :],
                                            preferred_element_type=f32)

        mxu_work = []
        for s in range(PEER_EXPERT_BLOCK // tile):
            span = slice(s * tile, (s + 1) * tile)
            mxu_work += [functools.partial(pre_activation, span, slice(c * tile, (c + 1) * tile))
                         for c in range(PEER_TOKEN_BLOCK // tile)]
            mxu_work += [functools.partial(accumulate, span, slice(m * tile, (m + 1) * tile))
                         for m in range(D_MODEL // tile)]
        units = [(key, lt) for key in range(PEER_EXPERT_BLOCK // N_KEYS) for lt in range(PEER_TOKEN_BLOCK // LANES)]
        issued = 0
        for n, (key, lt) in enumerate(units):
            while issued * len(units) < (n + 1) * len(mxu_work) and issued < len(mxu_work):
                mxu_work[issued]()
                issued += 1
            gating(key, lt)

    for parity in (0, 1):
        pl.when((j > 0) & (j < last) & (j % 2 == parity))(
            functools.partial(steady, *slots[parity], *slots[1 - parity]))

    @pl.when(j == last)
    def _():
        acc = acc_scr[...]
        for s, vT_ref in enumerate(vT_refs):
            acc = acc + jnp.dot(vT_ref[...], a1_scr[s * part:(s + 1) * part, :], preferred_element_type=f32)
        is_tail = pl.program_id(0) == pl.num_programs(0) - 1
        y = jnp.where(is_tail, res_tail_ref[...], res_ref[...]) + acc.T
        y = y * lax.rsqrt(jnp.mean(y * y, axis=-1, keepdims=True) + NORM_EPS) * gfin_ref[...]

        @pl.when(is_tail)
        def _():
            o_tail_ref[...] = y

        @pl.when(jnp.logical_not(is_tail))
        def _():
            o_ref[...] = y


def _peer_experts(hnT, a, e, res, res_tail, g_final, u_bf, vT_bf):
    t_pad = hnT.shape[1]
    tb, eb = PEER_TOKEN_BLOCK, PEER_EXPERT_BLOCK
    n_blocks = N_EXPERTS // eb
    main_blocks = t_pad // tb - 1
    assert res.shape[0] == main_blocks * tb and res_tail.shape[0] == tb
    main_spec = pl.BlockSpec((tb, D_MODEL), lambda i, j: (jnp.minimum(i, main_blocks - 1), 0))
    tail_spec = pl.BlockSpec((tb, D_MODEL), lambda i, j: (0, 0))
    parts = PEER_TABLE_PARTS
    tok_specs = [pl.BlockSpec((None, 2 * PEER_HEADS // parts, tb // LANES, N_KEYS, LANES),
                              lambda i, j, q=q: (i, q, 0, 0, 0))
                 for q in range(parts)]
    u_specs = [pl.BlockSpec((eb // parts, D_MODEL), lambda i, j, q=q: (jnp.minimum(j, n_blocks - 1) * parts + q, 0))
               for q in range(parts)]
    vT_specs = [pl.BlockSpec((None, None, D_MODEL, eb // parts),
                             lambda i, j, q=q: (jnp.clip(j - 2, 0, n_blocks - 1), q, 0, 0)) for q in range(parts)]
    return pl.pallas_call(
        _peer_expert_body,
        grid=(t_pad // tb, n_blocks + 2),
        in_specs=[
            pl.BlockSpec((D_MODEL, tb), lambda i, j: (0, i)),
            *u_specs, *vT_specs, *tok_specs, *tok_specs,
            main_spec, tail_spec,
            pl.BlockSpec((1, D_MODEL), lambda i, j: (0, 0)),
        ],
        out_specs=[main_spec, tail_spec],
        out_shape=[jax.ShapeDtypeStruct(res.shape, jnp.float32), jax.ShapeDtypeStruct(res_tail.shape, jnp.float32)],
        scratch_shapes=[
            pltpu.VMEM((eb, tb), jnp.float32), pltpu.VMEM((eb, tb), jnp.float32),
            pltpu.VMEM((eb, tb), jnp.bfloat16), pltpu.VMEM((eb, tb), jnp.bfloat16),
            pltpu.VMEM((D_MODEL, tb), jnp.float32),
        ],
        compiler_params=pltpu.CompilerParams(
            dimension_semantics=("arbitrary", "arbitrary"),
            vmem_limit_bytes=PEER_VMEM_LIMIT),
        name="peer_experts",
    )(hnT, *[u_bf] * parts, *[vT_bf] * parts, *[a] * parts, *[e] * parts, res, res_tail,
      g_final.reshape(1, D_MODEL))


def peer_block(h, hnT, h_tail, hnT_tail, w_q, sub_keys, u_tab, v_tab, g_final):
    tb = PEER_TOKEN_BLOCK
    t_tail = h_tail.shape[0]
    assert h.shape[0] % tb == 0 and t_tail <= tb
    hnT_all = jnp.concatenate([hnT, jnp.pad(hnT_tail, ((0, 0), (0, tb - t_tail)))], axis=1)
    a, e = _peer_route(hnT_all, w_q.astype(jnp.bfloat16).T, sub_keys.astype(jnp.bfloat16))
    out, out_tail = _peer_experts(
        hnT_all, a, e, h, jnp.pad(h_tail, ((0, tb - t_tail), (0, 0))), g_final, u_tab.astype(jnp.bfloat16),
        v_tab.astype(jnp.bfloat16).reshape(-1, PEER_TABLE_PARTS, PEER_EXPERT_BLOCK // PEER_TABLE_PARTS,
                                           D_MODEL).transpose(0, 1, 3, 2))
    return out, out_tail[:t_tail]


ATTN_VMEM_LIMIT = 40 * 1024 * 1024
HEADS_PER_TILE = LANES // HEAD_DIM
ATTN_STREAMS = 8


def _band_units(units, sinks):
    f32, bf16 = jnp.float32, jnp.bfloat16
    nt = (((1,), (1,)), ((), ()))
    rows = HEADS_PER_TILE * BLOCK
    qi = lax.broadcasted_iota(jnp.int32, (rows, 2 * BLOCK), 0) % BLOCK
    kj = lax.broadcasted_iota(jnp.int32, (rows, 2 * BLOCK), 1)
    off = BLOCK + qi - kj
    band = (off >= 0) & (off <= BLOCK)
    own_block = kj >= BLOCK
    ss = []
    for qs, load_k, _, key_lanes, _ in units:
        lhs = jnp.concatenate([jnp.where(key_lanes[i], qs[i], 0.0) for i in range(HEADS_PER_TILE)], axis=0)
        ss.append(lax.dot_general(lhs.astype(bf16), load_k().astype(bf16), nt, preferred_element_type=f32))
    ss = [jnp.where(band & (own_block | jnp.logical_not(unit[4])), s, NEG_INF) for s, unit in zip(ss, units)]
    if sinks is not None:
        assert HEADS_PER_TILE == 2
        sink_slot = kj == (qi + BLOCK + 1) % (2 * BLOCK)
        head0 = lax.broadcasted_iota(jnp.int32, (rows, 2 * BLOCK), 0) < BLOCK
        sink_logit = jnp.where(head0, sinks[0], sinks[1])
        ss = [jnp.where(sink_slot, sink_logit, s) for s in ss]
    ms = [jnp.max(s, axis=-1, keepdims=True) for s in ss]
    es = [jnp.exp(s - m) for s, m in zip(ss, ms)]
    denoms = [jnp.sum(e, axis=-1, keepdims=True) for e in es]
    results = []
    for e, d, m, unit in zip(es, denoms, ms, units):
        p = e / d
        if sinks is not None:
            p = jnp.where(sink_slot, 0.0, p)
        o = jnp.dot(p.astype(bf16), unit[2]().astype(bf16), preferred_element_type=f32)
        lse = m + jnp.log(d)
        results.append([(o[i * BLOCK:(i + 1) * BLOCK, :], lse[i * BLOCK:(i + 1) * BLOCK, :])
                        for i in range(HEADS_PER_TILE)])
    return results


def _prompt_attention_body(qa_ref, ka_ref, va_ref, qb_ref, kb_ref, vb_ref, sink_ref, oa_ref, ob_ref,
                           o_scr, lse_scr):
    f32 = jnp.float32
    seq = qa_ref.shape[1]
    lane = lax.broadcasted_iota(jnp.int32, (1, LANES), 1)
    low = lane < HEAD_DIM
    own = [low, jnp.logical_not(low)]

    def window(ref, prev, cur):
        return lambda: jnp.concatenate([ref[0, prev, :], ref[0, cur, :]], axis=0)

    for c, (reach, dil) in enumerate(DILATED_CONFIGS):
        assert reach // dil == BLOCK
        blocks = seq // (dil * BLOCK)

        def step(it, carry, c=c, dil=dil, blocks=blocks):
            units, curs = [], []
            for k in range(ATTN_STREAMS):
                u = it * ATTN_STREAMS + k
                res, blk = u // blocks, u % blocks
                cur = pl.ds(res + dil * BLOCK * blk, BLOCK, stride=dil)
                prev = pl.ds(res + dil * BLOCK * jnp.maximum(blk - 1, 0), BLOCK, stride=dil)
                q = qa_ref[0, cur, :] * SCALE
                units.append(([q, q], window(ka_ref, prev, cur), window(va_ref, prev, cur), own, blk == 0))
                curs.append(cur)
            for cur, ((o0, l0), (o1, l1)) in zip(curs, _band_units(units, None)):
                o_scr[c, cur, :] = jnp.where(low, o0, o1)
                lse_scr[c, cur, :] = jnp.where(low, l0, l1)
            return carry

        lax.fori_loop(0, dil * blocks // ATTN_STREAMS, step, 0)

    def merge(t, carry):
        rows = pl.ds(pl.multiple_of(t * BLOCK, BLOCK), BLOCK)
        ls = [lse_scr[c, rows, :] for c in range(len(DILATED_CONFIGS))]
        top = jnp.maximum(jnp.maximum(ls[0], ls[1]), ls[2])
        ws = [jnp.exp(l - top) for l in ls]
        num = ws[0] * o_scr[0, rows, :] + ws[1] * o_scr[1, rows, :] + ws[2] * o_scr[2, rows, :]
        oa_ref[0, rows, :] = num / (ws[0] + ws[1] + ws[2])
        return carry

    lax.fori_loop(0, seq // BLOCK, merge, 0)

    slab = pl.program_id(1)
    sinks = [sink_ref[slab * HEADS_PER_TILE + i] for i in range(HEADS_PER_TILE)]

    def step_b(it, carry):
        units, curs = [], []
        for k in range(ATTN_STREAMS):
            blk = it * ATTN_STREAMS + k
            cur = pl.ds(pl.multiple_of(blk * BLOCK, BLOCK), BLOCK)
            prev = pl.ds(pl.multiple_of(jnp.maximum(blk - 1, 0) * BLOCK, BLOCK), BLOCK)
            q = qb_ref[0, cur, :] * SCALE
            units.append(([q, q], window(kb_ref, prev, cur), window(vb_ref, prev, cur), own, blk == 0))
            curs.append(cur)
        for cur, ((o0, _), (o1, _)) in zip(curs, _band_units(units, sinks)):
            ob_ref[0, cur, :] = jnp.where(low, o0, o1)
        return carry

    lax.fori_loop(0, seq // BLOCK // ATTN_STREAMS, step_b, 0)


def prompt_attention(qa, ka, va, qb, kb, vb, sink):
    n, seq, _ = qa.shape
    assert kb.shape[-1] == KV_HEADS_B * LANES and seq % (BLOCK * max(d for _, d in DILATED_CONFIGS)) == 0
    slab = pl.BlockSpec((1, seq, LANES), lambda b, p: (b, 0, p))
    whole = pl.BlockSpec((1, seq, LANES), lambda b, p: (b, 0, p * HEADS_PER_TILE // GROUP_B))
    f32 = jnp.float32
    return pl.pallas_call(
        _prompt_attention_body,
        grid=(n, WIDTH_A // LANES),
        in_specs=[slab, slab, slab, slab, whole, whole, pl.BlockSpec(memory_space=pltpu.SMEM)],
        out_specs=[slab, slab],
        out_shape=[jax.ShapeDtypeStruct((n, seq, WIDTH_A), f32), jax.ShapeDtypeStruct((n, seq, WIDTH_B), f32)],
        scratch_shapes=[pltpu.VMEM((len(DILATED_CONFIGS), seq, LANES), f32),
                        pltpu.VMEM((len(DILATED_CONFIGS), seq, LANES), f32)],
        compiler_params=pltpu.CompilerParams(
            dimension_semantics=("parallel", "parallel"),
            vmem_limit_bytes=ATTN_VMEM_LIMIT),
        name="prompt_attention",
    )(qa, ka, va, qb, kb, vb, sink.reshape(HEADS_B))


PROJ_VMEM_LIMIT = 48 * 1024 * 1024
PROJ_TOKEN_BLOCK = 512
QKV_WIDTHS = IN_WIDTHS[:6]
QKV_WIDTH = sum(QKV_WIDTHS)
QKV_ROTATED = (True, True, False, True, True, False)


def _rms_normed(x, g):
    return x * lax.rsqrt(jnp.mean(x * x, axis=-1, keepdims=True) + NORM_EPS) * g


def _rope_slab(x, cos, sin_signed):
    lane = lax.broadcasted_iota(jnp.int32, (1, LANES), 1)
    half = HEAD_DIM // 2
    partner = jnp.where(lane % HEAD_DIM < half, pltpu.roll(x, LANES - half, axis=1), pltpu.roll(x, half, axis=1))
    return x * cos + partner * sin_signed


def _in_proj_body(x_ref, g_ref, w_ref, cos_ref, sin_ref, *out_refs, widths, channel_major):
    xn = _rms_normed(x_ref[...], g_ref[...]).astype(jnp.bfloat16)
    cos, sin = cos_ref[...], sin_ref[...]
    groups = []
    c0 = 0
    for width, rotated in zip(widths, QKV_ROTATED):
        z = jnp.dot(xn, w_ref[:, c0:c0 + width], preferred_element_type=jnp.float32)
        if rotated:
            z = jnp.concatenate([_rope_slab(z[:, c:c + LANES], cos, sin) for c in range(0, width, LANES)], axis=1)
        groups.append(z)
        c0 += width
    for ref, z in zip(out_refs[:6], groups):
        ref[...] = z
    if channel_major:
        kaT_ref, vaT_ref, kbT_ref, vbT_ref = out_refs[6:]
        kaT_ref[0] = groups[1].T
        vaT_ref[0] = groups[2].T
        tb = x_ref.shape[0]
        low = lax.broadcasted_iota(jnp.int32, (1, LANES), 1) < HEAD_DIM
        for ref, z in ((kbT_ref, groups[4]), (vbT_ref, groups[5])):
            tail = z[tb - WINDOW_B:, :]
            ref[0] = jnp.where(low, tail[:, :LANES], tail[:, LANES:]).T


def in_proj(x, g_mix, w_bf, cos, sin, tb, seq):
    t = x.shape[0]
    channel_major = seq is not None
    per_seq = seq // tb if channel_major else None
    widths = QKV_WIDTHS[:4] + ((2 * KV_WIDTH_B,) * 2 if channel_major else QKV_WIDTHS[4:])
    assert w_bf.shape[1] == sum(widths)
    f32 = jnp.float32
    tok = lambda w: pl.BlockSpec((tb, w), lambda i: (i, 0))
    out_specs = [tok(w) for w in widths]
    out_shape = [jax.ShapeDtypeStruct((t, w), f32) for w in widths]
    if channel_major:
        assert seq == WINDOW_A and tb >= WINDOW_B and KV_WIDTH_B == LANES
        n = t // seq
        out_specs += [pl.BlockSpec((1, WIDTH_A, tb), lambda i: (i // per_seq, 0, i % per_seq))] * 2
        out_specs += [pl.BlockSpec((1, KV_WIDTH_B, WINDOW_B), lambda i: (i // per_seq, 0, 0))] * 2
        out_shape += [jax.ShapeDtypeStruct((n, WIDTH_A, seq), f32)] * 2
        out_shape += [jax.ShapeDtypeStruct((n, KV_WIDTH_B, WINDOW_B), f32)] * 2
    return pl.pallas_call(
        functools.partial(_in_proj_body, widths=widths, channel_major=channel_major),
        grid=(t // tb,),
        in_specs=[tok(D_MODEL), pl.BlockSpec((1, D_MODEL), lambda i: (0, 0)),
                  pl.BlockSpec(w_bf.shape, lambda i: (0, 0)), tok(LANES), tok(LANES)],
        out_specs=out_specs,
        out_shape=out_shape,
        compiler_params=pltpu.CompilerParams(
            dimension_semantics=("arbitrary",),
            vmem_limit_bytes=PROJ_VMEM_LIMIT),
        name="in_proj",
    )(x, g_mix.reshape(1, D_MODEL), w_bf, cos, sin)


def duplicate_kv_columns(w_qkv):
    edges = np.cumsum((0,) + QKV_WIDTHS)
    cols = np.arange(KV_WIDTH_B).reshape(KV_HEADS_B, 1, HEAD_DIM)
    cols = np.broadcast_to(cols, (KV_HEADS_B, LANES // HEAD_DIM, HEAD_DIM)).reshape(-1)
    return jnp.concatenate([w_qkv[:, :edges[4]], w_qkv[:, edges[4] + cols], w_qkv[:, edges[5] + cols]], axis=1)


def _out_proj_body(x_ref, oa_ref, ob_ref, gmix_ref, wg_ref, wa_ref, wb_ref, wo_ref, gffn_ref, h_ref, hnT_ref):
    f32, bf16 = jnp.float32, jnp.bfloat16
    x = x_ref[...]
    xn = _rms_normed(x, gmix_ref[...]).astype(bf16)
    ya = jnp.dot(oa_ref[...].astype(bf16), wa_ref[...], preferred_element_type=f32)
    merged = jax.nn.sigmoid(jnp.dot(xn, wg_ref[:, :D_MODEL], preferred_element_type=f32)) * ya
    yb = jnp.dot(ob_ref[...].astype(bf16), wb_ref[...], preferred_element_type=f32)
    merged = merged + jax.nn.sigmoid(jnp.dot(xn, wg_ref[:, D_MODEL:], preferred_element_type=f32)) * yb
    h = x + jnp.dot(merged.astype(bf16), wo_ref[...], preferred_element_type=f32)
    h_ref[...] = h
    hnT_ref[...] = _rms_normed(h, gffn_ref[...]).T.astype(bf16)


def out_proj(x, oa, ob, g_mix, w_gate_bf, w_a_bf, w_b_bf, w_o_bf, g_ffn, tb):
    t = x.shape[0]
    tok = lambda w: pl.BlockSpec((tb, w), lambda i: (i, 0))
    full = lambda a: pl.BlockSpec(a.shape, lambda i: (0, 0))
    g_mix, g_ffn = g_mix.reshape(1, D_MODEL), g_ffn.reshape(1, D_MODEL)
    return pl.pallas_call(
        _out_proj_body,
        grid=(t // tb,),
        in_specs=[tok(D_MODEL), tok(WIDTH_A), tok(WIDTH_B), full(g_mix), full(w_gate_bf), full(w_a_bf),
                  full(w_b_bf), full(w_o_bf), full(g_ffn)],
        out_specs=[tok(D_MODEL), pl.BlockSpec((D_MODEL, tb), lambda i: (0, i))],
        out_shape=[jax.ShapeDtypeStruct((t, D_MODEL), jnp.float32),
                   jax.ShapeDtypeStruct((D_MODEL, t), jnp.bfloat16)],
        compiler_params=pltpu.CompilerParams(
            dimension_semantics=("parallel",),
            vmem_limit_bytes=PROJ_VMEM_LIMIT),
        name="out_proj",
    )(x, oa, ob, g_mix, w_gate_bf, w_a_bf, w_b_bf, w_o_bf, g_ffn)


def rotary_tables(pos):
    inv = ROPE_THETA ** (-jnp.arange(0, HEAD_DIM, 2, dtype=jnp.float32) / HEAD_DIM)
    ang = pos.astype(jnp.float32)[:, None] * inv[None, :]
    cos, sin = jnp.cos(ang), jnp.sin(ang)
    reps = LANES // HEAD_DIM
    return jnp.tile(jnp.concatenate([cos, cos], axis=1), (1, reps)), jnp.tile(jnp.concatenate([-sin, sin], axis=1), (1, reps))


def kernel(x_prompt, x_sample, cache_a_k, cache_a_v, cache_b_k, cache_b_v, norm_mix, w_in,
           w_branch_a, w_branch_b, w_out, sink_b, norm_ffn, w_peer_q, peer_sub_keys, peer_u,
           peer_v, norm_final):
    assert DEPTH == 1
    bf16 = jnp.bfloat16
    n, seq, _ = x_prompt.shape
    ns, dec = x_sample.shape[:2]
    assert dec == 1
    w_in_bf = w_in[0].astype(bf16)
    w_qkv, w_gate = w_in_bf[:, :QKV_WIDTH], w_in_bf[:, QKV_WIDTH:]
    proj_weights = (norm_mix[0], w_gate, w_branch_a[0].astype(bf16), w_branch_b[0].astype(bf16),
                    w_out[0].astype(bf16), norm_ffn[0])
    sink = sink_b[0].astype(jnp.float32)

    xp = x_prompt.reshape(n * seq, D_MODEL)
    cos, sin = rotary_tables(jnp.tile(jnp.arange(seq), n))
    qa, ka, va, qb, kb, vb, ka_t, va_t, kb_t, vb_t = in_proj(xp, norm_mix[0], duplicate_kv_columns(w_qkv), cos, sin,
                                                             PROJ_TOKEN_BLOCK, seq)
    per_seq = lambda a: a.reshape(n, seq, a.shape[-1])
    oa, ob = prompt_attention(per_seq(qa), per_seq(ka), per_seq(va), per_seq(qb), per_seq(kb), per_seq(vb), sink)
    hp, hnt_p = out_proj(xp, oa.reshape(n * seq, WIDTH_A), ob.reshape(n * seq, WIDTH_B), *proj_weights,
                         PROJ_TOKEN_BLOCK)
    windows = lambda a, heads: a.reshape(n, heads, HEAD_DIM, a.shape[-1]).transpose(0, 3, 1, 2)[None]
    state_p = (windows(ka_t, HEADS_A), windows(va_t, HEADS_A), windows(kb_t, KV_HEADS_B), windows(vb_t, KV_HEADS_B))

    xs = x_sample.reshape(ns, D_MODEL)
    cos, sin = rotary_tables(jnp.full((ns,), PAST_LEN))
    qa, ka, va, qb, kb, vb = in_proj(xs, norm_mix[0], w_qkv, cos, sin, ns, None)
    oa, ob, state_s = sample_mixers(qa, ka, va, qb, kb, vb, sink, cache_a_k[0], cache_a_v[0],
                                    cache_b_k[0], cache_b_v[0])
    hs, hnt_s = out_proj(xs, oa, ob, *proj_weights, ns)

    y_prompt, y_sample = peer_block(hp, hnt_p, hs, hnt_s, w_peer_q[0], peer_sub_keys[0], peer_u[0], peer_v[0],
                                    norm_final)
    return (y_prompt.reshape(x_prompt.shape), y_sample.reshape(x_sample.shape), *state_p,
            *[a[None] for a in state_s])
```
